```python
import jax, jax.numpy as jnp
from jax import lax
import numpy as np


D_MODEL = 1024
BATCH = 4
SEQ = 8192
DEPTH = 1
DEC_BATCH = 32
DEC_SEQ = 16
PAST_LEN = 2048

CHUNK = 64
N_HEADS = 8
HEAD_DIM = 64
ATT_WIDTH = N_HEADS * HEAD_DIM
D_CONV = 512
CONV_WIDTH = 31
CONV_CTX = CONV_WIDTH - 1
QBLOCK = 128
N_GROUPS = 4
EXPERTS_PER_GROUP = 8
N_EXPERTS = N_GROUPS * EXPERTS_PER_GROUP
TOP_K = 2
D_EXPERT = 256
N_PROJ = 3 * ATT_WIDTH + N_HEADS + 2 * D_CONV + 2 * D_MODEL
SPLITS = (ATT_WIDTH, 2 * ATT_WIDTH, 3 * ATT_WIDTH, 3 * ATT_WIDTH + N_HEADS,
          3 * ATT_WIDTH + N_HEADS + 2 * D_CONV)
ALPHA = (2 * DEPTH) ** 0.25
BETA = (8 * DEPTH) ** -0.25
LN_EPS = 1e-5
NEG = -1e30
FORGET_BIAS_INIT = 3.0

kernel_name = "fox_conformer_hmoe_stream_step"

f32 = jnp.float32


def _layernorm(x, g, b):
    xf = x.astype(f32)
    mu = jnp.mean(xf, axis=-1, keepdims=True)
    xc = xf - mu
    var = jnp.mean(xc * xc, axis=-1, keepdims=True)
    y = xc * lax.rsqrt(var + LN_EPS) * g.astype(f32) + b.astype(f32)
    return y.astype(x.dtype)


def _in_proj(x, w_in, b_forget):
    p = x @ w_in
    q, k, v, fl, glu, gl = jnp.split(p, SPLITS, axis=-1)
    lead = x.shape[:-1]
    q = q.reshape(*lead, N_HEADS, HEAD_DIM)
    k = k.reshape(*lead, N_HEADS, HEAD_DIM)
    v = v.reshape(*lead, N_HEADS, HEAD_DIM)
    logf = jax.nn.log_sigmoid(fl.astype(f32) + b_forget.astype(f32))
    u = glu[..., :D_CONV] * jax.nn.sigmoid(glu[..., D_CONV:])
    return q, k, v, logf, u, gl


def _fox_prompt(q, k, v, logf):
    B, S, H, Dh = q.shape
    nb = S // QBLOCK
    scale = HEAD_DIM ** -0.5
    c = jnp.cumsum(logf, axis=1)
    ck = c.transpose(0, 2, 1)
    kf = k.astype(f32)
    vf = v.astype(f32)
    qb = q.reshape(B, nb, QBLOCK, H, Dh).transpose(1, 0, 2, 3, 4)
    cb = ck.reshape(B, H, nb, QBLOCK).transpose(2, 0, 1, 3)
    kpos = jnp.arange(S)

    def block(args):
        q_blk, c_blk, i = args
        s = jnp.einsum('bqhd,bkhd->bhqk', q_blk.astype(f32), kf) * scale
        s = s + c_blk[..., :, None] - ck[:, :, None, :]
        qpos = i * QBLOCK + jnp.arange(QBLOCK)
        mask = kpos[None, :] <= qpos[:, None]
        pr = jax.nn.softmax(jnp.where(mask, s, NEG), axis=-1)
        return jnp.einsum('bhqk,bkhd->bqhd', pr, vf)

    o = lax.map(block, (qb, cb, jnp.arange(nb)))
    return o.transpose(1, 0, 2, 3, 4).reshape(B, S, ATT_WIDTH)


def _fox_sample(q, k, v, logf, ck, cv, clogf):
    B, T = q.shape[0], q.shape[1]
    P = ck.shape[1]
    scale = HEAD_DIM ** -0.5
    qf = q.astype(f32)
    cq = jnp.cumsum(logf, axis=1).transpose(0, 2, 1)
    clf = clogf.astype(f32)
    r = (lax.cumsum(clf, axis=1, reverse=True) - clf).transpose(0, 2, 1)
    s_c = jnp.einsum('bqhd,bkhd->bhqk', qf, ck.astype(f32)) * scale + cq[..., :, None] + r[:, :, None, :]
    s_n = jnp.einsum('bqhd,bkhd->bhqk', qf, k.astype(f32)) * scale + cq[..., :, None] - cq[..., None, :]
    causal = jnp.tril(jnp.ones((T, T), dtype=bool))
    s_n = jnp.where(causal, s_n, NEG)
    pr = jax.nn.softmax(jnp.concatenate([s_c, s_n], axis=-1), axis=-1)
    o = (jnp.einsum('bhqk,bkhd->bqhd', pr[..., :P], cv.astype(f32))
         + jnp.einsum('bhqk,bkhd->bqhd', pr[..., P:], v.astype(f32)))
    return o.reshape(B, T, ATT_WIDTH)


def _causal_depthwise(u_ext, w_dw, b_dw):
    out = lax.conv_general_dilated(
        u_ext, w_dw[:, None, :].astype(u_ext.dtype), window_strides=(1,), padding='VALID',
        dimension_numbers=('NWC', 'WIO', 'NWC'), feature_group_count=D_CONV)
    return out + b_dw


def _hier_moe(h, w_group, b_group, w_router, b_router, w1, w3, w2):
    n = h.shape[0]
    gl = (h @ w_group).astype(f32) + b_group.astype(f32)
    pg = jax.nn.softmax(gl, axis=-1)
    grp = jnp.argmax(gl, axis=-1)
    p_sel = jnp.take_along_axis(pg, grp[:, None], axis=1)
    el = ((h @ w_router).astype(f32) + b_router.astype(f32)).reshape(n, N_GROUPS, EXPERTS_PER_GROUP)
    el = jnp.take_along_axis(el, grp[:, None, None], axis=1)[:, 0]
    top_v, top_i = lax.top_k(el, TOP_K)
    wts = jax.nn.softmax(top_v, axis=-1) * p_sel
    eidx = grp[:, None] * EXPERTS_PER_GROUP + top_i
    gates = jnp.einsum('nk,nke->ne', wts, jax.nn.one_hot(eidx, N_EXPERTS, dtype=f32))

    def expert_step(acc, xs):
        g_e, w1_e, w3_e, w2_e = xs
        y_e = (jax.nn.silu(h @ w1_e) * (h @ w3_e)) @ w2_e
        return acc + g_e[:, None] * y_e.astype(f32), None

    acc, _ = lax.scan(expert_step, jnp.zeros((n, D_MODEL), f32), (gates.T, w1, w3, w2))
    return acc.astype(h.dtype)


def _tail(x, o_attn, u_ext, gate_logits, w_dw, b_dw, g_cln, b_cln, w_attn_out, w_conv_out,
          w_out, g_ln1, b_ln1, w_group, b_group, w_router, b_router, w1, w3, w2, g_ln2, b_ln2):
    a = o_attn.astype(x.dtype) @ w_attn_out
    c = _causal_depthwise(u_ext, w_dw, b_dw)
    c = jax.nn.silu(_layernorm(c, g_cln, b_cln))
    cb = c @ w_conv_out
    g = jax.nn.sigmoid(gate_logits.astype(f32)).astype(x.dtype)
    mix = (g[..., :D_MODEL] * a + g[..., D_MODEL:] * cb) @ w_out
    h = _layernorm(ALPHA * x + mix, g_ln1, b_ln1)
    m = _hier_moe(h.reshape(-1, D_MODEL), w_group, b_group, w_router, b_router, w1, w3, w2)
    return _layernorm(ALPHA * h + m.reshape(h.shape), g_ln2, b_ln2)


def setup_inputs(seed: int = 0) -> dict:
    key = jax.random.key(seed)
    ks = jax.random.split(key, 32)
    nrm = lambda k, s: jax.random.normal(k, s, f32)
    L, D = DEPTH, D_MODEL
    sd = D ** -0.5
    w_in = jnp.concatenate([
        nrm(ks[0], (L, D, ATT_WIDTH)) * sd,
        nrm(ks[1], (L, D, ATT_WIDTH)) * sd,
        nrm(ks[2], (L, D, ATT_WIDTH)) * sd * BETA,
        nrm(ks[3], (L, D, N_HEADS)) * sd,
        nrm(ks[4], (L, D, 2 * D_CONV)) * sd,
        nrm(ks[5], (L, D, 2 * D_MODEL)) * sd,
    ], axis=-1)
    return {
        'x_prompt': nrm(ks[6], (BATCH, SEQ, D)),
        'x_sample': nrm(ks[7], (DEC_BATCH, DEC_SEQ, D)),
        'cache_k': nrm(ks[8], (L, DEC_BATCH, PAST_LEN, N_HEADS, HEAD_DIM)),
        'cache_v': nrm(ks[9], (L, DEC_BATCH, PAST_LEN, N_HEADS, HEAD_DIM)) * BETA,
        'cache_logf': jax.nn.log_sigmoid(FORGET_BIAS_INIT + nrm(ks[10], (L, DEC_BATCH, PAST_LEN, N_HEADS))),
        'cache_conv': nrm(ks[11], (L, DEC_BATCH, CONV_CTX, D_CONV)) * 0.5,
        'w_in': w_in,
        'b_forget': FORGET_BIAS_INIT + 0.1 * nrm(ks[12], (L, N_HEADS)),
        'w_dw': nrm(ks[13], (L, CONV_WIDTH, D_CONV)) * CONV_WIDTH ** -0.5,
        'b_dw': 0.02 * nrm(ks[14], (L, D_CONV)),
        'g_conv_ln': 1.0 + 0.02 * nrm(ks[15], (L, D_CONV)),
        'b_conv_ln': 0.02 * nrm(ks[16], (L, D_CONV)),
        'w_attn_out': nrm(ks[17], (L, ATT_WIDTH, D)) * ATT_WIDTH ** -0.5 * BETA,
        'w_conv_out': nrm(ks[18], (L, D_CONV, D)) * D_CONV ** -0.5 * BETA,
        'w_out': nrm(ks[19], (L, D, D)) * sd * BETA,
        'g_ln1': 1.0 + 0.02 * nrm(ks[20], (L, D)),
        'b_ln1': 0.02 * nrm(ks[21], (L, D)),
        'w_group': nrm(ks[22], (L, D, N_GROUPS)) * sd,
        'b_group': 0.01 * nrm(ks[23], (L, N_GROUPS)),
        'w_router': nrm(ks[24], (L, D, N_EXPERTS)) * sd,
        'b_router': 0.01 * nrm(ks[25], (L, N_EXPERTS)),
        'w1': nrm(ks[26], (L, N_EXPERTS, D, D_EXPERT)) * sd,
        'w3': nrm(ks[27], (L, N_EXPERTS, D, D_EXPERT)) * sd,
        'w2': nrm(ks[28], (L, N_EXPERTS, D_EXPERT, D)) * D_EXPERT ** -0.5 * BETA,
        'g_ln2': 1.0 + 0.02 * nrm(ks[29], (L, D)),
        'b_ln2': 0.02 * nrm(ks[30], (L, D)),
    }


def reference(x_prompt, x_sample, cache_k, cache_v, cache_logf, cache_conv, w_in, b_forget,
              w_dw, b_dw, g_conv_ln, b_conv_ln, w_attn_out, w_conv_out, w_out, g_ln1, b_ln1,
              w_group, b_group, w_router, b_router, w1, w3, w2, g_ln2, b_ln2):
    xp, xs = x_prompt, x_sample
    kp_l, vp_l, fp_l, cp_l = [], [], [], []
    ks_l, vs_l, fs_l, cs_l = [], [], [], []
    for l in range(DEPTH):
        q, k, v, logf, u, gl = _in_proj(xp, w_in[l], b_forget[l])
        o = _fox_prompt(q, k, v, logf)
        u_ext = jnp.pad(u, ((0, 0), (CONV_CTX, 0), (0, 0)))
        yp = _tail(xp, o, u_ext, gl, w_dw[l], b_dw[l], g_conv_ln[l], b_conv_ln[l], w_attn_out[l],
                   w_conv_out[l], w_out[l], g_ln1[l], b_ln1[l], w_group[l], b_group[l],
                   w_router[l], b_router[l], w1[l], w3[l], w2[l], g_ln2[l], b_ln2[l])
        kp_l.append(k)
        vp_l.append(v)
        fp_l.append(logf)
        cp_l.append(u[:, -CONV_CTX:])
        q, k, v, logf, u, gl = _in_proj(xs, w_in[l], b_forget[l])
        o = _fox_sample(q, k, v, logf, cache_k[l], cache_v[l], cache_logf[l])
        u_ext = jnp.concatenate([cache_conv[l].astype(u.dtype), u], axis=1)
        ys = _tail(xs, o, u_ext, gl, w_dw[l], b_dw[l], g_conv_ln[l], b_conv_ln[l], w_attn_out[l],
                   w_conv_out[l], w_out[l], g_ln1[l], b_ln1[l], w_group[l], b_group[l],
                   w_router[l], b_router[l], w1[l], w3[l], w2[l], g_ln2[l], b_ln2[l])
        ks_l.append(k)
        vs_l.append(v)
        fs_l.append(logf)
        cs_l.append(u_ext[:, -CONV_CTX:])
        xp, xs = yp, ys
    return (xp, xs,
            jnp.stack(kp_l), jnp.stack(vp_l), jnp.stack(fp_l), jnp.stack(cp_l),
            jnp.stack(ks_l), jnp.stack(vs_l), jnp.stack(fs_l), jnp.stack(cs_l))
```

```python
import functools

import jax
import jax.numpy as jnp
from jax import lax
from jax.experimental import pallas as pl
from jax.experimental.pallas import tpu as pltpu

f32 = jnp.float32
bf16 = jnp.bfloat16

D_MODEL = 1024
N_HEADS = 8
HEAD_DIM = 64
ATT_WIDTH = N_HEADS * HEAD_DIM
D_CONV = 512
CONV_WIDTH = 31
CONV_CTX = CONV_WIDTH - 1
N_GROUPS = 4
EXPERTS_PER_GROUP = 8
N_EXPERTS = N_GROUPS * EXPERTS_PER_GROUP
D_EXPERT = 256
DEPTH = 1
ALPHA = (2 * DEPTH) ** 0.25
LN_EPS = 1e-5
NEG = -1e30
SCALE = HEAD_DIM ** -0.5

LANES = 128
HALO = 32
HALO_PAD = HALO - CONV_CTX
EXPERT_LANE0 = N_GROUPS
VMEM_LIMIT = 56 * 1024 * 1024

_C_Q, _C_K, _C_V, _C_GA, _C_GB, _C_F, _C_END = 0, 512, 1024, 1536, 2048, 2560, 2688


def _sigmoid(x):
    return 1.0 / (1.0 + jnp.exp(-x))


def _layernorm(x, g, b):
    mu = jnp.mean(x, axis=-1, keepdims=True)
    xc = x - mu
    var = jnp.mean(xc * xc, axis=-1, keepdims=True)
    return xc * lax.rsqrt(var + LN_EPS) * g + b


def _params(*sem):
    return pltpu.CompilerParams(dimension_semantics=sem, vmem_limit_bytes=VMEM_LIMIT)


def _in_proj_kernel(x_ref, w_ref, bf_ref, q_ref, k_ref, kb_ref, v_ref, vb_ref, lf_ref, u_ref):
    xb = x_ref[...].astype(bf16)

    def proj(lo, hi):
        return jnp.dot(xb, w_ref[:, lo:hi], preferred_element_type=f32)

    q_ref[...] = (proj(_C_Q, _C_K) * SCALE).astype(bf16)
    k = proj(_C_K, _C_V)
    k_ref[...] = k
    kb_ref[...] = k.astype(bf16)
    v = proj(_C_V, _C_GA)
    v_ref[...] = v
    vb_ref[...] = v.astype(bf16)
    u_ref[...] = proj(_C_GA, _C_GB) * _sigmoid(proj(_C_GB, _C_F))
    z = proj(_C_F, _C_END) + bf_ref[...]
    lf_ref[...] = jnp.minimum(z, 0.0) - jnp.log(1.0 + jnp.exp(-jnp.abs(z)))


def _in_proj(x2d, w_pack, bf_pad, tm):
    n = x2d.shape[0]
    row = lambda width: pl.BlockSpec((tm, width), lambda i: (i, 0))
    return pl.pallas_call(
        _in_proj_kernel,
        grid=(n // tm,),
        in_specs=[row(D_MODEL),
                  pl.BlockSpec((D_MODEL, _C_END), lambda i: (0, 0)),
                  pl.BlockSpec((1, LANES), lambda i: (0, 0))],
        out_specs=[row(ATT_WIDTH), row(ATT_WIDTH), row(ATT_WIDTH), row(ATT_WIDTH), row(ATT_WIDTH),
                   row(LANES), row(D_CONV)],
        out_shape=[jax.ShapeDtypeStruct((n, ATT_WIDTH), bf16),
                   jax.ShapeDtypeStruct((n, ATT_WIDTH), f32),
                   jax.ShapeDtypeStruct((n, ATT_WIDTH), bf16),
                   jax.ShapeDtypeStruct((n, ATT_WIDTH), f32),
                   jax.ShapeDtypeStruct((n, ATT_WIDTH), bf16),
                   jax.ShapeDtypeStruct((n, LANES), f32),
                   jax.ShapeDtypeStruct((n, D_CONV), f32)],
        compiler_params=_params("parallel"),
        name="in_proj",
    )(x2d, w_pack, bf_pad)


def _split3(x):
    hi = x.astype(bf16)
    r1 = x - hi.astype(f32)
    mid = r1.astype(bf16)
    lo = (r1 - mid.astype(f32)).astype(bf16)
    return hi, mid, lo


def _cumsum_kernel(x_ref, o_ref, *, reverse):
    nc, rows, _ = x_ref.shape
    r = lax.broadcasted_iota(jnp.int32, (LANES, LANES), 0)
    c = lax.broadcasted_iota(jnp.int32, (LANES, LANES), 1)
    tri = jnp.where((r > c) if reverse else (r <= c), 1.0, 0.0).astype(bf16)
    carry = jnp.zeros((rows, 1), f32)
    for j in (range(nc - 1, -1, -1) if reverse else range(nc)):
        xj = x_ref[j]
        y = carry
        for part in _split3(xj):
            y = y + jnp.dot(part, tri, preferred_element_type=f32)
        o_ref[j] = y
        carry = carry + jnp.sum(xj, axis=-1, keepdims=True)


def _cumsum_lanes(x, reverse):
    rows, s = x.shape
    nc = s // LANES
    xc = x.reshape(rows, nc, LANES).transpose(1, 0, 2)
    out = pl.pallas_call(
        functools.partial(_cumsum_kernel, reverse=reverse),
        out_shape=jax.ShapeDtypeStruct((nc, rows, LANES), f32),
        name="cumsum_rev" if reverse else "cumsum_fwd",
    )(xc)
    return out.transpose(1, 0, 2).reshape(rows, s)


def _attn_kernel(q_ref, k_ref, v_ref, nc_ref, o_ref, m_sc, l_sc, acc_sc, *, tq):
    i = pl.program_id(2)
    lane = lax.broadcasted_iota(jnp.int32, (1, LANES), 1)
    qf = q_ref[0].astype(f32)
    qh = [jnp.where(lane < HEAD_DIM, qf, 0.0).astype(bf16), jnp.where(lane >= HEAD_DIM, qf, 0.0).astype(bf16)]
    m_sc[...] = jnp.full(m_sc.shape, NEG, f32)
    l_sc[...] = jnp.zeros(l_sc.shape, f32)
    acc_sc[...] = jnp.zeros(acc_sc.shape, f32)

    def step(j, masked):
        start = pl.multiple_of(j * tq, tq)
        ks = k_ref[0, pl.ds(start, tq), :]
        vs = v_ref[0, pl.ds(start, tq), :]
        bias = nc_ref[0, 0, j]
        for h in range(2):
            s = lax.dot_general(qh[h], ks, (((1,), (1,)), ((), ())), preferred_element_type=f32)
            s = s + bias[h:h + 1, :]
            if masked:
                row = lax.broadcasted_iota(jnp.int32, (tq, tq), 0)
                col = lax.broadcasted_iota(jnp.int32, (tq, tq), 1)
                s = jnp.where(col <= row, s, NEG)
            m_prev = m_sc[h]
            m_new = jnp.maximum(m_prev, jnp.max(s, axis=-1, keepdims=True))
            alpha = jnp.exp(m_prev - m_new)
            p = jnp.exp(s - m_new)
            l_sc[h] = alpha * l_sc[h] + jnp.sum(p, axis=-1, keepdims=True)
            acc_sc[h] = alpha * acc_sc[h] + jnp.dot(p.astype(bf16), vs, preferred_element_type=f32)
            m_sc[h] = m_new

    def body(j, carry):
        step(j, False)
        return carry

    lax.fori_loop(0, i, body, 0)
    step(i, True)
    o0 = acc_sc[0] / l_sc[0]
    o1 = acc_sc[1] / l_sc[1]
    o_ref[0] = jnp.where(lane < HEAD_DIM, o0, o1).astype(bf16)


def _attn_prompt(q, kb, vb, negc, tq):
    b, s, _ = q.shape
    hp = ATT_WIDTH // LANES
    nq = s // tq
    return pl.pallas_call(
        functools.partial(_attn_kernel, tq=tq),
        grid=(b, hp, nq),
        in_specs=[pl.BlockSpec((1, tq, LANES), lambda b_, h_, i: (b_, i, h_)),
                  pl.BlockSpec((1, s, LANES), lambda b_, h_, i: (b_, 0, h_)),
                  pl.BlockSpec((1, s, LANES), lambda b_, h_, i: (b_, 0, h_)),
                  pl.BlockSpec((1, 1, nq, 2, tq), lambda b_, h_, i: (b_, h_, 0, 0, 0))],
        out_specs=pl.BlockSpec((1, tq, LANES), lambda b_, h_, i: (b_, i, h_)),
        out_shape=jax.ShapeDtypeStruct((b, s, ATT_WIDTH), bf16),
        scratch_shapes=[pltpu.VMEM((2, tq, 1), f32), pltpu.VMEM((2, tq, 1), f32),
                        pltpu.VMEM((2, tq, LANES), f32)],
        compiler_params=_params("parallel", "parallel", "arbitrary"),
        name="attn_prompt",
    )(q, kb, vb, negc)


def _attn_sample_kernel(qbd_ref, ck_ref, cv_ref, kn_ref, vn_ref, bc_ref, bn_ref, o_ref, *, t_new):
    qbd = qbd_ref[0]
    s_c = jnp.dot(ck_ref[0].astype(bf16), qbd, preferred_element_type=f32) + bc_ref[0]
    s_n = jnp.dot(kn_ref[0], qbd, preferred_element_type=f32) + bn_ref[0]
    m = jnp.maximum(jnp.max(s_c, axis=0, keepdims=True), jnp.max(s_n, axis=0, keepdims=True))
    p_c = jnp.exp(s_c - m)
    p_n = jnp.exp(s_n - m)
    inv = 1.0 / (jnp.sum(p_c, axis=0, keepdims=True) + jnp.sum(p_n, axis=0, keepdims=True))
    p_c = (p_c * inv).T.astype(bf16)
    p_n = (p_n * inv).T.astype(bf16)
    o_all = (jnp.dot(p_c, cv_ref[0].astype(bf16), preferred_element_type=f32)
             + jnp.dot(p_n, vn_ref[0], preferred_element_type=f32))
    head_of_lane = lax.broadcasted_iota(jnp.int32, (t_new, ATT_WIDTH), 1) // HEAD_DIM
    o = jnp.zeros((t_new, ATT_WIDTH), f32)
    for h in range(N_HEADS):
        o = o + jnp.where(head_of_lane == h, o_all[h * t_new:(h + 1) * t_new, :], 0.0)
    o_ref[0] = o.astype(bf16)


def _attn_sample(qbd, ck, cv, kn, vn, bias_c, bias_n, t_new):
    b, p, _ = ck.shape
    blk = lambda r, c: pl.BlockSpec((1, r, c), lambda i: (i, 0, 0))
    return pl.pallas_call(
        functools.partial(_attn_sample_kernel, t_new=t_new),
        grid=(b,),
        in_specs=[blk(ATT_WIDTH, LANES), blk(p, ATT_WIDTH), blk(p, ATT_WIDTH), blk(LANES, ATT_WIDTH),
                  blk(LANES, ATT_WIDTH), blk(p, LANES), blk(LANES, LANES)],
        out_specs=blk(t_new, ATT_WIDTH),
        out_shape=jax.ShapeDtypeStruct((b, t_new, ATT_WIDTH), bf16),
        compiler_params=_params("parallel"),
        name="attn_sample",
    )(qbd, ck, cv, kn, vn, bias_c, bias_n)


def _route(logits):
    lane = lax.broadcasted_iota(jnp.int32, logits.shape, 1).astype(f32)
    ninf = -jnp.inf
    gl = jnp.where(lane < N_GROUPS, logits, ninf)
    gmax = jnp.max(gl, axis=-1, keepdims=True)
    grp = jnp.min(jnp.where(gl == gmax, lane, float(LANES)), axis=-1, keepdims=True)
    p_sel = 1.0 / jnp.sum(jnp.exp(gl - gmax), axis=-1, keepdims=True)
    e_lo = EXPERT_LANE0 + grp * EXPERTS_PER_GROUP
    el = jnp.where(lane >= e_lo, jnp.where(lane < e_lo + EXPERTS_PER_GROUP, logits, ninf), ninf)
    t1 = jnp.max(el, axis=-1, keepdims=True)
    i1 = jnp.min(jnp.where(el == t1, lane, float(LANES)), axis=-1, keepdims=True)
    el2 = jnp.where(lane == i1, ninf, el)
    t2 = jnp.max(el2, axis=-1, keepdims=True)
    i2 = jnp.min(jnp.where(el2 == t2, lane, float(LANES)), axis=-1, keepdims=True)
    e2 = jnp.exp(t2 - t1)
    w1 = p_sel / (1.0 + e2)
    w2 = p_sel * e2 / (1.0 + e2)
    return jnp.where(lane == i1, w1, 0.0) + jnp.where(lane == i2, w2, 0.0)


def _tail_kernel(x_ref, o_ref, u_ref, uprev_ref, ctx_ref, wdw_ref, bdw_ref, gcl_ref, bcl_ref,
                 wa_ref, wc_ref, wg_ref, wo_ref, g1_ref, b1_ref, wr_ref, br_ref,
                 h_ref, gates_ref, uext_sc, c_sc, *, cb, ct):
    i = pl.program_id(1)
    bb, t, _ = x_ref.shape
    n = bb * t

    uext_sc[:, 0:HALO, :] = jnp.where(i == 0, ctx_ref[...], uprev_ref[...])
    uext_sc[:, HALO:HALO + t, :] = u_ref[...]
    for b0 in range(0, bb, cb):
        for t0 in range(0, t, ct):
            acc = jnp.zeros((cb, ct, D_CONV), f32) + bdw_ref[...]
            for k in range(CONV_WIDTH):
                r0 = HALO_PAD + k + t0
                acc = acc + wdw_ref[k:k + 1, :] * uext_sc[b0:b0 + cb, r0:r0 + ct, :]
            c_sc[b0:b0 + cb, t0:t0 + ct, :] = acc

    c = _layernorm(c_sc[...].reshape(n, D_CONV), gcl_ref[...], bcl_ref[...])
    c = c * _sigmoid(c)
    branch_b = jnp.dot(c.astype(bf16), wc_ref[...], preferred_element_type=f32)
    branch_a = jnp.dot(o_ref[...].reshape(n, ATT_WIDTH), wa_ref[...], preferred_element_type=f32)
    x = x_ref[...].reshape(n, D_MODEL)
    g = _sigmoid(jnp.dot(x.astype(bf16), wg_ref[...], preferred_element_type=f32))
    merged = g[:, :D_MODEL] * branch_a + g[:, D_MODEL:] * branch_b
    mix = jnp.dot(merged.astype(bf16), wo_ref[...], preferred_element_type=f32)
    h = _layernorm(ALPHA * x + mix, g1_ref[...], b1_ref[...])
    h_ref[...] = h.reshape(bb, t, D_MODEL)
    logits = jnp.dot(h, wr_ref[...], preferred_element_type=f32, precision=lax.Precision.HIGHEST) + br_ref[...]
    gates_ref[...] = _route(logits).reshape(bb, t, LANES)


def _tail(x, o, u, ctx, w, bb, t, cb, ct):
    b, s, _ = x.shape
    nt = s // t
    uprev = u if nt > 1 else ctx
    per_tile = t // HALO
    tile = lambda width: pl.BlockSpec((bb, t, width), lambda b_, i: (b_, i, 0))
    full = lambda a: pl.BlockSpec(a.shape, lambda b_, i: (0,) * a.ndim)
    weights = [w["w_dw"], w["b_dw"], w["g_cln"], w["b_cln"], w["w_attn_out"], w["w_conv_out"], w["w_gate"],
               w["w_out"], w["g_ln1"], w["b_ln1"], w["w_route"], w["b_route"]]
    return pl.pallas_call(
        functools.partial(_tail_kernel, cb=cb, ct=ct),
        grid=(b // bb, nt),
        in_specs=[tile(D_MODEL), tile(ATT_WIDTH), tile(D_CONV),
                  pl.BlockSpec((bb, HALO, D_CONV), lambda b_, i: (b_, jnp.maximum(i * per_tile - 1, 0), 0)),
                  pl.BlockSpec((bb, HALO, D_CONV), lambda b_, i: (b_, 0, 0))]
                 + [full(a) for a in weights],
        out_specs=[tile(D_MODEL), tile(LANES)],
        out_shape=[jax.ShapeDtypeStruct((b, s, D_MODEL), f32), jax.ShapeDtypeStruct((b, s, LANES), f32)],
        scratch_shapes=[pltpu.VMEM((bb, HALO + t, D_CONV), f32), pltpu.VMEM((bb, t, D_CONV), f32)],
        compiler_params=_params("parallel", "arbitrary"),
        name="tail",
    )(x, o, u, uprev, ctx, *weights)


def _moe_kernel(h_ref, g_ref, w1_ref, w3_ref, w2_ref, g2_ref, b2_ref, y_ref, acc_sc, hb_sc):
    e = pl.program_id(1)

    @pl.when(e == 0)
    def _():
        acc_sc[...] = jnp.zeros(acc_sc.shape, f32)
        hb_sc[...] = h_ref[...].astype(bf16)

    hb = hb_sc[...]
    a1 = jnp.dot(hb, w1_ref[0], preferred_element_type=f32)
    a3 = jnp.dot(hb, w3_ref[0], preferred_element_type=f32)
    act = (a1 * _sigmoid(a1)) * a3
    ye = jnp.dot(act.astype(bf16), w2_ref[0], preferred_element_type=f32)
    lane = lax.broadcasted_iota(jnp.int32, g_ref.shape, 1)
    ge = jnp.sum(jnp.where(lane == e + EXPERT_LANE0, g_ref[...], 0.0), axis=-1, keepdims=True)
    acc_sc[...] += ge * ye

    @pl.when(e == N_EXPERTS - 1)
    def _():
        y_ref[...] = _layernorm(ALPHA * h_ref[...] + acc_sc[...], g2_ref[...], b2_ref[...])


def _moe(h2d, gates, w1, w3, w2, g2, b2, tm):
    n = h2d.shape[0]
    return pl.pallas_call(
        _moe_kernel,
        grid=(n // tm, N_EXPERTS),
        in_specs=[pl.BlockSpec((tm, D_MODEL), lambda i, e: (i, 0)),
                  pl.BlockSpec((tm, LANES), lambda i, e: (i, 0)),
                  pl.BlockSpec((1, D_MODEL, D_EXPERT), lambda i, e: (e, 0, 0)),
                  pl.BlockSpec((1, D_MODEL, D_EXPERT), lambda i, e: (e, 0, 0)),
                  pl.BlockSpec((1, D_EXPERT, D_MODEL), lambda i, e: (e, 0, 0)),
                  pl.BlockSpec((1, D_MODEL), lambda i, e: (0, 0)),
                  pl.BlockSpec((1, D_MODEL), lambda i, e: (0, 0))],
        out_specs=pl.BlockSpec((tm, D_MODEL), lambda i, e: (i, 0)),
        out_shape=jax.ShapeDtypeStruct((n, D_MODEL), f32),
        scratch_shapes=[pltpu.VMEM((tm, D_MODEL), f32), pltpu.VMEM((tm, D_MODEL), bf16)],
        compiler_params=_params("parallel", "arbitrary"),
        name="moe",
    )(h2d, gates, w1, w3, w2, g2, b2)


def _pack_weights(w_in, b_forget, w_dw, b_dw, g_conv_ln, b_conv_ln, w_attn_out, w_conv_out, w_out,
                  g_ln1, b_ln1, w_group, b_group, w_router, b_router, w1, w3, w2, g_ln2, b_ln2):
    a = ATT_WIDTH
    c_f, c_glu, c_gate = 3 * a, 3 * a + N_HEADS, 3 * a + N_HEADS + 2 * D_CONV
    w_f = jnp.pad(w_in[:, c_f:c_glu], ((0, 0), (0, LANES - N_HEADS)))
    w_pack = jnp.concatenate([w_in[:, :c_f], w_in[:, c_glu:c_gate], w_f], axis=1).astype(bf16)
    pad_route = LANES - N_GROUPS - N_EXPERTS
    row = lambda v: v.reshape(1, -1)
    return dict(
        w_pack=w_pack,
        b_forget=jnp.pad(b_forget, (0, LANES - N_HEADS)).reshape(1, LANES),
        w_gate=w_in[:, c_gate:].astype(bf16),
        w_dw=jnp.pad(w_dw, ((0, HALO - CONV_WIDTH), (0, 0))), b_dw=row(b_dw),
        g_cln=row(g_conv_ln), b_cln=row(b_conv_ln),
        w_attn_out=w_attn_out.astype(bf16), w_conv_out=w_conv_out.astype(bf16), w_out=w_out.astype(bf16),
        g_ln1=row(g_ln1), b_ln1=row(b_ln1),
        w_route=jnp.pad(jnp.concatenate([w_group, w_router], axis=1), ((0, 0), (0, pad_route))),
        b_route=jnp.pad(jnp.concatenate([b_group, b_router]), (0, pad_route)).reshape(1, LANES),
        w1=w1.astype(bf16), w3=w3.astype(bf16), w2=w2.astype(bf16),
        g_ln2=row(g_ln2), b_ln2=row(b_ln2),
    )


def _finish(x, o, u, ctx, w, bb, t, cb, ct, tm_moe):
    b, s, _ = x.shape
    h, gates = _tail(x, o, u, ctx, w, bb, t, cb, ct)
    y = _moe(h.reshape(b * s, D_MODEL), gates.reshape(b * s, LANES), w["w1"], w["w3"], w["w2"],
             w["g_ln2"], w["b_ln2"], tm_moe)
    return y.reshape(b, s, D_MODEL)


def _layer(xp, xs, cache_k, cache_v, cache_logf, cache_conv, w):
    bp, sp, _ = xp.shape
    bs, ts, _ = xs.shape
    past = cache_k.shape[1]
    tq = 512

    q, k, kb, v, vb, lf, u = _in_proj(xp.reshape(bp * sp, D_MODEL), w["w_pack"], w["b_forget"], 512)
    logf = lf[:, :N_HEADS].reshape(bp, sp, N_HEADS)
    c = _cumsum_lanes(logf.transpose(0, 2, 1).reshape(bp * N_HEADS, sp), reverse=False)
    negc = (-c).reshape(bp, ATT_WIDTH // LANES, 2, sp // tq, tq).transpose(0, 1, 3, 2, 4)
    shp = (bp, sp, ATT_WIDTH)
    o = _attn_prompt(q.reshape(shp), kb.reshape(shp), vb.reshape(shp), negc, tq)
    u = u.reshape(bp, sp, D_CONV)
    yp = _finish(xp, o, u, jnp.zeros((bp, HALO, D_CONV), f32), w, 1, 512, 1, 32, 512)
    outs_p = (k.reshape(bp, sp, N_HEADS, HEAD_DIM), v.reshape(bp, sp, N_HEADS, HEAD_DIM), logf,
              u[:, sp - CONV_CTX:])

    q, k, kb, v, vb, lf, u = _in_proj(xs.reshape(bs * ts, D_MODEL), w["w_pack"], w["b_forget"], bs * ts)
    logf = lf[:, :N_HEADS].reshape(bs, ts, N_HEADS)
    lf_t = jnp.pad(logf.transpose(0, 2, 1), ((0, 0), (0, 0), (0, LANES - ts)))
    cq = _cumsum_lanes(lf_t.reshape(bs * N_HEADS, LANES), reverse=False).reshape(bs, N_HEADS, LANES)[:, :, :ts]
    r = _cumsum_lanes(cache_logf.transpose(0, 2, 1).reshape(bs * N_HEADS, past), reverse=True)
    bias_c = jnp.repeat(r.reshape(bs, N_HEADS, past).transpose(0, 2, 1), ts, axis=2)
    causal = jnp.arange(ts)[:, None] <= jnp.arange(ts)[None, :]
    bias_n = jnp.where(causal[None, :, None, :], -cq.transpose(0, 2, 1)[:, :, :, None], NEG)
    bias_n = jnp.pad(bias_n.reshape(bs, ts, LANES), ((0, 0), (0, LANES - ts), (0, 0)), constant_values=NEG)
    q4 = q.reshape(bs, ts, N_HEADS, HEAD_DIM).transpose(0, 2, 3, 1)
    qbd = (q4[:, :, :, None, :] * jnp.eye(N_HEADS, dtype=bf16)[None, :, None, :, None]).reshape(bs, ATT_WIDTH, LANES)
    pad_rows = lambda a: jnp.pad(a.reshape(bs, ts, ATT_WIDTH), ((0, 0), (0, LANES - ts), (0, 0)))
    o = _attn_sample(qbd, cache_k.reshape(bs, past, ATT_WIDTH), cache_v.reshape(bs, past, ATT_WIDTH),
                     pad_rows(kb), pad_rows(vb), bias_c, bias_n, ts)
    u = u.reshape(bs, ts, D_CONV)
    ctx = jnp.pad(cache_conv, ((0, 0), (HALO_PAD, 0), (0, 0)))
    ys = _finish(xs, o, u, ctx, w, bs, ts, 2, ts, bs * ts)
    u_ext = jnp.concatenate([cache_conv, u], axis=1)
    outs_s = (k.reshape(bs, ts, N_HEADS, HEAD_DIM), v.reshape(bs, ts, N_HEADS, HEAD_DIM), logf,
              u_ext[:, u_ext.shape[1] - CONV_CTX:])
    return yp, ys, outs_p, outs_s


def kernel(x_prompt, x_sample, cache_k, cache_v, cache_logf, cache_conv, w_in, b_forget, w_dw, b_dw,
           g_conv_ln, b_conv_ln, w_attn_out, w_conv_out, w_out, g_ln1, b_ln1, w_group, b_group,
           w_router, b_router, w1, w3, w2, g_ln2, b_ln2):
    xp, xs = x_prompt, x_sample
    per_layer_p, per_layer_s = [], []
    for l in range(DEPTH):
        w = _pack_weights(w_in[l], b_forget[l], w_dw[l], b_dw[l], g_conv_ln[l], b_conv_ln[l], w_attn_out[l],
                          w_conv_out[l], w_out[l], g_ln1[l], b_ln1[l], w_group[l], b_group[l], w_router[l],
                          b_router[l], w1[l], w3[l], w2[l], g_ln2[l], b_ln2[l])
        xp, xs, outs_p, outs_s = _layer(xp, xs, cache_k[l], cache_v[l], cache_logf[l], cache_conv[l], w)
        per_layer_p.append(outs_p)
        per_layer_s.append(outs_s)
    stack = lambda outs, j: jnp.stack([o[j] for o in outs])
    return (xp, xs,
            stack(per_layer_p, 0), stack(per_layer_p, 1), stack(per_layer_p, 2), stack(per_layer_p, 3),
            stack(per_layer_s, 0), stack(per_layer_s, 1), stack(per_layer_s, 2), stack(per_layer_s, 3))
```

```python
import functools

import jax
import jax.numpy as jnp
from jax import lax
from jax.experimental import pallas as pl
from jax.experimental.pallas import tpu as pltpu

f32 = jnp.float32
bf16 = jnp.bfloat16

D_MODEL = 1024
N_HEADS = 8
HEAD_DIM = 64
ATT_WIDTH = N_HEADS * HEAD_DIM
D_CONV = 512
CONV_WIDTH = 31
CONV_CTX = CONV_WIDTH - 1
N_GROUPS = 4
EXPERTS_PER_GROUP = 8
N_EXPERTS = N_GROUPS * EXPERTS_PER_GROUP
D_EXPERT = 256
DEPTH = 1
ALPHA = (2 * DEPTH) ** 0.25
LN_EPS = 1e-5
NEG = -1e30
SCALE = HEAD_DIM ** -0.5
LOG2E = 1.4426950408889634

N_SPLIT = 3
LANES = 128
HALO = 32
HALO_PAD = HALO - CONV_CTX
EXPERT_LANE0 = N_GROUPS
VMEM_LIMIT = 56 * 1024 * 1024

_C_Q, _C_K, _C_V, _C_GA, _C_GB, _C_F, _C_END = 0, 512, 1024, 1536, 2048, 2560, 2688


def _sigmoid(x):
    return 1.0 / (1.0 + jnp.exp(-x))


def _layernorm(x, g, b):
    mu = jnp.mean(x, axis=-1, keepdims=True)
    xc = x - mu
    var = jnp.mean(xc * xc, axis=-1, keepdims=True)
    return xc * lax.rsqrt(var + LN_EPS) * g + b


def _params(*sem):
    return pltpu.CompilerParams(dimension_semantics=sem, vmem_limit_bytes=VMEM_LIMIT)


def _in_proj_kernel(x_ref, w_ref, bf_ref, q_ref, k_ref, kb_ref, v_ref, vb_ref, lf_ref, u_ref):
    xb = x_ref[...].astype(bf16)

    def proj(lo, hi):
        return jnp.dot(xb, w_ref[:, lo:hi], preferred_element_type=f32)

    q_ref[...] = (proj(_C_Q, _C_K) * (SCALE * LOG2E)).astype(bf16)
    k = proj(_C_K, _C_V)
    k_ref[...] = k
    kb_ref[...] = k.astype(bf16)
    v = proj(_C_V, _C_GA)
    v_ref[...] = v
    vb_ref[...] = v.astype(bf16)
    u_ref[...] = proj(_C_GA, _C_GB) * _sigmoid(proj(_C_GB, _C_F))
    z = proj(_C_F, _C_END) + bf_ref[...]
    lf_ref[...] = jnp.minimum(z, 0.0) - jnp.log(1.0 + jnp.exp(-jnp.abs(z)))


def _in_proj(x2d, w_pack, bf_pad, tm):
    n = x2d.shape[0]
    row = lambda width: pl.BlockSpec((tm, width), lambda i: (i, 0))
    return pl.pallas_call(
        _in_proj_kernel,
        grid=(n // tm,),
        in_specs=[row(D_MODEL),
                  pl.BlockSpec((D_MODEL, _C_END), lambda i: (0, 0)),
                  pl.BlockSpec((1, LANES), lambda i: (0, 0))],
        out_specs=[row(ATT_WIDTH), row(ATT_WIDTH), row(ATT_WIDTH), row(ATT_WIDTH), row(ATT_WIDTH),
                   row(LANES), row(D_CONV)],
        out_shape=[jax.ShapeDtypeStruct((n, ATT_WIDTH), bf16),
                   jax.ShapeDtypeStruct((n, ATT_WIDTH), f32),
                   jax.ShapeDtypeStruct((n, ATT_WIDTH), bf16),
                   jax.ShapeDtypeStruct((n, ATT_WIDTH), f32),
                   jax.ShapeDtypeStruct((n, ATT_WIDTH), bf16),
                   jax.ShapeDtypeStruct((n, LANES), f32),
                   jax.ShapeDtypeStruct((n, D_CONV), f32)],
        compiler_params=_params("parallel"),
        name="in_proj",
    )(x2d, w_pack, bf_pad)


def _split3(x):
    hi = x.astype(bf16)
    r1 = x - hi.astype(f32)
    mid = r1.astype(bf16)
    lo = (r1 - mid.astype(f32)).astype(bf16)
    return hi, mid, lo


def _cumsum_kernel(x_ref, o_ref, *, reverse, split_scale):
    nc, rows, _ = x_ref.shape
    r = lax.broadcasted_iota(jnp.int32, (LANES, LANES), 0)
    c = lax.broadcasted_iota(jnp.int32, (LANES, LANES), 1)
    tri = jnp.where((r > c) if reverse else (r <= c), 1.0, 0.0).astype(bf16)
    carry = jnp.zeros((rows, 1), f32)
    for j in (range(nc - 1, -1, -1) if reverse else range(nc)):
        xj = x_ref[j]
        y = carry
        for part in _split3(xj):
            y = y + jnp.dot(part, tri, preferred_element_type=f32)
        if split_scale is None:
            o_ref[j] = y
        else:
            for n, part in enumerate(_split3(split_scale * y)):
                o_ref[n, j] = part
        carry = carry + jnp.sum(xj, axis=-1, keepdims=True)


def _cumsum_lanes(x, reverse, split_scale=None):
    rows, s = x.shape
    nc = s // LANES
    xc = x.reshape(rows, nc, LANES).transpose(1, 0, 2)
    shape = (nc, rows, LANES) if split_scale is None else (N_SPLIT, nc, rows, LANES)
    out = pl.pallas_call(
        functools.partial(_cumsum_kernel, reverse=reverse, split_scale=split_scale),
        out_shape=jax.ShapeDtypeStruct(shape, f32 if split_scale is None else bf16),
        name="cumsum_rev" if reverse else "cumsum_fwd",
    )(xc)
    if split_scale is None:
        return out.transpose(1, 0, 2).reshape(rows, s)
    return out.transpose(0, 2, 1, 3).reshape(N_SPLIT, rows, s)


def _attn_kernel(q_ref, k_ref, e_ref, vt_ref, o_ref, qx_sc, acc_sc, *, tq):
    i = pl.program_id(2)
    qt = q_ref[0].astype(f32).T
    row = lax.broadcasted_iota(jnp.int32, (LANES, tq), 0)
    for h in range(2):
        qx_sc[h, 0:LANES, :] = jnp.where(row // HEAD_DIM == h, qt, 0.0).astype(bf16)
        qx_sc[h, LANES:2 * LANES, :] = jnp.where(row // N_SPLIT == h, 1.0, 0.0).astype(bf16)
    acc_sc[...] = jnp.zeros(acc_sc.shape, f32)
    ones_rows = jnp.ones((16, tq), bf16)

    def step(j, carry, masked):
        start = pl.multiple_of(j * tq, tq)
        kx = jnp.concatenate([k_ref[0, pl.ds(start, tq), :], e_ref[0, 0, pl.ds(start, tq), :]], axis=1)
        vt = vt_ref[0, 0, j]
        scores = [jnp.dot(kx, qx_sc[h], preferred_element_type=f32) for h in range(2)]
        out = []
        for h in range(2):
            m_prev, l_prev = carry[h]
            s = scores[h]
            if masked:
                key = lax.broadcasted_iota(jnp.int32, (tq, tq), 0)
                qry = lax.broadcasted_iota(jnp.int32, (tq, tq), 1)
                s = jnp.where(key <= qry, s, NEG)
            m_new = jnp.maximum(m_prev, jnp.max(s, axis=0, keepdims=True))
            alpha = jnp.exp2(m_prev - m_new)
            p = jnp.exp2(s - m_new).astype(bf16)
            vx = jnp.concatenate([vt[h * HEAD_DIM:(h + 1) * HEAD_DIM, :], ones_rows], axis=0)
            pv = jnp.dot(vx, p, preferred_element_type=f32)
            acc_sc[h] = alpha * acc_sc[h] + pv[0:HEAD_DIM, :]
            out.append((m_new, alpha * l_prev + pv[HEAD_DIM:HEAD_DIM + 1, :]))
        return tuple(out)

    init = tuple((jnp.full((1, tq), NEG, f32), jnp.zeros((1, tq), f32)) for _ in range(2))
    carry = lax.fori_loop(0, i, lambda j, c: step(j, c, False), init)
    (_, l0), (_, l1) = step(i, carry, True)
    ot = jnp.concatenate([acc_sc[0] / l0, acc_sc[1] / l1], axis=0)
    o_ref[0] = ot.T.astype(bf16)


def _attn_prompt(q, kb, ext, vt, tq):
    b, s, _ = q.shape
    hp = ATT_WIDTH // LANES
    nq = s // tq
    return pl.pallas_call(
        functools.partial(_attn_kernel, tq=tq),
        grid=(b, hp, nq),
        in_specs=[pl.BlockSpec((1, tq, LANES), lambda b_, h_, i: (b_, i, h_)),
                  pl.BlockSpec((1, s, LANES), lambda b_, h_, i: (b_, 0, h_)),
                  pl.BlockSpec((1, 1, s, LANES), lambda b_, h_, i: (b_, h_, 0, 0)),
                  pl.BlockSpec((1, 1, nq, LANES, tq), lambda b_, h_, i: (b_, h_, 0, 0, 0))],
        out_specs=pl.BlockSpec((1, tq, LANES), lambda b_, h_, i: (b_, i, h_)),
        out_shape=jax.ShapeDtypeStruct((b, s, ATT_WIDTH), bf16),
        scratch_shapes=[pltpu.VMEM((2, 2 * LANES, tq), bf16), pltpu.VMEM((2, HEAD_DIM, tq), f32)],
        compiler_params=_params("parallel", "parallel", "arbitrary"),
        name="attn_prompt",
    )(q, kb, ext, vt)


def _attn_sample_kernel(qbd_ref, ck_ref, cv_ref, kn_ref, vn_ref, bc_ref, bn_ref, o_ref, *, t_new):
    qbd = qbd_ref[0]
    s_c = jnp.dot(ck_ref[0].astype(bf16), qbd, preferred_element_type=f32) + bc_ref[0]
    s_n = jnp.dot(kn_ref[0], qbd, preferred_element_type=f32) + bn_ref[0]
    m = jnp.maximum(jnp.max(s_c, axis=0, keepdims=True), jnp.max(s_n, axis=0, keepdims=True))
    p_c = jnp.exp2(s_c - m)
    p_n = jnp.exp2(s_n - m)
    inv = 1.0 / (jnp.sum(p_c, axis=0, keepdims=True) + jnp.sum(p_n, axis=0, keepdims=True))
    p_c = (p_c * inv).T.astype(bf16)
    p_n = (p_n * inv).T.astype(bf16)
    o_all = (jnp.dot(p_c, cv_ref[0].astype(bf16), preferred_element_type=f32)
             + jnp.dot(p_n, vn_ref[0], preferred_element_type=f32))
    head_of_lane = lax.broadcasted_iota(jnp.int32, (t_new, ATT_WIDTH), 1) // HEAD_DIM
    o = jnp.zeros((t_new, ATT_WIDTH), f32)
    for h in range(N_HEADS):
        o = o + jnp.where(head_of_lane == h, o_all[h * t_new:(h + 1) * t_new, :], 0.0)
    o_ref[0] = o.astype(bf16)


def _attn_sample(qbd, ck, cv, kn, vn, bias_c, bias_n, t_new):
    b, p, _ = ck.shape
    blk = lambda r, c: pl.BlockSpec((1, r, c), lambda i: (i, 0, 0))
    return pl.pallas_call(
        functools.partial(_attn_sample_kernel, t_new=t_new),
        grid=(b,),
        in_specs=[blk(ATT_WIDTH, LANES), blk(p, ATT_WIDTH), blk(p, ATT_WIDTH), blk(LANES, ATT_WIDTH),
                  blk(LANES, ATT_WIDTH), blk(p, LANES), blk(LANES, LANES)],
        out_specs=blk(t_new, ATT_WIDTH),
        out_shape=jax.ShapeDtypeStruct((b, t_new, ATT_WIDTH), bf16),
        compiler_params=_params("parallel"),
        name="attn_sample",
    )(qbd, ck, cv, kn, vn, bias_c, bias_n)


def _route(logits):
    lane = lax.broadcasted_iota(jnp.int32, logits.shape, 1).astype(f32)
    ninf = -jnp.inf
    gl = jnp.where(lane < N_GROUPS, logits, ninf)
    gmax = jnp.max(gl, axis=-1, keepdims=True)
    grp = jnp.min(jnp.where(gl == gmax, lane, float(LANES)), axis=-1, keepdims=True)
    p_sel = 1.0 / jnp.sum(jnp.exp(gl - gmax), axis=-1, keepdims=True)
    e_lo = EXPERT_LANE0 + grp * EXPERTS_PER_GROUP
    el = jnp.where(lane >= e_lo, jnp.where(lane < e_lo + EXPERTS_PER_GROUP, logits, ninf), ninf)
    t1 = jnp.max(el, axis=-1, keepdims=True)
    i1 = jnp.min(jnp.where(el == t1, lane, float(LANES)), axis=-1, keepdims=True)
    el2 = jnp.where(lane == i1, ninf, el)
    t2 = jnp.max(el2, axis=-1, keepdims=True)
    i2 = jnp.min(jnp.where(el2 == t2, lane, float(LANES)), axis=-1, keepdims=True)
    e2 = jnp.exp(t2 - t1)
    w1 = p_sel / (1.0 + e2)
    w2 = p_sel * e2 / (1.0 + e2)
    return jnp.where(lane == i1, w1, 0.0) + jnp.where(lane == i2, w2, 0.0)


def _tail_kernel(x_ref, o_ref, u_ref, uprev_ref, ctx_ref, wdw_ref, bdw_ref, gcl_ref, bcl_ref,
                 wa_ref, wc_ref, wg_ref, wo_ref, g1_ref, b1_ref, wr_ref, br_ref,
                 h_ref, gates_ref, uext_sc, c_sc, *, cb, ct):
    i = pl.program_id(1)
    bb, t, _ = x_ref.shape
    n = bb * t

    uext_sc[:, 0:HALO, :] = jnp.where(i == 0, ctx_ref[...], uprev_ref[...])
    uext_sc[:, HALO:HALO + t, :] = u_ref[...]
    for b0 in range(0, bb, cb):
        for t0 in range(0, t, ct):
            acc = jnp.zeros((cb, ct, D_CONV), f32) + bdw_ref[...]
            for k in range(CONV_WIDTH):
                r0 = HALO_PAD + k + t0
                acc = acc + wdw_ref[k:k + 1, :] * uext_sc[b0:b0 + cb, r0:r0 + ct, :]
            c_sc[b0:b0 + cb, t0:t0 + ct, :] = acc

    c = _layernorm(c_sc[...].reshape(n, D_CONV), gcl_ref[...], bcl_ref[...])
    c = c * _sigmoid(c)
    branch_b = jnp.dot(c.astype(bf16), wc_ref[...], preferred_element_type=f32)
    branch_a = jnp.dot(o_ref[...].reshape(n, ATT_WIDTH), wa_ref[...], preferred_element_type=f32)
    x = x_ref[...].reshape(n, D_MODEL)
    g = _sigmoid(jnp.dot(x.astype(bf16), wg_ref[...], preferred_element_type=f32))
    merged = g[:, :D_MODEL] * branch_a + g[:, D_MODEL:] * branch_b
    mix = jnp.dot(merged.astype(bf16), wo_ref[...], preferred_element_type=f32)
    h = _layernorm(ALPHA * x + mix, g1_ref[...], b1_ref[...])
    h_ref[...] = h.reshape(bb, t, D_MODEL)
    logits = jnp.dot(h, wr_ref[...], preferred_element_type=f32, precision=lax.Precision.HIGHEST) + br_ref[...]
    gates_ref[...] = _route(logits).reshape(bb, t, LANES)


def _tail(x, o, u, ctx, w, bb, t, cb, ct):
    b, s, _ = x.shape
    nt = s // t
    uprev = u if nt > 1 else ctx
    per_tile = t // HALO
    tile = lambda width: pl.BlockSpec((bb, t, width), lambda b_, i: (b_, i, 0))
    full = lambda a: pl.BlockSpec(a.shape, lambda b_, i: (0,) * a.ndim)
    weights = [w["w_dw"], w["b_dw"], w["g_cln"], w["b_cln"], w["w_attn_out"], w["w_conv_out"], w["w_gate"],
               w["w_out"], w["g_ln1"], w["b_ln1"], w["w_route"], w["b_route"]]
    return pl.pallas_call(
        functools.partial(_tail_kernel, cb=cb, ct=ct),
        grid=(b // bb, nt),
        in_specs=[tile(D_MODEL), tile(ATT_WIDTH), tile(D_CONV),
                  pl.BlockSpec((bb, HALO, D_CONV), lambda b_, i: (b_, jnp.maximum(i * per_tile - 1, 0), 0)),
                  pl.BlockSpec((bb, HALO, D_CONV), lambda b_, i: (b_, 0, 0))]
                 + [full(a) for a in weights],
        out_specs=[tile(D_MODEL), tile(LANES)],
        out_shape=[jax.ShapeDtypeStruct((b, s, D_MODEL), f32), jax.ShapeDtypeStruct((b, s, LANES), f32)],
        scratch_shapes=[pltpu.VMEM((bb, HALO + t, D_CONV), f32), pltpu.VMEM((bb, t, D_CONV), f32)],
        compiler_params=_params("parallel", "arbitrary"),
        name="tail",
    )(x, o, u, uprev, ctx, *weights)


def _moe_kernel(h_ref, g_ref, w1_ref, w3_ref, w2_ref, g2_ref, b2_ref, y_ref, acc_sc, hb_sc):
    e = pl.program_id(1)

    @pl.when(e == 0)
    def _():
        acc_sc[...] = jnp.zeros(acc_sc.shape, f32)
        hb_sc[...] = h_ref[...].astype(bf16)

    hb = hb_sc[...]
    a1 = jnp.dot(hb, w1_ref[0], preferred_element_type=f32)
    a3 = jnp.dot(hb, w3_ref[0], preferred_element_type=f32)
    act = (a1 * _sigmoid(a1)) * a3
    ye = jnp.dot(act.astype(bf16), w2_ref[0], preferred_element_type=f32)
    lane = lax.broadcasted_iota(jnp.int32, g_ref.shape, 1)
    ge = jnp.sum(jnp.where(lane == e + EXPERT_LANE0, g_ref[...], 0.0), axis=-1, keepdims=True)
    acc_sc[...] += ge * ye

    @pl.when(e == N_EXPERTS - 1)
    def _():
        y_ref[...] = _layernorm(ALPHA * h_ref[...] + acc_sc[...], g2_ref[...], b2_ref[...])


def _moe(h2d, gates, w1, w3, w2, g2, b2, tm):
    n = h2d.shape[0]
    return pl.pallas_call(
        _moe_kernel,
        grid=(n // tm, N_EXPERTS),
        in_specs=[pl.BlockSpec((tm, D_MODEL), lambda i, e: (i, 0)),
                  pl.BlockSpec((tm, LANES), lambda i, e: (i, 0)),
                  pl.BlockSpec((1, D_MODEL, D_EXPERT), lambda i, e: (e, 0, 0)),
                  pl.BlockSpec((1, D_MODEL, D_EXPERT), lambda i, e: (e, 0, 0)),
                  pl.BlockSpec((1, D_EXPERT, D_MODEL), lambda i, e: (e, 0, 0)),
                  pl.BlockSpec((1, D_MODEL), lambda i, e: (0, 0)),
                  pl.BlockSpec((1, D_MODEL), lambda i, e: (0, 0))],
        out_specs=pl.BlockSpec((tm, D_MODEL), lambda i, e: (i, 0)),
        out_shape=jax.ShapeDtypeStruct((n, D_MODEL), f32),
        scratch_shapes=[pltpu.VMEM((tm, D_MODEL), f32), pltpu.VMEM((tm, D_MODEL), bf16)],
        compiler_params=_params("parallel", "arbitrary"),
        name="moe",
    )(h2d, gates, w1, w3, w2, g2, b2)


def _pack_weights(w_in, b_forget, w_dw, b_dw, g_conv_ln, b_conv_ln, w_attn_out, w_conv_out, w_out,
                  g_ln1, b_ln1, w_group, b_group, w_router, b_router, w1, w3, w2, g_ln2, b_ln2):
    a = ATT_WIDTH
    c_f, c_glu, c_gate = 3 * a, 3 * a + N_HEADS, 3 * a + N_HEADS + 2 * D_CONV
    w_f = jnp.pad(w_in[:, c_f:c_glu], ((0, 0), (0, LANES - N_HEADS)))
    w_pack = jnp.concatenate([w_in[:, :c_f], w_in[:, c_glu:c_gate], w_f], axis=1).astype(bf16)
    pad_route = LANES - N_GROUPS - N_EXPERTS
    row = lambda v: v.reshape(1, -1)
    return dict(
        w_pack=w_pack,
        b_forget=jnp.pad(b_forget, (0, LANES - N_HEADS)).reshape(1, LANES),
        w_gate=w_in[:, c_gate:].astype(bf16),
        w_dw=jnp.pad(w_dw, ((0, HALO - CONV_WIDTH), (0, 0))), b_dw=row(b_dw),
        g_cln=row(g_conv_ln), b_cln=row(b_conv_ln),
        w_attn_out=w_attn_out.astype(bf16), w_conv_out=w_conv_out.astype(bf16), w_out=w_out.astype(bf16),
        g_ln1=row(g_ln1), b_ln1=row(b_ln1),
        w_route=jnp.pad(jnp.concatenate([w_group, w_router], axis=1), ((0, 0), (0, pad_route))),
        b_route=jnp.pad(jnp.concatenate([b_group, b_router]), (0, pad_route)).reshape(1, LANES),
        w1=w1.astype(bf16), w3=w3.astype(bf16), w2=w2.astype(bf16),
        g_ln2=row(g_ln2), b_ln2=row(b_ln2),
    )


def _finish(x, o, u, ctx, w, bb, t, cb, ct, tm_moe):
    b, s, _ = x.shape
    h, gates = _tail(x, o, u, ctx, w, bb, t, cb, ct)
    y = _moe(h.reshape(b * s, D_MODEL), gates.reshape(b * s, LANES), w["w1"], w["w3"], w["w2"],
             w["g_ln2"], w["b_ln2"], tm_moe)
    return y.reshape(b, s, D_MODEL)


def _layer(xp, xs, cache_k, cache_v, cache_logf, cache_conv, w):
    bp, sp, _ = xp.shape
    bs, ts, _ = xs.shape
    past = cache_k.shape[1]
    tq = 512

    q, k, kb, v, vb, lf, u = _in_proj(xp.reshape(bp * sp, D_MODEL), w["w_pack"], w["b_forget"], 512)
    logf = lf[:, :N_HEADS].reshape(bp, sp, N_HEADS)
    parts = _cumsum_lanes(logf.transpose(0, 2, 1).reshape(bp * N_HEADS, sp), reverse=False, split_scale=-LOG2E)
    hp = ATT_WIDTH // LANES
    parts = parts.transpose(1, 0, 2).reshape(bp, hp, 2 * N_SPLIT, sp)
    ext = jnp.pad(parts.transpose(0, 1, 3, 2), ((0, 0), (0, 0), (0, 0), (0, LANES - 2 * N_SPLIT)))
    shp = (bp, sp, ATT_WIDTH)
    vt = vb.reshape(bp, sp // tq, tq, hp, LANES).transpose(0, 3, 1, 4, 2)
    o = _attn_prompt(q.reshape(shp), kb.reshape(shp), ext, vt, tq)
    u = u.reshape(bp, sp, D_CONV)
    yp = _finish(xp, o, u, jnp.zeros((bp, HALO, D_CONV), f32), w, 1, 512, 1, 32, 512)
    outs_p = (k.reshape(bp, sp, N_HEADS, HEAD_DIM), v.reshape(bp, sp, N_HEADS, HEAD_DIM), logf,
              u[:, sp - CONV_CTX:])

    q, k, kb, v, vb, lf, u = _in_proj(xs.reshape(bs * ts, D_MODEL), w["w_pack"], w["b_forget"], bs * ts)
    logf = lf[:, :N_HEADS].reshape(bs, ts, N_HEADS)
    lf_t = jnp.pad(logf.transpose(0, 2, 1), ((0, 0), (0, 0), (0, LANES - ts)))
    cq = _cumsum_lanes(lf_t.reshape(bs * N_HEADS, LANES), reverse=False).reshape(bs, N_HEADS, LANES)[:, :, :ts]
    r = _cumsum_lanes(cache_logf.transpose(0, 2, 1).reshape(bs * N_HEADS, past), reverse=True)
    bias_c = jnp.repeat(LOG2E * r.reshape(bs, N_HEADS, past).transpose(0, 2, 1), ts, axis=2)
    causal = jnp.arange(ts)[:, None] <= jnp.arange(ts)[None, :]
    bias_n = jnp.where(causal[None, :, None, :], -LOG2E * cq.transpose(0, 2, 1)[:, :, :, None], NEG)
    bias_n = jnp.pad(bias_n.reshape(bs, ts, LANES), ((0, 0), (0, LANES - ts), (0, 0)), constant_values=NEG)
    q4 = q.reshape(bs, ts, N_HEADS, HEAD_DIM).transpose(0, 2, 3, 1)
    qbd = (q4[:, :, :, None, :] * jnp.eye(N_HEADS, dtype=bf16)[None, :, None, :, None]).reshape(bs, ATT_WIDTH, LANES)
    pad_rows = lambda a: jnp.pad(a.reshape(bs, ts, ATT_WIDTH), ((0, 0), (0, LANES - ts), (0, 0)))
    o = _attn_sample(qbd, cache_k.reshape(bs, past, ATT_WIDTH), cache_v.reshape(bs, past, ATT_WIDTH),
                     pad_rows(kb), pad_rows(vb), bias_c, bias_n, ts)
    u = u.reshape(bs, ts, D_CONV)
    ctx = jnp.pad(cache_conv, ((0, 0), (HALO_PAD, 0), (0, 0)))
    ys = _finish(xs, o, u, ctx, w, bs, ts, 2, ts, bs * ts)
    u_ext = jnp.concatenate([cache_conv, u], axis=1)
    outs_s = (k.reshape(bs, ts, N_HEADS, HEAD_DIM), v.reshape(bs, ts, N_HEADS, HEAD_DIM), logf,
              u_ext[:, u_ext.shape[1] - CONV_CTX:])
    return yp, ys, outs_p, outs_s


def kernel(x_prompt, x_sample, cache_k, cache_v, cache_logf, cache_conv, w_in, b_forget, w_dw, b_dw,
           g_conv_ln, b_conv_ln, w_attn_out, w_conv_out, w_out, g_ln1, b_ln1, w_group, b_group,
           w_router, b_router, w1, w3, w2, g_ln2, b_ln2):
    xp, xs = x_prompt, x_sample
    per_layer_p, per_layer_s = [], []
    for l in range(DEPTH):
        w = _pack_weights(w_in[l], b_forget[l], w_dw[l], b_dw[l], g_conv_ln[l], b_conv_ln[l], w_attn_out[l],
                          w_conv_out[l], w_out[l], g_ln1[l], b_ln1[l], w_group[l], b_group[l], w_router[l],
                          b_router[l], w1[l], w3[l], w2[l], g_ln2[l], b_ln2[l])
        xp, xs, outs_p, outs_s = _layer(xp, xs, cache_k[l], cache_v[l], cache_logf[l], cache_conv[l], w)
        per_layer_p.append(outs_p)
        per_layer_s.append(outs_s)
    stack = lambda outs, j: jnp.stack([o[j] for o in outs])
    return (xp, xs,
            stack(per_layer_p, 0), stack(per_layer_p, 1), stack(per_layer_p, 2), stack(per_layer_p, 3),
            stack(per_layer_s, 0), stack(per_layer_s, 1), stack(per_layer_s, 2), stack(per_layer_s, 3))
```

```python
import functools

import jax
import jax.numpy as jnp
from jax import lax
from jax.experimental import pallas as pl
from jax.experimental.pallas import tpu as pltpu

f32 = jnp.float32
bf16 = jnp.bfloat16

D_MODEL = 1024
N_HEADS = 8
HEAD_DIM = 64
ATT_WIDTH = N_HEADS * HEAD_DIM
D_CONV = 512
CONV_WIDTH = 31
CONV_CTX = CONV_WIDTH - 1
N_GROUPS = 4
EXPERTS_PER_GROUP = 8
N_EXPERTS = N_GROUPS * EXPERTS_PER_GROUP
D_EXPERT = 256
DEPTH = 1
ALPHA = (2 * DEPTH) ** 0.25
LN_EPS = 1e-5
NEG = -1e30
SCALE = HEAD_DIM ** -0.5
LOG2E = 1.4426950408889634

N_SPLIT = 3
LANES = 128
HALO = 32
HALO_PAD = HALO - CONV_CTX
EXPERT_LANE0 = N_GROUPS
VMEM_LIMIT = 56 * 1024 * 1024

_C_Q, _C_K, _C_V, _C_GA, _C_GB, _C_F, _C_END = 0, 512, 1024, 1536, 2048, 2560, 2688


def _sigmoid(x):
    return 1.0 / (1.0 + jnp.exp(-x))


def _layernorm(x, g, b):
    mu = jnp.mean(x, axis=-1, keepdims=True)
    xc = x - mu
    var = jnp.mean(xc * xc, axis=-1, keepdims=True)
    return xc * lax.rsqrt(var + LN_EPS) * g + b


def _params(*sem):
    return pltpu.CompilerParams(dimension_semantics=sem, vmem_limit_bytes=VMEM_LIMIT)


def _in_proj_kernel(x_ref, w_ref, bf_ref, q_ref, k_ref, kb_ref, v_ref, vb_ref, lf_ref, u_ref):
    xb = x_ref[...].astype(bf16)

    def proj(lo, hi):
        return jnp.dot(xb, w_ref[:, lo:hi], preferred_element_type=f32)

    q_ref[...] = (proj(_C_Q, _C_K) * (SCALE * LOG2E)).astype(bf16)
    k = proj(_C_K, _C_V)
    k_ref[...] = k
    kb_ref[...] = k.astype(bf16)
    v = proj(_C_V, _C_GA)
    v_ref[...] = v
    vb_ref[...] = v.astype(bf16)
    u_ref[...] = proj(_C_GA, _C_GB) * _sigmoid(proj(_C_GB, _C_F))
    z = proj(_C_F, _C_END) + bf_ref[...]
    lf_ref[...] = jnp.minimum(z, 0.0) - jnp.log(1.0 + jnp.exp(-jnp.abs(z)))


def _in_proj(x2d, w_pack, bf_pad, tm):
    n = x2d.shape[0]
    row = lambda width: pl.BlockSpec((tm, width), lambda i: (i, 0))
    return pl.pallas_call(
        _in_proj_kernel,
        grid=(n // tm,),
        in_specs=[row(D_MODEL),
                  pl.BlockSpec((D_MODEL, _C_END), lambda i: (0, 0)),
                  pl.BlockSpec((1, LANES), lambda i: (0, 0))],
        out_specs=[row(ATT_WIDTH), row(ATT_WIDTH), row(ATT_WIDTH), row(ATT_WIDTH), row(ATT_WIDTH),
                   row(LANES), row(D_CONV)],
        out_shape=[jax.ShapeDtypeStruct((n, ATT_WIDTH), bf16),
                   jax.ShapeDtypeStruct((n, ATT_WIDTH), f32),
                   jax.ShapeDtypeStruct((n, ATT_WIDTH), bf16),
                   jax.ShapeDtypeStruct((n, ATT_WIDTH), f32),
                   jax.ShapeDtypeStruct((n, ATT_WIDTH), bf16),
                   jax.ShapeDtypeStruct((n, LANES), f32),
                   jax.ShapeDtypeStruct((n, D_CONV), f32)],
        compiler_params=_params("parallel"),
        name="in_proj",
    )(x2d, w_pack, bf_pad)


def _split3(x):
    hi = x.astype(bf16)
    r1 = x - hi.astype(f32)
    mid = r1.astype(bf16)
    lo = (r1 - mid.astype(f32)).astype(bf16)
    return hi, mid, lo


def _cumsum_kernel(x_ref, o_ref, *, reverse, split_scale):
    nc, rows, _ = x_ref.shape
    r = lax.broadcasted_iota(jnp.int32, (LANES, LANES), 0)
    c = lax.broadcasted_iota(jnp.int32, (LANES, LANES), 1)
    tri = jnp.where((r > c) if reverse else (r <= c), 1.0, 0.0).astype(bf16)
    carry = jnp.zeros((rows, 1), f32)
    for j in (range(nc - 1, -1, -1) if reverse else range(nc)):
        xj = x_ref[j]
        y = carry
        for part in _split3(xj):
            y = y + jnp.dot(part, tri, preferred_element_type=f32)
        if split_scale is None:
            o_ref[j] = y
        else:
            for n, part in enumerate(_split3(split_scale * y)):
                o_ref[n, j] = part
        carry = carry + jnp.sum(xj, axis=-1, keepdims=True)


def _cumsum_lanes(x, reverse, split_scale=None):
    rows, s = x.shape
    nc = s // LANES
    xc = x.reshape(rows, nc, LANES).transpose(1, 0, 2)
    shape = (nc, rows, LANES) if split_scale is None else (N_SPLIT, nc, rows, LANES)
    out = pl.pallas_call(
        functools.partial(_cumsum_kernel, reverse=reverse, split_scale=split_scale),
        out_shape=jax.ShapeDtypeStruct(shape, f32 if split_scale is None else bf16),
        name="cumsum_rev" if reverse else "cumsum_fwd",
    )(xc)
    if split_scale is None:
        return out.transpose(1, 0, 2).reshape(rows, s)
    return out.transpose(0, 2, 1, 3).reshape(N_SPLIT, rows, s)


def _attn_kernel(q_ref, k_ref, e_ref, vt_ref, o_ref, qx_sc, acc_sc, *, tq):
    i = pl.program_id(2)
    qt = q_ref[0].astype(f32).T
    row = lax.broadcasted_iota(jnp.int32, (LANES, tq), 0)
    for h in range(2):
        qx_sc[h, 0:LANES, :] = jnp.where(row // HEAD_DIM == h, qt, 0.0).astype(bf16)
        qx_sc[h, LANES:2 * LANES, :] = jnp.where(row // N_SPLIT == h, 1.0, 0.0).astype(bf16)
    acc_sc[...] = jnp.zeros(acc_sc.shape, f32)
    ones_rows = jnp.ones((16, tq), bf16)

    def step(j, carry, masked):
        start = pl.multiple_of(j * tq, tq)
        kx = jnp.concatenate([k_ref[0, pl.ds(start, tq), :], e_ref[0, 0, pl.ds(start, tq), :]], axis=1)
        vt = vt_ref[0, 0, j]
        scores = [jnp.dot(kx, qx_sc[h], preferred_element_type=f32) for h in range(2)]
        out = []
        for h in range(2):
            m_prev, l_prev = carry[h]
            s = scores[h]
            if masked:
                key = lax.broadcasted_iota(jnp.int32, (tq, tq), 0)
                qry = lax.broadcasted_iota(jnp.int32, (tq, tq), 1)
                s = jnp.where(key <= qry, s, NEG)
            m_new = jnp.maximum(m_prev, jnp.max(s, axis=0, keepdims=True))
            alpha = jnp.exp2(m_prev - m_new)
            p = jnp.exp2(s - m_new).astype(bf16)
            vx = jnp.concatenate([vt[h * HEAD_DIM:(h + 1) * HEAD_DIM, :], ones_rows], axis=0)
            pv = jnp.dot(vx, p, preferred_element_type=f32)
            acc_sc[h] = alpha * acc_sc[h] + pv[0:HEAD_DIM, :]
            out.append((m_new, alpha * l_prev + pv[HEAD_DIM:HEAD_DIM + 1, :]))
        return tuple(out)

    init = tuple((jnp.full((1, tq), NEG, f32), jnp.zeros((1, tq), f32)) for _ in range(2))
    carry = lax.fori_loop(0, i, lambda j, c: step(j, c, False), init)
    (_, l0), (_, l1) = step(i, carry, True)
    ot = jnp.concatenate([acc_sc[0] / l0, acc_sc[1] / l1], axis=0)
    o_ref[0] = ot.T.astype(bf16)


def _attn_prompt(q, kb, ext, vt, tq):
    b, s, _ = q.shape
    hp = ATT_WIDTH // LANES
    nq = s // tq
    return pl.pallas_call(
        functools.partial(_attn_kernel, tq=tq),
        grid=(b, hp, nq),
        in_specs=[pl.BlockSpec((1, tq, LANES), lambda b_, h_, i: (b_, i, h_)),
                  pl.BlockSpec((1, s, LANES), lambda b_, h_, i: (b_, 0, h_)),
                  pl.BlockSpec((1, 1, s, LANES), lambda b_, h_, i: (b_, h_, 0, 0)),
                  pl.BlockSpec((1, 1, nq, LANES, tq), lambda b_, h_, i: (b_, h_, 0, 0, 0))],
        out_specs=pl.BlockSpec((1, tq, LANES), lambda b_, h_, i: (b_, i, h_)),
        out_shape=jax.ShapeDtypeStruct((b, s, ATT_WIDTH), bf16),
        scratch_shapes=[pltpu.VMEM((2, 2 * LANES, tq), bf16), pltpu.VMEM((2, HEAD_DIM, tq), f32)],
        compiler_params=_params("parallel", "parallel", "arbitrary"),
        name="attn_prompt",
    )(q, kb, ext, vt)


def _attn_sample_kernel(qbd_ref, ck_ref, cv_ref, kn_ref, vn_ref, bc_ref, bn_ref, o_ref, *, t_new):
    qbd = qbd_ref[0]
    s_c = jnp.dot(ck_ref[0].astype(bf16), qbd, preferred_element_type=f32) + bc_ref[0]
    s_n = jnp.dot(kn_ref[0], qbd, preferred_element_type=f32) + bn_ref[0]
    m = jnp.maximum(jnp.max(s_c, axis=0, keepdims=True), jnp.max(s_n, axis=0, keepdims=True))
    p_c = jnp.exp2(s_c - m)
    p_n = jnp.exp2(s_n - m)
    inv = 1.0 / (jnp.sum(p_c, axis=0, keepdims=True) + jnp.sum(p_n, axis=0, keepdims=True))
    p_c = (p_c * inv).T.astype(bf16)
    p_n = (p_n * inv).T.astype(bf16)
    o_all = (jnp.dot(p_c, cv_ref[0].astype(bf16), preferred_element_type=f32)
             + jnp.dot(p_n, vn_ref[0], preferred_element_type=f32))
    head_of_lane = lax.broadcasted_iota(jnp.int32, (t_new, ATT_WIDTH), 1) // HEAD_DIM
    o = jnp.zeros((t_new, ATT_WIDTH), f32)
    for h in range(N_HEADS):
        o = o + jnp.where(head_of_lane == h, o_all[h * t_new:(h + 1) * t_new, :], 0.0)
    o_ref[0] = o.astype(bf16)


def _attn_sample(qbd, ck, cv, kn, vn, bias_c, bias_n, t_new):
    b, p, _ = ck.shape
    blk = lambda r, c: pl.BlockSpec((1, r, c), lambda i: (i, 0, 0))
    return pl.pallas_call(
        functools.partial(_attn_sample_kernel, t_new=t_new),
        grid=(b,),
        in_specs=[blk(ATT_WIDTH, LANES), blk(p, ATT_WIDTH), blk(p, ATT_WIDTH), blk(LANES, ATT_WIDTH),
                  blk(LANES, ATT_WIDTH), blk(p, LANES), blk(LANES, LANES)],
        out_specs=blk(t_new, ATT_WIDTH),
        out_shape=jax.ShapeDtypeStruct((b, t_new, ATT_WIDTH), bf16),
        compiler_params=_params("parallel"),
        name="attn_sample",
    )(qbd, ck, cv, kn, vn, bias_c, bias_n)


def _route(logits):
    lane = lax.broadcasted_iota(jnp.int32, logits.shape, 1).astype(f32)
    ninf = -jnp.inf
    gl = jnp.where(lane < N_GROUPS, logits, ninf)
    gmax = jnp.max(gl, axis=-1, keepdims=True)
    grp = jnp.min(jnp.where(gl == gmax, lane, float(LANES)), axis=-1, keepdims=True)
    p_sel = 1.0 / jnp.sum(jnp.exp(gl - gmax), axis=-1, keepdims=True)
    e_lo = EXPERT_LANE0 + grp * EXPERTS_PER_GROUP
    el = jnp.where(lane >= e_lo, jnp.where(lane < e_lo + EXPERTS_PER_GROUP, logits, ninf), ninf)
    t1 = jnp.max(el, axis=-1, keepdims=True)
    i1 = jnp.min(jnp.where(el == t1, lane, float(LANES)), axis=-1, keepdims=True)
    el2 = jnp.where(lane == i1, ninf, el)
    t2 = jnp.max(el2, axis=-1, keepdims=True)
    i2 = jnp.min(jnp.where(el2 == t2, lane, float(LANES)), axis=-1, keepdims=True)
    e2 = jnp.exp(t2 - t1)
    w1 = p_sel / (1.0 + e2)
    w2 = p_sel * e2 / (1.0 + e2)
    picks = (i1 - EXPERT_LANE0, i2 - EXPERT_LANE0, w1, w2)
    out = jnp.zeros(logits.shape, f32)
    for n, val in enumerate(picks):
        out = jnp.where(lane == float(n), val, out)
    return out


def _tail_kernel(x_ref, o_ref, u_ref, uprev_ref, ctx_ref, wdw_ref, bdw_ref, gcl_ref, bcl_ref,
                 wa_ref, wc_ref, wg_ref, wo_ref, g1_ref, b1_ref, wr_ref, br_ref,
                 h_ref, gates_ref, uext_sc, c_sc, *, cb, ct):
    i = pl.program_id(1)
    bb, t, _ = x_ref.shape
    n = bb * t

    uext_sc[:, 0:HALO, :] = jnp.where(i == 0, ctx_ref[...], uprev_ref[...])
    uext_sc[:, HALO:HALO + t, :] = u_ref[...]
    for b0 in range(0, bb, cb):
        for t0 in range(0, t, ct):
            acc = jnp.zeros((cb, ct, D_CONV), f32) + bdw_ref[...]
            for k in range(CONV_WIDTH):
                r0 = HALO_PAD + k + t0
                acc = acc + wdw_ref[k:k + 1, :] * uext_sc[b0:b0 + cb, r0:r0 + ct, :]
            c_sc[b0:b0 + cb, t0:t0 + ct, :] = acc

    c = _layernorm(c_sc[...].reshape(n, D_CONV), gcl_ref[...], bcl_ref[...])
    c = c * _sigmoid(c)
    branch_b = jnp.dot(c.astype(bf16), wc_ref[...], preferred_element_type=f32)
    branch_a = jnp.dot(o_ref[...].reshape(n, ATT_WIDTH), wa_ref[...], preferred_element_type=f32)
    x = x_ref[...].reshape(n, D_MODEL)
    g = _sigmoid(jnp.dot(x.astype(bf16), wg_ref[...], preferred_element_type=f32))
    merged = g[:, :D_MODEL] * branch_a + g[:, D_MODEL:] * branch_b
    mix = jnp.dot(merged.astype(bf16), wo_ref[...], preferred_element_type=f32)
    h = _layernorm(ALPHA * x + mix, g1_ref[...], b1_ref[...])
    h_ref[...] = h.reshape(bb, t, D_MODEL)
    logits = jnp.dot(h, wr_ref[...], preferred_element_type=f32, precision=lax.Precision.HIGHEST) + br_ref[...]
    gates_ref[...] = _route(logits).reshape(bb, t, LANES)


def _tail(x, o, u, ctx, w, bb, t, cb, ct):
    b, s, _ = x.shape
    nt = s // t
    uprev = u if nt > 1 else ctx
    per_tile = t // HALO
    tile = lambda width: pl.BlockSpec((bb, t, width), lambda b_, i: (b_, i, 0))
    full = lambda a: pl.BlockSpec(a.shape, lambda b_, i: (0,) * a.ndim)
    weights = [w["w_dw"], w["b_dw"], w["g_cln"], w["b_cln"], w["w_attn_out"], w["w_conv_out"], w["w_gate"],
               w["w_out"], w["g_ln1"], w["b_ln1"], w["w_route"], w["b_route"]]
    return pl.pallas_call(
        functools.partial(_tail_kernel, cb=cb, ct=ct),
        grid=(b // bb, nt),
        in_specs=[tile(D_MODEL), tile(ATT_WIDTH), tile(D_CONV),
                  pl.BlockSpec((bb, HALO, D_CONV), lambda b_, i: (b_, jnp.maximum(i * per_tile - 1, 0), 0)),
                  pl.BlockSpec((bb, HALO, D_CONV), lambda b_, i: (b_, 0, 0))]
                 + [full(a) for a in weights],
        out_specs=[tile(D_MODEL), tile(LANES)],
        out_shape=[jax.ShapeDtypeStruct((b, s, D_MODEL), f32), jax.ShapeDtypeStruct((b, s, LANES), f32)],
        scratch_shapes=[pltpu.VMEM((bb, HALO + t, D_CONV), f32), pltpu.VMEM((bb, t, D_CONV), f32)],
        compiler_params=_params("parallel", "arbitrary"),
        name="tail",
    )(x, o, u, uprev, ctx, *weights)


MOE_SUB = 256
MOE_CAP = 32
MOE_EPC = 8
MOE_KEY = 4096
_CAP_SHIFT = MOE_CAP.bit_length() - 1


def _moe_kernel(h_ref, r_ref, w1_ref, w3_ref, w2_ref, g2_ref, b2_ref, y_ref,
                acc_sc, hb_sc, key_sc, keyt_sc, nr_sc, *, n_sub):
    c = pl.program_id(1)
    slots = MOE_EPC * MOE_CAP

    @pl.when(c == 0)
    def _():
        acc_sc[...] = jnp.zeros(acc_sc.shape, f32)
        hb_sc[...] = h_ref[...].astype(bf16)
        lane = lax.broadcasted_iota(jnp.int32, (MOE_SUB, LANES), 1).astype(f32)
        ti = lax.broadcasted_iota(jnp.int32, (MOE_SUB, MOE_SUB), 0)
        tj = lax.broadcasted_iota(jnp.int32, (MOE_SUB, MOE_SUB), 1)
        earlier = jnp.where(tj < ti, 1.0, 0.0).astype(bf16)
        most = jnp.zeros((1, LANES), f32)
        for a in range(n_sub):
            r = r_ref[a * MOE_SUB:(a + 1) * MOE_SUB, :]
            e1, e2 = r[:, 0:1], r[:, 1:2]
            oh1 = jnp.where(lane == e1, 1.0, 0.0)
            oh2 = jnp.where(lane == e2, 1.0, 0.0)
            both = oh1 + oh2
            before = jnp.dot(earlier, both.astype(bf16), preferred_element_type=f32)
            key1 = e1 * MOE_KEY + jnp.sum(before * oh1, axis=-1, keepdims=True)
            key2 = e2 * MOE_KEY + jnp.sum(before * oh2, axis=-1, keepdims=True)
            info = jnp.where(lane == 0.0, key1, jnp.where(lane == 1.0, key2, r))
            key_sc[a] = info
            keyt_sc[a] = info.T[0:8, :]
            most = jnp.maximum(most, jnp.sum(both, axis=0, keepdims=True))
        nr_sc[0] = (jnp.max(most).astype(jnp.int32) + (MOE_CAP - 1)) // MOE_CAP

    e0 = c * MOE_EPC
    s_row = lax.broadcasted_iota(jnp.int32, (slots, MOE_SUB), 0)
    s_col = lax.broadcasted_iota(jnp.int32, (MOE_SUB, slots), 1)

    def slot_key(s, rd):
        expert = lax.shift_right_logical(s, _CAP_SHIFT) + e0
        return (expert * MOE_KEY + jnp.bitwise_and(s, MOE_CAP - 1) + rd * MOE_CAP).astype(f32)

    def one_round(rd, carry):
        key_r = slot_key(s_row, rd)
        key_c = slot_key(s_col, rd)
        xs = []
        for a in range(n_sub):
            k1, k2 = keyt_sc[a, 0:1, :], keyt_sc[a, 1:2, :]
            p = jnp.where(key_r == k1, 1.0, jnp.where(key_r == k2, 1.0, 0.0)).astype(bf16)
            hb = hb_sc[a * MOE_SUB:(a + 1) * MOE_SUB, :]
            xs.append(jnp.dot(p, hb, preferred_element_type=f32).astype(bf16))
        ys = [[] for _ in range(n_sub)]
        for e in range(MOE_EPC):
            xe = jnp.concatenate([x[e * MOE_CAP:(e + 1) * MOE_CAP] for x in xs], axis=0)
            a1 = jnp.dot(xe, w1_ref[e], preferred_element_type=f32)
            a3 = jnp.dot(xe, w3_ref[e], preferred_element_type=f32)
            act = (a1 * _sigmoid(a1)) * a3
            ye = jnp.dot(act.astype(bf16), w2_ref[e], preferred_element_type=f32).astype(bf16)
            for a in range(n_sub):
                ys[a].append(ye[a * MOE_CAP:(a + 1) * MOE_CAP])
        for a in range(n_sub):
            info = key_sc[a]
            pw = jnp.where(key_c == info[:, 0:1], info[:, 2:3],
                           jnp.where(key_c == info[:, 1:2], info[:, 3:4], 0.0)).astype(bf16)
            rows = slice(a * MOE_SUB, (a + 1) * MOE_SUB)
            acc_sc[rows, :] += jnp.dot(pw, jnp.concatenate(ys[a], axis=0), preferred_element_type=f32)
        return carry

    lax.fori_loop(0, nr_sc[0], one_round, 0)

    @pl.when(c == pl.num_programs(1) - 1)
    def _():
        y_ref[...] = _layernorm(ALPHA * h_ref[...] + acc_sc[...], g2_ref[...], b2_ref[...])


def _moe(h2d, route, w1, w3, w2, g2, b2, tm):
    n = h2d.shape[0]
    n_sub = tm // MOE_SUB
    expert_block = lambda a: pl.BlockSpec((MOE_EPC,) + a.shape[1:], lambda i, c: (c, 0, 0))
    return pl.pallas_call(
        functools.partial(_moe_kernel, n_sub=n_sub),
        grid=(n // tm, N_EXPERTS // MOE_EPC),
        in_specs=[pl.BlockSpec((tm, D_MODEL), lambda i, c: (i, 0)),
                  pl.BlockSpec((tm, LANES), lambda i, c: (i, 0)),
                  expert_block(w1), expert_block(w3), expert_block(w2),
                  pl.BlockSpec((1, D_MODEL), lambda i, c: (0, 0)),
                  pl.BlockSpec((1, D_MODEL), lambda i, c: (0, 0))],
        out_specs=pl.BlockSpec((tm, D_MODEL), lambda i, c: (i, 0)),
        out_shape=jax.ShapeDtypeStruct((n, D_MODEL), f32),
        scratch_shapes=[pltpu.VMEM((tm, D_MODEL), f32), pltpu.VMEM((tm, D_MODEL), bf16),
                        pltpu.VMEM((n_sub, MOE_SUB, LANES), f32), pltpu.VMEM((n_sub, 8, MOE_SUB), f32),
                        pltpu.SMEM((1,), jnp.int32)],
        compiler_params=_params("parallel", "arbitrary"),
        name="moe",
    )(h2d, route, w1, w3, w2, g2, b2)


def _pack_weights(w_in, b_forget, w_dw, b_dw, g_conv_ln, b_conv_ln, w_attn_out, w_conv_out, w_out,
                  g_ln1, b_ln1, w_group, b_group, w_router, b_router, w1, w3, w2, g_ln2, b_ln2):
    a = ATT_WIDTH
    c_f, c_glu, c_gate = 3 * a, 3 * a + N_HEADS, 3 * a + N_HEADS + 2 * D_CONV
    w_f = jnp.pad(w_in[:, c_f:c_glu], ((0, 0), (0, LANES - N_HEADS)))
    w_pack = jnp.concatenate([w_in[:, :c_f], w_in[:, c_glu:c_gate], w_f], axis=1).astype(bf16)
    pad_route = LANES - N_GROUPS - N_EXPERTS
    row = lambda v: v.reshape(1, -1)
    return dict(
        w_pack=w_pack,
        b_forget=jnp.pad(b_forget, (0, LANES - N_HEADS)).reshape(1, LANES),
        w_gate=w_in[:, c_gate:].astype(bf16),
        w_dw=jnp.pad(w_dw, ((0, HALO - CONV_WIDTH), (0, 0))), b_dw=row(b_dw),
        g_cln=row(g_conv_ln), b_cln=row(b_conv_ln),
        w_attn_out=w_attn_out.astype(bf16), w_conv_out=w_conv_out.astype(bf16), w_out=w_out.astype(bf16),
        g_ln1=row(g_ln1), b_ln1=row(b_ln1),
        w_route=jnp.pad(jnp.concatenate([w_group, w_router], axis=1), ((0, 0), (0, pad_route))),
        b_route=jnp.pad(jnp.concatenate([b_group, b_router]), (0, pad_route)).reshape(1, LANES),
        w1=w1.astype(bf16), w3=w3.astype(bf16), w2=w2.astype(bf16),
        g_ln2=row(g_ln2), b_ln2=row(b_ln2),
    )


def _finish(x, o, u, ctx, w, bb, t, cb, ct, tm_moe):
    b, s, _ = x.shape
    h, gates = _tail(x, o, u, ctx, w, bb, t, cb, ct)
    y = _moe(h.reshape(b * s, D_MODEL), gates.reshape(b * s, LANES), w["w1"], w["w3"], w["w2"],
             w["g_ln2"], w["b_ln2"], tm_moe)
    return y.reshape(b, s, D_MODEL)


def _layer(xp, xs, cache_k, cache_v, cache_logf, cache_conv, w):
    bp, sp, _ = xp.shape
    bs, ts, _ = xs.shape
    past = cache_k.shape[1]
    tq = 512

    q, k, kb, v, vb, lf, u = _in_proj(xp.reshape(bp * sp, D_MODEL), w["w_pack"], w["b_forget"], 512)
    logf = lf[:, :N_HEADS].reshape(bp, sp, N_HEADS)
    parts = _cumsum_lanes(logf.transpose(0, 2, 1).reshape(bp * N_HEADS, sp), reverse=False, split_scale=-LOG2E)
    hp = ATT_WIDTH // LANES
    parts = parts.transpose(1, 0, 2).reshape(bp, hp, 2 * N_SPLIT, sp)
    ext = jnp.pad(parts.transpose(0, 1, 3, 2), ((0, 0), (0, 0), (0, 0), (0, LANES - 2 * N_SPLIT)))
    shp = (bp, sp, ATT_WIDTH)
    vt = vb.reshape(bp, sp // tq, tq, hp, LANES).transpose(0, 3, 1, 4, 2)
    o = _attn_prompt(q.reshape(shp), kb.reshape(shp), ext, vt, tq)
    u = u.reshape(bp, sp, D_CONV)
    yp = _finish(xp, o, u, jnp.zeros((bp, HALO, D_CONV), f32), w, 1, 512, 1, 32, 4 * MOE_SUB)
    outs_p = (k.reshape(bp, sp, N_HEADS, HEAD_DIM), v.reshape(bp, sp, N_HEADS, HEAD_DIM), logf,
              u[:, sp - CONV_CTX:])

    q, k, kb, v, vb, lf, u = _in_proj(xs.reshape(bs * ts, D_MODEL), w["w_pack"], w["b_forget"], bs * ts)
    logf = lf[:, :N_HEADS].reshape(bs, ts, N_HEADS)
    lf_t = jnp.pad(logf.transpose(0, 2, 1), ((0, 0), (0, 0), (0, LANES - ts)))
    cq = _cumsum_lanes(lf_t.reshape(bs * N_HEADS, LANES), reverse=False).reshape(bs, N_HEADS, LANES)[:, :, :ts]
    r = _cumsum_lanes(cache_logf.transpose(0, 2, 1).reshape(bs * N_HEADS, past), reverse=True)
    bias_c = jnp.repeat(LOG2E * r.reshape(bs, N_HEADS, past).transpose(0, 2, 1), ts, axis=2)
    causal = jnp.arange(ts)[:, None] <= jnp.arange(ts)[None, :]
    bias_n = jnp.where(causal[None, :, None, :], -LOG2E * cq.transpose(0, 2, 1)[:, :, :, None], NEG)
    bias_n = jnp.pad(bias_n.reshape(bs, ts, LANES), ((0, 0), (0, LANES - ts), (0, 0)), constant_values=NEG)
    q4 = q.reshape(bs, ts, N_HEADS, HEAD_DIM).transpose(0, 2, 3, 1)
    qbd = (q4[:, :, :, None, :] * jnp.eye(N_HEADS, dtype=bf16)[None, :, None, :, None]).reshape(bs, ATT_WIDTH, LANES)
    pad_rows = lambda a: jnp.pad(a.reshape(bs, ts, ATT_WIDTH), ((0, 0), (0, LANES - ts), (0, 0)))
    o = _attn_sample(qbd, cache_k.reshape(bs, past, ATT_WIDTH), cache_v.reshape(bs, past, ATT_WIDTH),
                     pad_rows(kb), pad_rows(vb), bias_c, bias_n, ts)
    u = u.reshape(bs, ts, D_CONV)
    ctx = jnp.pad(cache_conv, ((0, 0), (HALO_PAD, 0), (0, 0)))
    ys = _finish(xs, o, u, ctx, w, bs, ts, 2, ts, bs * ts)
    u_ext = jnp.concatenate([cache_conv, u], axis=1)
    outs_s = (k.reshape(bs, ts, N_HEADS, HEAD_DIM), v.reshape(bs, ts, N_HEADS, HEAD_DIM), logf,
              u_ext[:, u_ext.shape[1] - CONV_CTX:])
    return yp, ys, outs_p, outs_s


def kernel(x_prompt, x_sample, cache_k, cache_v, cache_logf, cache_conv, w_in, b_forget, w_dw, b_dw,
           g_conv_ln, b_conv_ln, w_attn_out, w_conv_out, w_out, g_ln1, b_ln1, w_group, b_group,
           w_router, b_router, w1, w3, w2, g_ln2, b_ln2):
    xp, xs = x_prompt, x_sample
    per_layer_p, per_layer_s = [], []
    for l in range(DEPTH):
        w = _pack_weights(w_in[l], b_forget[l], w_dw[l], b_dw[l], g_conv_ln[l], b_conv_ln[l], w_attn_out[l],
                          w_conv_out[l], w_out[l], g_ln1[l], b_ln1[l], w_group[l], b_group[l], w_router[l],
                          b_router[l], w1[l], w3[l], w2[l], g_ln2[l], b_ln2[l])
        xp, xs, outs_p, outs_s = _layer(xp, xs, cache_k[l], cache_v[l], cache_logf[l], cache_conv[l], w)
        per_layer_p.append(outs_p)
        per_layer_s.append(outs_s)
    stack = lambda outs, j: jnp.stack([o[j] for o in outs])
    return (xp, xs,
            stack(per_layer_p, 0), stack(per_layer_p, 1), stack(per_layer_p, 2), stack(per_layer_p, 3),
            stack(per_layer_s, 0), stack(per_layer_s, 1), stack(per_layer_s, 2), stack(per_layer_s, 3))
```

```python
import functools

import jax
import jax.numpy as jnp
from jax import lax
from jax.experimental import pallas as pl
from jax.experimental.pallas import tpu as pltpu

f32 = jnp.float32
bf16 = jnp.bfloat16

D_MODEL = 1024
N_HEADS = 8
HEAD_DIM = 64
ATT_WIDTH = N_HEADS * HEAD_DIM
D_CONV = 512
CONV_WIDTH = 31
CONV_CTX = CONV_WIDTH - 1
N_GROUPS = 4
EXPERTS_PER_GROUP = 8
N_EXPERTS = N_GROUPS * EXPERTS_PER_GROUP
D_EXPERT = 256
DEPTH = 1
ALPHA = (2 * DEPTH) ** 0.25
LN_EPS = 1e-5
NEG = -1e30
SCALE = HEAD_DIM ** -0.5
LOG2E = 1.4426950408889634

N_SPLIT = 3
LANES = 128
SUBLANES = 8
HALO = 32
HALO_PAD = HALO - CONV_CTX
EXPERT_LANE0 = N_GROUPS
VMEM_LIMIT = 56 * 1024 * 1024

_C_Q, _C_K, _C_V, _C_GA, _C_GB, _C_F, _C_END = 0, 512, 1024, 1536, 2048, 2560, 2688


def _sigmoid(x):
    return 1.0 / (1.0 + jnp.exp(-x))


def _layernorm(x, g, b):
    mu = jnp.mean(x, axis=-1, keepdims=True)
    xc = x - mu
    var = jnp.mean(xc * xc, axis=-1, keepdims=True)
    return xc * lax.rsqrt(var + LN_EPS) * g + b


def _params(*sem):
    return pltpu.CompilerParams(dimension_semantics=sem, vmem_limit_bytes=VMEM_LIMIT)


def _in_proj_kernel(x_ref, w_ref, bf_ref, q_ref, k_ref, kb_ref, v_ref, vb_ref, lf_ref, u_ref):
    xb = x_ref[...].astype(bf16)

    def proj(lo, hi):
        return jnp.dot(xb, w_ref[:, lo:hi], preferred_element_type=f32)

    q_ref[...] = (proj(_C_Q, _C_K) * (SCALE * LOG2E)).astype(bf16)
    k = proj(_C_K, _C_V)
    k_ref[...] = k
    kb_ref[...] = k.astype(bf16)
    v = proj(_C_V, _C_GA)
    v_ref[...] = v
    vb_ref[...] = v.astype(bf16)
    u_ref[...] = proj(_C_GA, _C_GB) * _sigmoid(proj(_C_GB, _C_F))
    z = proj(_C_F, _C_END) + bf_ref[...]
    lf_ref[...] = jnp.minimum(z, 0.0) - jnp.log(1.0 + jnp.exp(-jnp.abs(z)))


def _in_proj(x2d, w_pack, bf_pad, tm):
    n = x2d.shape[0]
    row = lambda width: pl.BlockSpec((tm, width), lambda i: (i, 0))
    return pl.pallas_call(
        _in_proj_kernel,
        grid=(n // tm,),
        in_specs=[row(D_MODEL),
                  pl.BlockSpec((D_MODEL, _C_END), lambda i: (0, 0)),
                  pl.BlockSpec((1, LANES), lambda i: (0, 0))],
        out_specs=[row(ATT_WIDTH), row(ATT_WIDTH), row(ATT_WIDTH), row(ATT_WIDTH), row(ATT_WIDTH),
                   row(LANES), row(D_CONV)],
        out_shape=[jax.ShapeDtypeStruct((n, ATT_WIDTH), bf16),
                   jax.ShapeDtypeStruct((n, ATT_WIDTH), f32),
                   jax.ShapeDtypeStruct((n, ATT_WIDTH), bf16),
                   jax.ShapeDtypeStruct((n, ATT_WIDTH), f32),
                   jax.ShapeDtypeStruct((n, ATT_WIDTH), bf16),
                   jax.ShapeDtypeStruct((n, LANES), f32),
                   jax.ShapeDtypeStruct((n, D_CONV), f32)],
        compiler_params=_params("parallel"),
        name="in_proj",
    )(x2d, w_pack, bf_pad)


def _split2(x):
    hi = x.astype(bf16)
    return hi, (x - hi.astype(f32)).astype(bf16)


def _split3(x):
    hi = x.astype(bf16)
    r1 = x - hi.astype(f32)
    mid = r1.astype(bf16)
    lo = (r1 - mid.astype(f32)).astype(bf16)
    return hi, mid, lo


def _cumsum_kernel(x_ref, o_ref, *, reverse, split_scale):
    nc, rows, _ = x_ref.shape
    r = lax.broadcasted_iota(jnp.int32, (LANES, LANES), 0)
    c = lax.broadcasted_iota(jnp.int32, (LANES, LANES), 1)
    tri = jnp.where((r > c) if reverse else (r <= c), 1.0, 0.0).astype(bf16)
    carry = jnp.zeros((rows, 1), f32)
    for j in (range(nc - 1, -1, -1) if reverse else range(nc)):
        xj = x_ref[j]
        y = carry
        for part in _split3(xj):
            y = y + jnp.dot(part, tri, preferred_element_type=f32)
        if split_scale is None:
            o_ref[j] = y
        else:
            for n, part in enumerate(_split3(split_scale * y)):
                o_ref[n, j] = part
        carry = carry + jnp.sum(xj, axis=-1, keepdims=True)


def _cumsum_lanes(x, reverse, split_scale=None):
    rows, s = x.shape
    nc = s // LANES
    xc = x.reshape(rows, nc, LANES).transpose(1, 0, 2)
    shape = (nc, rows, LANES) if split_scale is None else (N_SPLIT, nc, rows, LANES)
    out = pl.pallas_call(
        functools.partial(_cumsum_kernel, reverse=reverse, split_scale=split_scale),
        out_shape=jax.ShapeDtypeStruct(shape, f32 if split_scale is None else bf16),
        name="cumsum_rev" if reverse else "cumsum_fwd",
    )(xc)
    if split_scale is None:
        return out.transpose(1, 0, 2).reshape(rows, s)
    return out.transpose(0, 2, 1, 3).reshape(N_SPLIT, rows, s)


def _attn_kernel(q_ref, k_ref, e_ref, vt_ref, o_ref, qx_sc, acc_sc, s0_sc, s1_sc, *, tq):
    i = pl.program_id(2)
    qt = q_ref[0].astype(f32).T
    row = lax.broadcasted_iota(jnp.int32, (LANES, tq), 0)
    for h in range(2):
        qx_sc[h, 0:LANES, :] = jnp.where(row // HEAD_DIM == h, qt, 0.0).astype(bf16)
        qx_sc[h, LANES:2 * LANES, :] = jnp.where(row // N_SPLIT == h, 1.0, 0.0).astype(bf16)
    acc_sc[...] = jnp.zeros(acc_sc.shape, f32)
    ones_rows = jnp.ones((16, tq), bf16)

    s_sc = (s0_sc, s1_sc)

    def scores(j, h):
        start = pl.multiple_of(j * tq, tq)
        kx = jnp.concatenate([k_ref[0, pl.ds(start, tq), :], e_ref[0, 0, pl.ds(start, tq), :]], axis=1)
        s_sc[h][...] = jnp.dot(kx, qx_sc[h], preferred_element_type=f32)

    def absorb(j, h, stats, masked):
        m_prev, l_prev = stats
        s = s_sc[h][...]
        if masked:
            key = lax.broadcasted_iota(jnp.int32, (tq, tq), 0)
            qry = lax.broadcasted_iota(jnp.int32, (tq, tq), 1)
            s = jnp.where(key <= qry, s, NEG)
        m_new = jnp.maximum(m_prev, jnp.max(s, axis=0, keepdims=True))
        alpha = jnp.exp2(m_prev - m_new)
        p = jnp.exp2(s - m_new).astype(bf16)
        vx = jnp.concatenate([vt_ref[0, 0, j, h * HEAD_DIM:(h + 1) * HEAD_DIM, :], ones_rows], axis=0)
        pv = jnp.dot(vx, p, preferred_element_type=f32)
        acc_sc[h] = alpha * acc_sc[h] + pv[0:HEAD_DIM, :]
        return m_new, alpha * l_prev + pv[HEAD_DIM:HEAD_DIM + 1, :]

    def body(j, stats):
        scores(j, 1)
        st0 = absorb(j, 0, stats[0], False)
        scores(j + 1, 0)
        st1 = absorb(j, 1, stats[1], False)
        return st0, st1

    init = tuple((jnp.full((1, tq), NEG, f32), jnp.zeros((1, tq), f32)) for _ in range(2))
    scores(0, 0)
    stats = lax.fori_loop(0, i, body, init)
    scores(i, 1)
    _, l0 = absorb(i, 0, stats[0], True)
    _, l1 = absorb(i, 1, stats[1], True)
    ot = jnp.concatenate([acc_sc[0] / l0, acc_sc[1] / l1], axis=0)
    o_ref[0] = ot.T.astype(bf16)


def _attn_prompt(q, kb, ext, vt, tq):
    b, s, _ = q.shape
    hp = ATT_WIDTH // LANES
    nq = s // tq
    return pl.pallas_call(
        functools.partial(_attn_kernel, tq=tq),
        grid=(b, hp, nq),
        in_specs=[pl.BlockSpec((1, tq, LANES), lambda b_, h_, i: (b_, i, h_)),
                  pl.BlockSpec((1, s, LANES), lambda b_, h_, i: (b_, 0, h_)),
                  pl.BlockSpec((1, 1, s, LANES), lambda b_, h_, i: (b_, h_, 0, 0)),
                  pl.BlockSpec((1, 1, nq, LANES, tq), lambda b_, h_, i: (b_, h_, 0, 0, 0))],
        out_specs=pl.BlockSpec((1, tq, LANES), lambda b_, h_, i: (b_, i, h_)),
        out_shape=jax.ShapeDtypeStruct((b, s, ATT_WIDTH), bf16),
        scratch_shapes=[pltpu.VMEM((2, 2 * LANES, tq), bf16), pltpu.VMEM((2, HEAD_DIM, tq), f32),
                        pltpu.VMEM((tq, tq), f32), pltpu.VMEM((tq, tq), f32)],
        compiler_params=_params("parallel", "parallel", "arbitrary"),
        name="attn_prompt",
    )(q, kb, ext, vt)


def _attn_sample_kernel(qbd_ref, ck_ref, cv_ref, kn_ref, vn_ref, bc_ref, bn_ref, o_ref, *, t_new):
    qbd = qbd_ref[0]
    s_c = jnp.dot(ck_ref[0].astype(bf16), qbd, preferred_element_type=f32) + bc_ref[0]
    s_n = jnp.dot(kn_ref[0], qbd, preferred_element_type=f32) + bn_ref[0]
    m = jnp.maximum(jnp.max(s_c, axis=0, keepdims=True), jnp.max(s_n, axis=0, keepdims=True))
    p_c = jnp.exp2(s_c - m)
    p_n = jnp.exp2(s_n - m)
    inv = 1.0 / (jnp.sum(p_c, axis=0, keepdims=True) + jnp.sum(p_n, axis=0, keepdims=True))
    p_c = (p_c * inv).T.astype(bf16)
    p_n = (p_n * inv).T.astype(bf16)
    o_all = (jnp.dot(p_c, cv_ref[0].astype(bf16), preferred_element_type=f32)
             + jnp.dot(p_n, vn_ref[0], preferred_element_type=f32))
    head_of_lane = lax.broadcasted_iota(jnp.int32, (t_new, ATT_WIDTH), 1) // HEAD_DIM
    o = jnp.zeros((t_new, ATT_WIDTH), f32)
    for h in range(N_HEADS):
        o = o + jnp.where(head_of_lane == h, o_all[h * t_new:(h + 1) * t_new, :], 0.0)
    o_ref[0] = o.astype(bf16)


def _attn_sample(qbd, ck, cv, kn, vn, bias_c, bias_n, t_new):
    b, p, _ = ck.shape
    blk = lambda r, c: pl.BlockSpec((1, r, c), lambda i: (i, 0, 0))
    return pl.pallas_call(
        functools.partial(_attn_sample_kernel, t_new=t_new),
        grid=(b,),
        in_specs=[blk(ATT_WIDTH, LANES), blk(p, ATT_WIDTH), blk(p, ATT_WIDTH), blk(LANES, ATT_WIDTH),
                  blk(LANES, ATT_WIDTH), blk(p, LANES), blk(LANES, LANES)],
        out_specs=blk(t_new, ATT_WIDTH),
        out_shape=jax.ShapeDtypeStruct((b, t_new, ATT_WIDTH), bf16),
        compiler_params=_params("parallel"),
        name="attn_sample",
    )(qbd, ck, cv, kn, vn, bias_c, bias_n)


def _route(logits):
    lane = lax.broadcasted_iota(jnp.int32, logits.shape, 1).astype(f32)
    ninf = -jnp.inf
    gl = jnp.where(lane < N_GROUPS, logits, ninf)
    gmax = jnp.max(gl, axis=-1, keepdims=True)
    grp = jnp.min(jnp.where(gl == gmax, lane, float(LANES)), axis=-1, keepdims=True)
    p_sel = 1.0 / jnp.sum(jnp.exp(gl - gmax), axis=-1, keepdims=True)
    e_lo = EXPERT_LANE0 + grp * EXPERTS_PER_GROUP
    el = jnp.where(lane >= e_lo, jnp.where(lane < e_lo + EXPERTS_PER_GROUP, logits, ninf), ninf)
    t1 = jnp.max(el, axis=-1, keepdims=True)
    i1 = jnp.min(jnp.where(el == t1, lane, float(LANES)), axis=-1, keepdims=True)
    el2 = jnp.where(lane == i1, ninf, el)
    t2 = jnp.max(el2, axis=-1, keepdims=True)
    i2 = jnp.min(jnp.where(el2 == t2, lane, float(LANES)), axis=-1, keepdims=True)
    e2 = jnp.exp(t2 - t1)
    w1 = p_sel / (1.0 + e2)
    w2 = p_sel * e2 / (1.0 + e2)
    picks = (i1 - EXPERT_LANE0, i2 - EXPERT_LANE0, w1, w2)
    out = jnp.zeros(logits.shape, f32)
    for n, val in enumerate(picks):
        out = jnp.where(lane == float(n), val, out)
    return out


def _tail_kernel(x_ref, o_ref, u_ref, uprev_ref, ctx_ref, wdw_ref, bdw_ref, gcl_ref, bcl_ref,
                 wa_ref, wc_ref, wg_ref, wo_ref, g1_ref, b1_ref, wr_ref, br_ref,
                 h_ref, gates_ref, uext_sc, shift_sc, c_sc, *, cb, ct):
    i = pl.program_id(1)
    bb, t, _ = x_ref.shape
    n = bb * t

    x = x_ref[...].reshape(n, D_MODEL)
    g = _sigmoid(jnp.dot(x.astype(bf16), wg_ref[...], preferred_element_type=f32))
    branch_a = jnp.dot(o_ref[...].reshape(n, ATT_WIDTH), wa_ref[...], preferred_element_type=f32)

    uext_sc[:, 0:HALO, :] = jnp.where(i == 0, ctx_ref[...], uprev_ref[...])
    uext_sc[:, HALO:HALO + t, :] = u_ref[...]
    span = t + HALO - SUBLANES
    for r in range(1, SUBLANES):
        shift_sc[r - 1] = uext_sc[:, r:r + span, :]
    for b0 in range(0, bb, cb):
        for t0 in range(0, t, ct):
            acc = jnp.zeros((cb, ct, D_CONV), f32) + bdw_ref[...]
            for k in range(CONV_WIDTH):
                a, r = divmod(HALO_PAD + k, SUBLANES)
                r0 = t0 + a * SUBLANES
                src = uext_sc[b0:b0 + cb, r0:r0 + ct, :] if r == 0 else shift_sc[r - 1, b0:b0 + cb, r0:r0 + ct, :]
                acc = acc + wdw_ref[k:k + 1, :] * src
            c_sc[b0:b0 + cb, t0:t0 + ct, :] = acc

    c = _layernorm(c_sc[...].reshape(n, D_CONV), gcl_ref[...], bcl_ref[...])
    c = c * _sigmoid(c)
    branch_b = jnp.dot(c.astype(bf16), wc_ref[...], preferred_element_type=f32)
    merged = g[:, :D_MODEL] * branch_a + g[:, D_MODEL:] * branch_b
    mix = jnp.dot(merged.astype(bf16), wo_ref[...], preferred_element_type=f32)
    h = _layernorm(ALPHA * x + mix, g1_ref[...], b1_ref[...])
    h_ref[...] = h.reshape(bb, t, D_MODEL)
    w_hi, w_lo = _split2(wr_ref[...])
    w_parts = jnp.concatenate([w_hi, w_lo], axis=1)
    cross = sum(jnp.dot(part, w_parts, preferred_element_type=f32) for part in _split2(h))
    logits = cross[:, :LANES] + cross[:, LANES:] + br_ref[...]
    gates_ref[...] = _route(logits).reshape(bb, t, LANES)


def _tail(x, o, u, ctx, w, bb, t, cb, ct):
    b, s, _ = x.shape
    nt = s // t
    uprev = u if nt > 1 else ctx
    per_tile = t // HALO
    tile = lambda width: pl.BlockSpec((bb, t, width), lambda b_, i: (b_, i, 0))
    full = lambda a: pl.BlockSpec(a.shape, lambda b_, i: (0,) * a.ndim)
    weights = [w["w_dw"], w["b_dw"], w["g_cln"], w["b_cln"], w["w_attn_out"], w["w_conv_out"], w["w_gate"],
               w["w_out"], w["g_ln1"], w["b_ln1"], w["w_route"], w["b_route"]]
    return pl.pallas_call(
        functools.partial(_tail_kernel, cb=cb, ct=ct),
        grid=(b // bb, nt),
        in_specs=[tile(D_MODEL), tile(ATT_WIDTH), tile(D_CONV),
                  pl.BlockSpec((bb, HALO, D_CONV), lambda b_, i: (b_, jnp.maximum(i * per_tile - 1, 0), 0)),
                  pl.BlockSpec((bb, HALO, D_CONV), lambda b_, i: (b_, 0, 0))]
                 + [full(a) for a in weights],
        out_specs=[tile(D_MODEL), tile(LANES)],
        out_shape=[jax.ShapeDtypeStruct((b, s, D_MODEL), f32), jax.ShapeDtypeStruct((b, s, LANES), f32)],
        scratch_shapes=[pltpu.VMEM((bb, HALO + t, D_CONV), f32),
                        pltpu.VMEM((SUBLANES - 1, bb, t + HALO - SUBLANES, D_CONV), f32),
                        pltpu.VMEM((bb, t, D_CONV), f32)],
        compiler_params=_params("parallel", "arbitrary"),
        name="tail",
    )(x, o, u, uprev, ctx, *weights)


MOE_SUB = 256
MOE_CAP = 32
MOE_EPC = 8
MOE_KEY = 4096
_CAP_SHIFT = MOE_CAP.bit_length() - 1


def _moe_kernel(h_ref, r_ref, w1_ref, w3_ref, w2_ref, g2_ref, b2_ref, y_ref,
                acc_sc, hb_sc, key_sc, keyt_sc, nr_sc, *, n_sub):
    c = pl.program_id(1)
    slots = MOE_EPC * MOE_CAP

    @pl.when(c == 0)
    def _():
        acc_sc[...] = jnp.zeros(acc_sc.shape, f32)
        hb_sc[...] = h_ref[...].astype(bf16)
        lane = lax.broadcasted_iota(jnp.int32, (MOE_SUB, LANES), 1).astype(f32)
        ti = lax.broadcasted_iota(jnp.int32, (MOE_SUB, MOE_SUB), 0)
        tj = lax.broadcasted_iota(jnp.int32, (MOE_SUB, MOE_SUB), 1)
        earlier = jnp.where(tj < ti, 1.0, 0.0).astype(bf16)
        most = jnp.zeros((1, LANES), f32)
        for a in range(n_sub):
            r = r_ref[a * MOE_SUB:(a + 1) * MOE_SUB, :]
            e1, e2 = r[:, 0:1], r[:, 1:2]
            oh1 = jnp.where(lane == e1, 1.0, 0.0)
            oh2 = jnp.where(lane == e2, 1.0, 0.0)
            both = oh1 + oh2
            before = jnp.dot(earlier, both.astype(bf16), preferred_element_type=f32)
            key1 = e1 * MOE_KEY + jnp.sum(before * oh1, axis=-1, keepdims=True)
            key2 = e2 * MOE_KEY + jnp.sum(before * oh2, axis=-1, keepdims=True)
            info = jnp.where(lane == 0.0, key1, jnp.where(lane == 1.0, key2, r))
            key_sc[a] = info
            keyt_sc[a] = info.T[0:8, :]
            most = jnp.maximum(most, jnp.sum(both, axis=0, keepdims=True))
        nr_sc[0] = (jnp.max(most).astype(jnp.int32) + (MOE_CAP - 1)) // MOE_CAP

    e0 = c * MOE_EPC
    s_row = lax.broadcasted_iota(jnp.int32, (slots, MOE_SUB), 0)
    s_col = lax.broadcasted_iota(jnp.int32, (MOE_SUB, slots), 1)

    def slot_key(s, rd):
        expert = lax.shift_right_logical(s, _CAP_SHIFT) + e0
        return (expert * MOE_KEY + jnp.bitwise_and(s, MOE_CAP - 1) + rd * MOE_CAP).astype(f32)

    def one_round(rd, carry):
        key_r = slot_key(s_row, rd)
        key_c = slot_key(s_col, rd)
        xs = []
        for a in range(n_sub):
            k1, k2 = keyt_sc[a, 0:1, :], keyt_sc[a, 1:2, :]
            p = jnp.where(key_r == k1, 1.0, jnp.where(key_r == k2, 1.0, 0.0)).astype(bf16)
            hb = hb_sc[a * MOE_SUB:(a + 1) * MOE_SUB, :]
            xs.append(jnp.dot(p, hb, preferred_element_type=f32).astype(bf16))
        ys = [[] for _ in range(n_sub)]
        for e in range(MOE_EPC):
            xe = jnp.concatenate([x[e * MOE_CAP:(e + 1) * MOE_CAP] for x in xs], axis=0)
            a1 = jnp.dot(xe, w1_ref[e], preferred_element_type=f32)
            a3 = jnp.dot(xe, w3_ref[e], preferred_element_type=f32)
            act = (a1 * _sigmoid(a1)) * a3
            ye = jnp.dot(act.astype(bf16), w2_ref[e], preferred_element_type=f32).astype(bf16)
            for a in range(n_sub):
                ys[a].append(ye[a * MOE_CAP:(a + 1) * MOE_CAP])
        for a in range(n_sub):
            info = key_sc[a]
            pw = jnp.where(key_c == info[:, 0:1], info[:, 2:3],
                           jnp.where(key_c == info[:, 1:2], info[:, 3:4], 0.0)).astype(bf16)
            rows = slice(a * MOE_SUB, (a + 1) * MOE_SUB)
            acc_sc[rows, :] += jnp.dot(pw, jnp.concatenate(ys[a], axis=0), preferred_element_type=f32)
        return carry

    lax.fori_loop(0, nr_sc[0], one_round, 0)

    @pl.when(c == pl.num_programs(1) - 1)
    def _():
        y_ref[...] = _layernorm(ALPHA * h_ref[...] + acc_sc[...], g2_ref[...], b2_ref[...])


def _moe(h2d, route, w1, w3, w2, g2, b2, tm):
    n = h2d.shape[0]
    n_sub = tm // MOE_SUB
    expert_block = lambda a: pl.BlockSpec((MOE_EPC,) + a.shape[1:], lambda i, c: (c, 0, 0))
    return pl.pallas_call(
        functools.partial(_moe_kernel, n_sub=n_sub),
        grid=(n // tm, N_EXPERTS // MOE_EPC),
        in_specs=[pl.BlockSpec((tm, D_MODEL), lambda i, c: (i, 0)),
                  pl.BlockSpec((tm, LANES), lambda i, c: (i, 0)),
                  expert_block(w1), expert_block(w3), expert_block(w2),
                  pl.BlockSpec((1, D_MODEL), lambda i, c: (0, 0)),
                  pl.BlockSpec((1, D_MODEL), lambda i, c: (0, 0))],
        out_specs=pl.BlockSpec((tm, D_MODEL), lambda i, c: (i, 0)),
        out_shape=jax.ShapeDtypeStruct((n, D_MODEL), f32),
        scratch_shapes=[pltpu.VMEM((tm, D_MODEL), f32), pltpu.VMEM((tm, D_MODEL), bf16),
                        pltpu.VMEM((n_sub, MOE_SUB, LANES), f32), pltpu.VMEM((n_sub, 8, MOE_SUB), f32),
                        pltpu.SMEM((1,), jnp.int32)],
        compiler_params=_params("parallel", "arbitrary"),
        name="moe",
    )(h2d, route, w1, w3, w2, g2, b2)


def _pack_weights(w_in, b_forget, w_dw, b_dw, g_conv_ln, b_conv_ln, w_attn_out, w_conv_out, w_out,
                  g_ln1, b_ln1, w_group, b_group, w_router, b_router, w1, w3, w2, g_ln2, b_ln2):
    a = ATT_WIDTH
    c_f, c_glu, c_gate = 3 * a, 3 * a + N_HEADS, 3 * a + N_HEADS + 2 * D_CONV
    w_f = jnp.pad(w_in[:, c_f:c_glu], ((0, 0), (0, LANES - N_HEADS)))
    w_pack = jnp.concatenate([w_in[:, :c_f], w_in[:, c_glu:c_gate], w_f], axis=1).astype(bf16)
    pad_route = LANES - N_GROUPS - N_EXPERTS
    row = lambda v: v.reshape(1, -1)
    return dict(
        w_pack=w_pack,
        b_forget=jnp.pad(b_forget, (0, LANES - N_HEADS)).reshape(1, LANES),
        w_gate=w_in[:, c_gate:].astype(bf16),
        w_dw=jnp.pad(w_dw, ((0, HALO - CONV_WIDTH), (0, 0))), b_dw=row(b_dw),
        g_cln=row(g_conv_ln), b_cln=row(b_conv_ln),
        w_attn_out=w_attn_out.astype(bf16), w_conv_out=w_conv_out.astype(bf16), w_out=w_out.astype(bf16),
        g_ln1=row(g_ln1), b_ln1=row(b_ln1),
        w_route=jnp.pad(jnp.concatenate([w_group, w_router], axis=1), ((0, 0), (0, pad_route))),
        b_route=jnp.pad(jnp.concatenate([b_group, b_router]), (0, pad_route)).reshape(1, LANES),
        w1=w1.astype(bf16), w3=w3.astype(bf16), w2=w2.astype(bf16),
        g_ln2=row(g_ln2), b_ln2=row(b_ln2),
    )


def _finish(x, o, u, ctx, w, bb, t, cb, ct, tm_moe):
    b, s, _ = x.shape
    h, gates = _tail(x, o, u, ctx, w, bb, t, cb, ct)
    y = _moe(h.reshape(b * s, D_MODEL), gates.reshape(b * s, LANES), w["w1"], w["w3"], w["w2"],
             w["g_ln2"], w["b_ln2"], tm_moe)
    return y.reshape(b, s, D_MODEL)


def _layer(xp, xs, cache_k, cache_v, cache_logf, cache_conv, w):
    bp, sp, _ = xp.shape
    bs, ts, _ = xs.shape
    past = cache_k.shape[1]
    tq = 512

    q, k, kb, v, vb, lf, u = _in_proj(xp.reshape(bp * sp, D_MODEL), w["w_pack"], w["b_forget"], 512)
    logf = lf[:, :N_HEADS].reshape(bp, sp, N_HEADS)
    parts = _cumsum_lanes(logf.transpose(0, 2, 1).reshape(bp * N_HEADS, sp), reverse=False, split_scale=-LOG2E)
    hp = ATT_WIDTH // LANES
    parts = parts.transpose(1, 0, 2).reshape(bp, hp, 2 * N_SPLIT, sp)
    ext = jnp.pad(parts.transpose(0, 1, 3, 2), ((0, 0), (0, 0), (0, 0), (0, LANES - 2 * N_SPLIT)))
    shp = (bp, sp, ATT_WIDTH)
    vt = vb.reshape(bp, sp // tq, tq, hp, LANES).transpose(0, 3, 1, 4, 2)
    o = _attn_prompt(q.reshape(shp), kb.reshape(shp), ext, vt, tq)
    u = u.reshape(bp, sp, D_CONV)
    yp = _finish(xp, o, u, jnp.zeros((bp, HALO, D_CONV), f32), w, 1, 512, 1, 32, 4 * MOE_SUB)
    outs_p = (k.reshape(bp, sp, N_HEADS, HEAD_DIM), v.reshape(bp, sp, N_HEADS, HEAD_DIM), logf,
              u[:, sp - CONV_CTX:])

    q, k, kb, v, vb, lf, u = _in_proj(xs.reshape(bs * ts, D_MODEL), w["w_pack"], w["b_forget"], bs * ts)
    logf = lf[:, :N_HEADS].reshape(bs, ts, N_HEADS)
    lf_t = jnp.pad(logf.transpose(0, 2, 1), ((0, 0), (0, 0), (0, LANES - ts)))
    cq = _cumsum_lanes(lf_t.reshape(bs * N_HEADS, LANES), reverse=False).reshape(bs, N_HEADS, LANES)[:, :, :ts]
    r = _cumsum_lanes(cache_logf.transpose(0, 2, 1).reshape(bs * N_HEADS, past), reverse=True)
    bias_c = jnp.repeat(LOG2E * r.reshape(bs, N_HEADS, past).transpose(0, 2, 1), ts, axis=2)
    causal = jnp.arange(ts)[:, None] <= jnp.arange(ts)[None, :]
    bias_n = jnp.where(causal[None, :, None, :], -LOG2E * cq.transpose(0, 2, 1)[:, :, :, None], NEG)
    bias_n = jnp.pad(bias_n.reshape(bs, ts, LANES), ((0, 0), (0, LANES - ts), (0, 0)), constant_values=NEG)
    q4 = q.reshape(bs, ts, N_HEADS, HEAD_DIM).transpose(0, 2, 3, 1)
    qbd = (q4[:, :, :, None, :] * jnp.eye(N_HEADS, dtype=bf16)[None, :, None, :, None]).reshape(bs, ATT_WIDTH, LANES)
    pad_rows = lambda a: jnp.pad(a.reshape(bs, ts, ATT_WIDTH), ((0, 0), (0, LANES - ts), (0, 0)))
    o = _attn_sample(qbd, cache_k.reshape(bs, past, ATT_WIDTH), cache_v.reshape(bs, past, ATT_WIDTH),
                     pad_rows(kb), pad_rows(vb), bias_c, bias_n, ts)
    u = u.reshape(bs, ts, D_CONV)
    ctx = jnp.pad(cache_conv, ((0, 0), (HALO_PAD, 0), (0, 0)))
    ys = _finish(xs, o, u, ctx, w, bs, ts, 2, ts, bs * ts)
    u_ext = jnp.concatenate([cache_conv, u], axis=1)
    outs_s = (k.reshape(bs, ts, N_HEADS, HEAD_DIM), v.reshape(bs, ts, N_HEADS, HEAD_DIM), logf,
              u_ext[:, u_ext.shape[1] - CONV_CTX:])
    return yp, ys, outs_p, outs_s


def kernel(x_prompt, x_sample, cache_k, cache_v, cache_logf, cache_conv, w_in, b_forget, w_dw, b_dw,
           g_conv_ln, b_conv_ln, w_attn_out, w_conv_out, w_out, g_ln1, b_ln1, w_group, b_group,
           w_router, b_router, w1, w3, w2, g_ln2, b_ln2):
    xp, xs = x_prompt, x_sample
    per_layer_p, per_layer_s = [], []
    for l in range(DEPTH):
        w = _pack_weights(w_in[l], b_forget[l], w_dw[l], b_dw[l], g_conv_ln[l], b_conv_ln[l], w_attn_out[l],
                          w_conv_out[l], w_out[l], g_ln1[l], b_ln1[l], w_group[l], b_group[l], w_router[l],
                          b_router[l], w1[l], w3[l], w2[l], g_ln2[l], b_ln2[l])
        xp, xs, outs_p, outs_s = _layer(xp, xs, cache_k[l], cache_v[l], cache_logf[l], cache_conv[l], w)
        per_layer_p.append(outs_p)
        per_layer_s.append(outs_s)
    stack = lambda outs, j: jnp.stack([o[j] for o in outs])
    return (xp, xs,
            stack(per_layer_p, 0), stack(per_layer_p, 1), stack(per_layer_p, 2), stack(per_layer_p, 3),
            stack(per_layer_s, 0), stack(per_layer_s, 1), stack(per_layer_s, 2), stack(per_layer_s, 3))
```

```python
import functools

import jax
import jax.numpy as jnp
from jax import lax
from jax.experimental import pallas as pl
from jax.experimental.pallas import tpu as pltpu

f32 = jnp.float32
bf16 = jnp.bfloat16

D_MODEL = 1024
N_HEADS = 8
HEAD_DIM = 64
ATT_WIDTH = N_HEADS * HEAD_DIM
D_CONV = 512
CONV_WIDTH = 31
CONV_CTX = CONV_WIDTH - 1
N_GROUPS = 4
EXPERTS_PER_GROUP = 8
N_EXPERTS = N_GROUPS * EXPERTS_PER_GROUP
D_EXPERT = 256
DEPTH = 1
ALPHA = (2 * DEPTH) ** 0.25
LN_EPS = 1e-5
NEG = -1e30
SCALE = HEAD_DIM ** -0.5
LOG2E = 1.4426950408889634

N_SPLIT = 3
LANES = 128
SUBLANES = 8
HALO = 32
HALO_PAD = HALO - CONV_CTX
EXPERT_LANE0 = N_GROUPS
VMEM_LIMIT = 56 * 1024 * 1024

_C_Q, _C_K, _C_V, _C_GA, _C_GB, _C_F, _C_END = 0, 512, 1024, 1536, 2048, 2560, 2688


def _sigmoid(x):
    return 1.0 / (1.0 + jnp.exp(-x))


def _layernorm(x, g, b):
    mu = jnp.mean(x, axis=-1, keepdims=True)
    xc = x - mu
    var = jnp.mean(xc * xc, axis=-1, keepdims=True)
    return xc * lax.rsqrt(var + LN_EPS) * g + b


def _params(*sem):
    return pltpu.CompilerParams(dimension_semantics=sem, vmem_limit_bytes=VMEM_LIMIT)


def _store_head_major(ref, val):
    tokens = val.shape[0]
    for h in range(N_HEADS):
        ref[pl.ds(h, tokens, stride=N_HEADS), :] = val[:, h * HEAD_DIM:(h + 1) * HEAD_DIM]


def _in_proj_kernel(x_ref, w_ref, bf_ref, q_ref, k_ref, kb_ref, v_ref, vb_ref, lf_ref, u_ref):
    xb = x_ref[...].astype(bf16)

    def proj(lo, hi):
        return jnp.dot(xb, w_ref[:, lo:hi], preferred_element_type=f32)

    q_ref[...] = (proj(_C_Q, _C_K) * (SCALE * LOG2E)).astype(bf16)
    k = proj(_C_K, _C_V)
    _store_head_major(k_ref, k)
    kb_ref[...] = k.astype(bf16)
    v = proj(_C_V, _C_GA)
    _store_head_major(v_ref, v)
    vb_ref[...] = v.astype(bf16)
    u_ref[...] = proj(_C_GA, _C_GB) * _sigmoid(proj(_C_GB, _C_F))
    z = proj(_C_F, _C_END) + bf_ref[...]
    lf_ref[...] = jnp.minimum(z, 0.0) - jnp.log(1.0 + jnp.exp(-jnp.abs(z)))


def _in_proj(x2d, w_pack, bf_pad, tm):
    n = x2d.shape[0]
    row = lambda width: pl.BlockSpec((tm, width), lambda i: (i, 0))
    heads = pl.BlockSpec((tm * N_HEADS, HEAD_DIM), lambda i: (i, 0))
    return pl.pallas_call(
        _in_proj_kernel,
        grid=(n // tm,),
        in_specs=[row(D_MODEL),
                  pl.BlockSpec((D_MODEL, _C_END), lambda i: (0, 0)),
                  pl.BlockSpec((1, LANES), lambda i: (0, 0))],
        out_specs=[row(ATT_WIDTH), heads, row(ATT_WIDTH), heads, row(ATT_WIDTH),
                   row(LANES), row(D_CONV)],
        out_shape=[jax.ShapeDtypeStruct((n, ATT_WIDTH), bf16),
                   jax.ShapeDtypeStruct((n * N_HEADS, HEAD_DIM), f32),
                   jax.ShapeDtypeStruct((n, ATT_WIDTH), bf16),
                   jax.ShapeDtypeStruct((n * N_HEADS, HEAD_DIM), f32),
                   jax.ShapeDtypeStruct((n, ATT_WIDTH), bf16),
                   jax.ShapeDtypeStruct((n, LANES), f32),
                   jax.ShapeDtypeStruct((n, D_CONV), f32)],
        compiler_params=_params("parallel"),
        name="in_proj",
    )(x2d, w_pack, bf_pad)


def _split2(x):
    hi = x.astype(bf16)
    return hi, (x - hi.astype(f32)).astype(bf16)


def _split3(x):
    hi = x.astype(bf16)
    r1 = x - hi.astype(f32)
    mid = r1.astype(bf16)
    lo = (r1 - mid.astype(f32)).astype(bf16)
    return hi, mid, lo


def _cumsum_kernel(x_ref, o_ref, *, reverse, split_scale):
    nc, rows, _ = x_ref.shape
    r = lax.broadcasted_iota(jnp.int32, (LANES, LANES), 0)
    c = lax.broadcasted_iota(jnp.int32, (LANES, LANES), 1)
    tri = jnp.where((r > c) if reverse else (r <= c), 1.0, 0.0).astype(bf16)
    carry = jnp.zeros((rows, 1), f32)
    for j in (range(nc - 1, -1, -1) if reverse else range(nc)):
        xj = x_ref[j]
        y = carry
        for part in _split3(xj):
            y = y + jnp.dot(part, tri, preferred_element_type=f32)
        if split_scale is None:
            o_ref[j] = y
        else:
            for n, part in enumerate(_split3(split_scale * y)):
                o_ref[n, j] = part
        carry = carry + jnp.sum(xj, axis=-1, keepdims=True)


def _cumsum_lanes(x, reverse, split_scale=None):
    rows, s = x.shape
    nc = s // LANES
    xc = x.reshape(rows, nc, LANES).transpose(1, 0, 2)
    shape = (nc, rows, LANES) if split_scale is None else (N_SPLIT, nc, rows, LANES)
    out = pl.pallas_call(
        functools.partial(_cumsum_kernel, reverse=reverse, split_scale=split_scale),
        out_shape=jax.ShapeDtypeStruct(shape, f32 if split_scale is None else bf16),
        name="cumsum_rev" if reverse else "cumsum_fwd",
    )(xc)
    if split_scale is None:
        return out.transpose(1, 0, 2).reshape(rows, s)
    return out.transpose(0, 2, 1, 3).reshape(N_SPLIT, rows, s)


def _attn_kernel(q_ref, k_ref, e_ref, vt_ref, o_ref, qx_sc, acc_sc, s0_sc, s1_sc, *, tq):
    i = pl.program_id(2)
    qt = q_ref[0].astype(f32).T
    row = lax.broadcasted_iota(jnp.int32, (LANES, tq), 0)
    for h in range(2):
        qx_sc[h, 0:LANES, :] = jnp.where(row // HEAD_DIM == h, qt, 0.0).astype(bf16)
        qx_sc[h, LANES:2 * LANES, :] = jnp.where(row // N_SPLIT == h, 1.0, 0.0).astype(bf16)
    acc_sc[...] = jnp.zeros(acc_sc.shape, f32)
    ones_rows = jnp.ones((16, tq), bf16)

    s_sc = (s0_sc, s1_sc)

    def scores(j, h):
        start = pl.multiple_of(j * tq, tq)
        kx = jnp.concatenate([k_ref[0, pl.ds(start, tq), :], e_ref[0, 0, pl.ds(start, tq), :]], axis=1)
        s_sc[h][...] = jnp.dot(kx, qx_sc[h], preferred_element_type=f32)

    def absorb(j, h, stats, masked):
        m_prev, l_prev = stats
        s = s_sc[h][...]
        if masked:
            key = lax.broadcasted_iota(jnp.int32, (tq, tq), 0)
            qry = lax.broadcasted_iota(jnp.int32, (tq, tq), 1)
            s = jnp.where(key <= qry, s, NEG)
        m_new = jnp.maximum(m_prev, jnp.max(s, axis=0, keepdims=True))
        alpha = jnp.exp2(m_prev - m_new)
        p = jnp.exp2(s - m_new).astype(bf16)
        vx = jnp.concatenate([vt_ref[0, 0, j, h * HEAD_DIM:(h + 1) * HEAD_DIM, :], ones_rows], axis=0)
        pv = jnp.dot(vx, p, preferred_element_type=f32)
        acc_sc[h] = alpha * acc_sc[h] + pv[0:HEAD_DIM, :]
        return m_new, alpha * l_prev + pv[HEAD_DIM:HEAD_DIM + 1, :]

    def body(j, stats):
        scores(j, 1)
        st0 = absorb(j, 0, stats[0], False)
        scores(j + 1, 0)
        st1 = absorb(j, 1, stats[1], False)
        return st0, st1

    init = tuple((jnp.full((1, tq), NEG, f32), jnp.zeros((1, tq), f32)) for _ in range(2))
    scores(0, 0)
    stats = lax.fori_loop(0, i // 2, lambda p, st: body(2 * p + 1, body(2 * p, st)), init)
    stats = lax.cond(i % 2 == 1, lambda st: body(i - 1, st), lambda st: st, stats)
    scores(i, 1)
    _, l0 = absorb(i, 0, stats[0], True)
    _, l1 = absorb(i, 1, stats[1], True)
    ot = jnp.concatenate([acc_sc[0] / l0, acc_sc[1] / l1], axis=0)
    o_ref[0] = ot.T.astype(bf16)


def _attn_prompt(q, kb, ext, vt, tq):
    b, s, _ = q.shape
    hp = ATT_WIDTH // LANES
    nq = s // tq
    return pl.pallas_call(
        functools.partial(_attn_kernel, tq=tq),
        grid=(b, hp, nq),
        in_specs=[pl.BlockSpec((1, tq, LANES), lambda b_, h_, i: (b_, i, h_)),
                  pl.BlockSpec((1, s, LANES), lambda b_, h_, i: (b_, 0, h_)),
                  pl.BlockSpec((1, 1, s, LANES), lambda b_, h_, i: (b_, h_, 0, 0)),
                  pl.BlockSpec((1, 1, nq, LANES, tq), lambda b_, h_, i: (b_, h_, 0, 0, 0))],
        out_specs=pl.BlockSpec((1, tq, LANES), lambda b_, h_, i: (b_, i, h_)),
        out_shape=jax.ShapeDtypeStruct((b, s, ATT_WIDTH), bf16),
        scratch_shapes=[pltpu.VMEM((2, 2 * LANES, tq), bf16), pltpu.VMEM((2, HEAD_DIM, tq), f32),
                        pltpu.VMEM((tq, tq), f32), pltpu.VMEM((tq, tq), f32)],
        compiler_params=_params("parallel", "parallel", "arbitrary"),
        name="attn_prompt",
    )(q, kb, ext, vt)


def _attn_sample_kernel(qbd_ref, ck_ref, cv_ref, kn_ref, vn_ref, bc_ref, bn_ref, o_ref, *, t_new):
    past = ck_ref.shape[1] // N_HEADS

    def load_cache(ref):
        heads = [ref[0, pl.ds(h, past, stride=N_HEADS), :].astype(bf16) for h in range(N_HEADS)]
        return jnp.concatenate(heads, axis=1)

    qbd = qbd_ref[0]
    s_c = jnp.dot(load_cache(ck_ref), qbd, preferred_element_type=f32) + bc_ref[0]
    s_n = jnp.dot(kn_ref[0], qbd, preferred_element_type=f32) + bn_ref[0]
    m = jnp.maximum(jnp.max(s_c, axis=0, keepdims=True), jnp.max(s_n, axis=0, keepdims=True))
    p_c = jnp.exp2(s_c - m)
    p_n = jnp.exp2(s_n - m)
    inv = 1.0 / (jnp.sum(p_c, axis=0, keepdims=True) + jnp.sum(p_n, axis=0, keepdims=True))
    p_c = (p_c * inv).T.astype(bf16)
    p_n = (p_n * inv).T.astype(bf16)
    o_all = (jnp.dot(p_c, load_cache(cv_ref), preferred_element_type=f32)
             + jnp.dot(p_n, vn_ref[0], preferred_element_type=f32))
    head_of_lane = lax.broadcasted_iota(jnp.int32, (t_new, ATT_WIDTH), 1) // HEAD_DIM
    o = jnp.zeros((t_new, ATT_WIDTH), f32)
    for h in range(N_HEADS):
        o = o + jnp.where(head_of_lane == h, o_all[h * t_new:(h + 1) * t_new, :], 0.0)
    o_ref[0] = o.astype(bf16)


def _attn_sample(qbd, ck, cv, kn, vn, bias_c, bias_n, t_new):
    b, rows, _ = ck.shape
    p = rows // N_HEADS
    blk = lambda r, c: pl.BlockSpec((1, r, c), lambda i: (i, 0, 0))
    return pl.pallas_call(
        functools.partial(_attn_sample_kernel, t_new=t_new),
        grid=(b,),
        in_specs=[blk(ATT_WIDTH, LANES), blk(rows, HEAD_DIM), blk(rows, HEAD_DIM), blk(LANES, ATT_WIDTH),
                  blk(LANES, ATT_WIDTH), blk(p, LANES), blk(LANES, LANES)],
        out_specs=blk(t_new, ATT_WIDTH),
        out_shape=jax.ShapeDtypeStruct((b, t_new, ATT_WIDTH), bf16),
        compiler_params=_params("parallel"),
        name="attn_sample",
    )(qbd, ck, cv, kn, vn, bias_c, bias_n)


def _route(logits):
    lane = lax.broadcasted_iota(jnp.int32, logits.shape, 1).astype(f32)
    ninf = -jnp.inf
    gl = jnp.where(lane < N_GROUPS, logits, ninf)
    gmax = jnp.max(gl, axis=-1, keepdims=True)
    grp = jnp.min(jnp.where(gl == gmax, lane, float(LANES)), axis=-1, keepdims=True)
    p_sel = 1.0 / jnp.sum(jnp.exp(gl - gmax), axis=-1, keepdims=True)
    e_lo = EXPERT_LANE0 + grp * EXPERTS_PER_GROUP
    el = jnp.where(lane >= e_lo, jnp.where(lane < e_lo + EXPERTS_PER_GROUP, logits, ninf), ninf)
    t1 = jnp.max(el, axis=-1, keepdims=True)
    i1 = jnp.min(jnp.where(el == t1, lane, float(LANES)), axis=-1, keepdims=True)
    el2 = jnp.where(lane == i1, ninf, el)
    t2 = jnp.max(el2, axis=-1, keepdims=True)
    i2 = jnp.min(jnp.where(el2 == t2, lane, float(LANES)), axis=-1, keepdims=True)
    e2 = jnp.exp(t2 - t1)
    w1 = p_sel / (1.0 + e2)
    w2 = p_sel * e2 / (1.0 + e2)
    picks = (i1 - EXPERT_LANE0, i2 - EXPERT_LANE0, w1, w2)
    out = jnp.zeros(logits.shape, f32)
    for n, val in enumerate(picks):
        out = jnp.where(lane == float(n), val, out)
    return out


def _tail_kernel(x_ref, o_ref, u_ref, uprev_ref, ctx_ref, wdw_ref, bdw_ref, gcl_ref, bcl_ref,
                 wa_ref, wc_ref, wg_ref, wo_ref, g1_ref, b1_ref, wr_ref, br_ref,
                 h_ref, gates_ref, uext_sc, shift_sc, c_sc, *, cb, ct):
    i = pl.program_id(1)
    bb, t, _ = x_ref.shape
    n = bb * t

    x = x_ref[...].reshape(n, D_MODEL)
    g = _sigmoid(jnp.dot(x.astype(bf16), wg_ref[...], preferred_element_type=f32))
    branch_a = jnp.dot(o_ref[...].reshape(n, ATT_WIDTH), wa_ref[...], preferred_element_type=f32)

    uext_sc[:, 0:HALO, :] = jnp.where(i == 0, ctx_ref[...], uprev_ref[...])
    uext_sc[:, HALO:HALO + t, :] = u_ref[...]
    span = t + HALO - SUBLANES
    for r in range(1, SUBLANES):
        shift_sc[r - 1] = uext_sc[:, r:r + span, :]
    for b0 in range(0, bb, cb):
        for t0 in range(0, t, ct):
            acc = jnp.zeros((cb, ct, D_CONV), f32) + bdw_ref[...]
            for k in range(CONV_WIDTH):
                a, r = divmod(HALO_PAD + k, SUBLANES)
                r0 = t0 + a * SUBLANES
                src = uext_sc[b0:b0 + cb, r0:r0 + ct, :] if r == 0 else shift_sc[r - 1, b0:b0 + cb, r0:r0 + ct, :]
                acc = acc + wdw_ref[k:k + 1, :] * src
            c_sc[b0:b0 + cb, t0:t0 + ct, :] = acc

    c = _layernorm(c_sc[...].reshape(n, D_CONV), gcl_ref[...], bcl_ref[...])
    c = c * _sigmoid(c)
    branch_b = jnp.dot(c.astype(bf16), wc_ref[...], preferred_element_type=f32)
    merged = g[:, :D_MODEL] * branch_a + g[:, D_MODEL:] * branch_b
    mix = jnp.dot(merged.astype(bf16), wo_ref[...], preferred_element_type=f32)
    h = _layernorm(ALPHA * x + mix, g1_ref[...], b1_ref[...])
    h_ref[...] = h.reshape(bb, t, D_MODEL)
    w_hi, w_lo = _split2(wr_ref[...])
    w_parts = jnp.concatenate([w_hi, w_lo], axis=1)
    cross = sum(jnp.dot(part, w_parts, preferred_element_type=f32) for part in _split2(h))
    logits = cross[:, :LANES] + cross[:, LANES:] + br_ref[...]
    gates_ref[...] = _route(logits).reshape(bb, t, LANES)


def _tail(x, o, u, ctx, w, bb, t, cb, ct):
    b, s, _ = x.shape
    nt = s // t
    uprev = u if nt > 1 else ctx
    per_tile = t // HALO
    tile = lambda width: pl.BlockSpec((bb, t, width), lambda b_, i: (b_, i, 0))
    full = lambda a: pl.BlockSpec(a.shape, lambda b_, i: (0,) * a.ndim)
    weights = [w["w_dw"], w["b_dw"], w["g_cln"], w["b_cln"], w["w_attn_out"], w["w_conv_out"], w["w_gate"],
               w["w_out"], w["g_ln1"], w["b_ln1"], w["w_route"], w["b_route"]]
    return pl.pallas_call(
        functools.partial(_tail_kernel, cb=cb, ct=ct),
        grid=(b // bb, nt),
        in_specs=[tile(D_MODEL), tile(ATT_WIDTH), tile(D_CONV),
                  pl.BlockSpec((bb, HALO, D_CONV), lambda b_, i: (b_, jnp.maximum(i * per_tile - 1, 0), 0)),
                  pl.BlockSpec((bb, HALO, D_CONV), lambda b_, i: (b_, 0, 0))]
                 + [full(a) for a in weights],
        out_specs=[tile(D_MODEL), tile(LANES)],
        out_shape=[jax.ShapeDtypeStruct((b, s, D_MODEL), f32), jax.ShapeDtypeStruct((b, s, LANES), f32)],
        scratch_shapes=[pltpu.VMEM((bb, HALO + t, D_CONV), f32),
                        pltpu.VMEM((SUBLANES - 1, bb, t + HALO - SUBLANES, D_CONV), f32),
                        pltpu.VMEM((bb, t, D_CONV), f32)],
        compiler_params=_params("parallel", "arbitrary"),
        name="tail",
    )(x, o, u, uprev, ctx, *weights)


MOE_SUB = 256
MOE_CAP = 32
MOE_EPC = 8
MOE_KEY = 4096
_CAP_SHIFT = MOE_CAP.bit_length() - 1


def _moe_kernel(h_ref, r_ref, w1_ref, w3_ref, w2_ref, g2_ref, b2_ref, y_ref,
                acc_sc, hb_sc, key_sc, keyt_sc, nr_sc, *, n_sub):
    c = pl.program_id(1)
    slots = MOE_EPC * MOE_CAP

    @pl.when(c == 0)
    def _():
        acc_sc[...] = jnp.zeros(acc_sc.shape, f32)
        hb_sc[...] = h_ref[...].astype(bf16)
        lane = lax.broadcasted_iota(jnp.int32, (MOE_SUB, LANES), 1).astype(f32)
        ti = lax.broadcasted_iota(jnp.int32, (MOE_SUB, MOE_SUB), 0)
        tj = lax.broadcasted_iota(jnp.int32, (MOE_SUB, MOE_SUB), 1)
        earlier = jnp.where(tj < ti, 1.0, 0.0).astype(bf16)
        most = jnp.zeros((1, LANES), f32)
        for a in range(n_sub):
            r = r_ref[a * MOE_SUB:(a + 1) * MOE_SUB, :]
            e1, e2 = r[:, 0:1], r[:, 1:2]
            oh1 = jnp.where(lane == e1, 1.0, 0.0)
            oh2 = jnp.where(lane == e2, 1.0, 0.0)
            both = oh1 + oh2
            before = jnp.dot(earlier, both.astype(bf16), preferred_element_type=f32)
            key1 = e1 * MOE_KEY + jnp.sum(before * oh1, axis=-1, keepdims=True)
            key2 = e2 * MOE_KEY + jnp.sum(before * oh2, axis=-1, keepdims=True)
            info = jnp.where(lane == 0.0, key1, jnp.where(lane == 1.0, key2, r))
            key_sc[a] = info
            keyt_sc[a] = info.T[0:8, :]
            most = jnp.maximum(most, jnp.sum(both, axis=0, keepdims=True))
        nr_sc[0] = (jnp.max(most).astype(jnp.int32) + (MOE_CAP - 1)) // MOE_CAP

    e0 = c * MOE_EPC
    s_row = lax.broadcasted_iota(jnp.int32, (slots, MOE_SUB), 0)
    s_col = lax.broadcasted_iota(jnp.int32, (MOE_SUB, slots), 1)

    def slot_key(s, rd):
        expert = lax.shift_right_logical(s, _CAP_SHIFT) + e0
        return (expert * MOE_KEY + jnp.bitwise_and(s, MOE_CAP - 1) + rd * MOE_CAP).astype(f32)

    def one_round(rd, carry):
        key_r = slot_key(s_row, rd)
        key_c = slot_key(s_col, rd)
        xs = []
        for a in range(n_sub):
            k1, k2 = keyt_sc[a, 0:1, :], keyt_sc[a, 1:2, :]
            p = jnp.where(key_r == k1, 1.0, jnp.where(key_r == k2, 1.0, 0.0)).astype(bf16)
            hb = hb_sc[a * MOE_SUB:(a + 1) * MOE_SUB, :]
            xs.append(jnp.dot(p, hb, preferred_element_type=f32).astype(bf16))
        ys = [[] for _ in range(n_sub)]
        for e in range(MOE_EPC):
            xe = jnp.concatenate([x[e * MOE_CAP:(e + 1) * MOE_CAP] for x in xs], axis=0)
            a1 = jnp.dot(xe, w1_ref[e], preferred_element_type=f32)
            a3 = jnp.dot(xe, w3_ref[e], preferred_element_type=f32)
            act = (a1 * _sigmoid(a1)) * a3
            ye = jnp.dot(act.astype(bf16), w2_ref[e], preferred_element_type=f32).astype(bf16)
            for a in range(n_sub):
                ys[a].append(ye[a * MOE_CAP:(a + 1) * MOE_CAP])
        for a in range(n_sub):
            info = key_sc[a]
            pw = jnp.where(key_c == info[:, 0:1], info[:, 2:3],
                           jnp.where(key_c == info[:, 1:2], info[:, 3:4], 0.0)).astype(bf16)
            rows = slice(a * MOE_SUB, (a + 1) * MOE_SUB)
            acc_sc[rows, :] += jnp.dot(pw, jnp.concatenate(ys[a], axis=0), preferred_element_type=f32)
        return carry

    lax.fori_loop(0, nr_sc[0], one_round, 0)

    @pl.when(c == pl.num_programs(1) - 1)
    def _():
        y_ref[...] = _layernorm(ALPHA * h_ref[...] + acc_sc[...], g2_ref[...], b2_ref[...])


def _moe(h2d, route, w1, w3, w2, g2, b2, tm):
    n = h2d.shape[0]
    n_sub = tm // MOE_SUB
    expert_block = lambda a: pl.BlockSpec((MOE_EPC,) + a.shape[1:], lambda i, c: (c, 0, 0))
    return pl.pallas_call(
        functools.partial(_moe_kernel, n_sub=n_sub),
        grid=(n // tm, N_EXPERTS // MOE_EPC),
        in_specs=[pl.BlockSpec((tm, D_MODEL), lambda i, c: (i, 0)),
                  pl.BlockSpec((tm, LANES), lambda i, c: (i, 0)),
                  expert_block(w1), expert_block(w3), expert_block(w2),
                  pl.BlockSpec((1, D_MODEL), lambda i, c: (0, 0)),
                  pl.BlockSpec((1, D_MODEL), lambda i, c: (0, 0))],
        out_specs=pl.BlockSpec((tm, D_MODEL), lambda i, c: (i, 0)),
        out_shape=jax.ShapeDtypeStruct((n, D_MODEL), f32),
        scratch_shapes=[pltpu.VMEM((tm, D_MODEL), f32), pltpu.VMEM((tm, D_MODEL), bf16),
                        pltpu.VMEM((n_sub, MOE_SUB, LANES), f32), pltpu.VMEM((n_sub, 8, MOE_SUB), f32),
                        pltpu.SMEM((1,), jnp.int32)],
        compiler_params=_params("parallel", "arbitrary"),
        name="moe",
    )(h2d, route, w1, w3, w2, g2, b2)


def _pack_weights(w_in, b_forget, w_dw, b_dw, g_conv_ln, b_conv_ln, w_attn_out, w_conv_out, w_out,
                  g_ln1, b_ln1, w_group, b_group, w_router, b_router, w1, w3, w2, g_ln2, b_ln2):
    a = ATT_WIDTH
    c_f, c_glu, c_gate = 3 * a, 3 * a + N_HEADS, 3 * a + N_HEADS + 2 * D_CONV
    w_f = jnp.pad(w_in[:, c_f:c_glu], ((0, 0), (0, LANES - N_HEADS)))
    w_pack = jnp.concatenate([w_in[:, :c_f], w_in[:, c_glu:c_gate], w_f], axis=1).astype(bf16)
    pad_route = LANES - N_GROUPS - N_EXPERTS
    row = lambda v: v.reshape(1, -1)
    return dict(
        w_pack=w_pack,
        b_forget=jnp.pad(b_forget, (0, LANES - N_HEADS)).reshape(1, LANES),
        w_gate=w_in[:, c_gate:].astype(bf16),
        w_dw=jnp.pad(w_dw, ((0, HALO - CONV_WIDTH), (0, 0))), b_dw=row(b_dw),
        g_cln=row(g_conv_ln), b_cln=row(b_conv_ln),
        w_attn_out=w_attn_out.astype(bf16), w_conv_out=w_conv_out.astype(bf16), w_out=w_out.astype(bf16),
        g_ln1=row(g_ln1), b_ln1=row(b_ln1),
        w_route=jnp.pad(jnp.concatenate([w_group, w_router], axis=1), ((0, 0), (0, pad_route))),
        b_route=jnp.pad(jnp.concatenate([b_group, b_router]), (0, pad_route)).reshape(1, LANES),
        w1=w1.astype(bf16), w3=w3.astype(bf16), w2=w2.astype(bf16),
        g_ln2=row(g_ln2), b_ln2=row(b_ln2),
    )


def _finish(x, o, u, ctx, w, bb, t, cb, ct, tm_moe):
    b, s, _ = x.shape
    h, gates = _tail(x, o, u, ctx, w, bb, t, cb, ct)
    y = _moe(h.reshape(b * s, D_MODEL), gates.reshape(b * s, LANES), w["w1"], w["w3"], w["w2"],
             w["g_ln2"], w["b_ln2"], tm_moe)
    return y.reshape(b, s, D_MODEL)


def _layer(xp, xs, cache_k, cache_v, cache_logf, cache_conv, w):
    bp, sp, _ = xp.shape
    bs, ts, _ = xs.shape
    past = cache_k.shape[1]
    tq = 512

    q, k, kb, v, vb, lf, u = _in_proj(xp.reshape(bp * sp, D_MODEL), w["w_pack"], w["b_forget"], 512)
    logf = lf[:, :N_HEADS].reshape(bp, sp, N_HEADS)
    parts = _cumsum_lanes(logf.transpose(0, 2, 1).reshape(bp * N_HEADS, sp), reverse=False, split_scale=-LOG2E)
    hp = ATT_WIDTH // LANES
    parts = parts.transpose(1, 0, 2).reshape(bp, hp, 2 * N_SPLIT, sp)
    ext = jnp.pad(parts.transpose(0, 1, 3, 2), ((0, 0), (0, 0), (0, 0), (0, LANES - 2 * N_SPLIT)))
    shp = (bp, sp, ATT_WIDTH)
    vt = vb.reshape(bp, sp // tq, tq, hp, LANES).transpose(0, 3, 1, 4, 2)
    o = _attn_prompt(q.reshape(shp), kb.reshape(shp), ext, vt, tq)
    u = u.reshape(bp, sp, D_CONV)
    yp = _finish(xp, o, u, jnp.zeros((bp, HALO, D_CONV), f32), w, 1, 512, 1, 32, 4 * MOE_SUB)
    outs_p = (k.reshape(bp, sp, N_HEADS, HEAD_DIM), v.reshape(bp, sp, N_HEADS, HEAD_DIM), logf,
              u[:, sp - CONV_CTX:])

    q, k, kb, v, vb, lf, u = _in_proj(xs.reshape(bs * ts, D_MODEL), w["w_pack"], w["b_forget"], bs * ts)
    logf = lf[:, :N_HEADS].reshape(bs, ts, N_HEADS)
    lf_t = jnp.pad(logf.transpose(0, 2, 1), ((0, 0), (0, 0), (0, LANES - ts)))
    cq = _cumsum_lanes(lf_t.reshape(bs * N_HEADS, LANES), reverse=False).reshape(bs, N_HEADS, LANES)[:, :, :ts]
    r = _cumsum_lanes(cache_logf.transpose(0, 2, 1).reshape(bs * N_HEADS, past), reverse=True)
    bias_c = jnp.repeat(LOG2E * r.reshape(bs, N_HEADS, past).transpose(0, 2, 1), ts, axis=2)
    causal = jnp.arange(ts)[:, None] <= jnp.arange(ts)[None, :]
    bias_n = jnp.where(causal[None, :, None, :], -LOG2E * cq.transpose(0, 2, 1)[:, :, :, None], NEG)
    bias_n = jnp.pad(bias_n.reshape(bs, ts, LANES), ((0, 0), (0, LANES - ts), (0, 0)), constant_values=NEG)
    q4 = q.reshape(bs, ts, N_HEADS, HEAD_DIM).transpose(0, 2, 3, 1)
    qbd = (q4[:, :, :, None, :] * jnp.eye(N_HEADS, dtype=bf16)[None, :, None, :, None]).reshape(bs, ATT_WIDTH, LANES)
    pad_rows = lambda a: jnp.pad(a.reshape(bs, ts, ATT_WIDTH), ((0, 0), (0, LANES - ts), (0, 0)))
    o = _attn_sample(qbd, cache_k.reshape(bs, past * N_HEADS, HEAD_DIM), cache_v.reshape(bs, past * N_HEADS, HEAD_DIM),
                     pad_rows(kb), pad_rows(vb), bias_c, bias_n, ts)
    u = u.reshape(bs, ts, D_CONV)
    ctx = jnp.pad(cache_conv, ((0, 0), (HALO_PAD, 0), (0, 0)))
    ys = _finish(xs, o, u, ctx, w, bs, ts, 2, ts, bs * ts)
    u_ext = jnp.concatenate([cache_conv, u], axis=1)
    outs_s = (k.reshape(bs, ts, N_HEADS, HEAD_DIM), v.reshape(bs, ts, N_HEADS, HEAD_DIM), logf,
              u_ext[:, u_ext.shape[1] - CONV_CTX:])
    return yp, ys, outs_p, outs_s


def kernel(x_prompt, x_sample, cache_k, cache_v, cache_logf, cache_conv, w_in, b_forget, w_dw, b_dw,
           g_conv_ln, b_conv_ln, w_attn_out, w_conv_out, w_out, g_ln1, b_ln1, w_group, b_group,
           w_router, b_router, w1, w3, w2, g_ln2, b_ln2):
    xp, xs = x_prompt, x_sample
    per_layer_p, per_layer_s = [], []
    for l in range(DEPTH):
        w = _pack_weights(w_in[l], b_forget[l], w_dw[l], b_dw[l], g_conv_ln[l], b_conv_ln[l], w_attn_out[l],
                          w_conv_out[l], w_out[l], g_ln1[l], b_ln1[l], w_group[l], b_group[l], w_router[l],
                          b_router[l], w1[l], w3[l], w2[l], g_ln2[l], b_ln2[l])
        xp, xs, outs_p, outs_s = _layer(xp, xs, cache_k[l], cache_v[l], cache_logf[l], cache_conv[l], w)
        per_layer_p.append(outs_p)
        per_layer_s.append(outs_s)
    stack = lambda outs, j: jnp.stack([o[j] for o in outs])
    return (xp, xs,
            stack(per_layer_p, 0), stack(per_layer_p, 1), stack(per_layer_p, 2), stack(per_layer_p, 3),
            stack(per_layer_s, 0), stack(per_layer_s, 1), stack(per_layer_s, 2), stack(per_layer_s, 3))
```

```python
import functools

import jax
import jax.numpy as jnp
from jax import lax
from jax.experimental import pallas as pl
from jax.experimental.pallas import tpu as pltpu

f32 = jnp.float32
bf16 = jnp.bfloat16

D_MODEL = 1024
N_HEADS = 8
HEAD_DIM = 64
ATT_WIDTH = N_HEADS * HEAD_DIM
D_CONV = 512
CONV_WIDTH = 31
CONV_CTX = CONV_WIDTH - 1
N_GROUPS = 4
EXPERTS_PER_GROUP = 8
N_EXPERTS = N_GROUPS * EXPERTS_PER_GROUP
D_EXPERT = 256
DEPTH = 1
ALPHA = (2 * DEPTH) ** 0.25
LN_EPS = 1e-5
NEG = -1e30
SCALE = HEAD_DIM ** -0.5
LOG2E = 1.4426950408889634

N_SPLIT = 3
LANES = 128
SUBLANES = 8
HALO = 32
HALO_PAD = HALO - CONV_CTX
EXPERT_LANE0 = N_GROUPS
VMEM_LIMIT = 56 * 1024 * 1024

_C_Q, _C_K, _C_V, _C_GA, _C_GB, _C_F, _C_END = 0, 512, 1024, 1536, 2048, 2560, 2688


def _sigmoid(x):
    return 1.0 / (1.0 + jnp.exp(-x))


def _layernorm(x, g, b):
    mu = jnp.mean(x, axis=-1, keepdims=True)
    xc = x - mu
    var = jnp.mean(xc * xc, axis=-1, keepdims=True)
    return xc * lax.rsqrt(var + LN_EPS) * g + b


def _params(*sem):
    return pltpu.CompilerParams(dimension_semantics=sem, vmem_limit_bytes=VMEM_LIMIT)


def _store_head_major(ref, val):
    tokens = val.shape[0]
    for h in range(N_HEADS):
        ref[pl.ds(h, tokens, stride=N_HEADS), :] = val[:, h * HEAD_DIM:(h + 1) * HEAD_DIM]


def _in_proj_kernel(x_ref, w_ref, bf_ref, q_ref, k_ref, kb_ref, v_ref, vb_ref, lf_ref, u_ref):
    xb = x_ref[...].astype(bf16)

    def proj(lo, hi):
        return jnp.dot(xb, w_ref[:, lo:hi], preferred_element_type=f32)

    q_ref[...] = (proj(_C_Q, _C_K) * (SCALE * LOG2E)).astype(bf16)
    k = proj(_C_K, _C_V)
    _store_head_major(k_ref, k)
    kb_ref[...] = k.astype(bf16)
    v = proj(_C_V, _C_GA)
    _store_head_major(v_ref, v)
    vb_ref[...] = v.astype(bf16)
    u_ref[...] = proj(_C_GA, _C_GB) * _sigmoid(proj(_C_GB, _C_F))
    z = proj(_C_F, _C_END) + bf_ref[...]
    lf_ref[...] = jnp.minimum(z, 0.0) - jnp.log(1.0 + jnp.exp(-jnp.abs(z)))


def _in_proj(x2d, w_pack, bf_pad, tm):
    n = x2d.shape[0]
    row = lambda width: pl.BlockSpec((tm, width), lambda i: (i, 0))
    heads = pl.BlockSpec((tm * N_HEADS, HEAD_DIM), lambda i: (i, 0))
    return pl.pallas_call(
        _in_proj_kernel,
        grid=(n // tm,),
        in_specs=[row(D_MODEL),
                  pl.BlockSpec((D_MODEL, _C_END), lambda i: (0, 0)),
                  pl.BlockSpec((1, LANES), lambda i: (0, 0))],
        out_specs=[row(ATT_WIDTH), heads, row(ATT_WIDTH), heads, row(ATT_WIDTH),
                   row(LANES), row(D_CONV)],
        out_shape=[jax.ShapeDtypeStruct((n, ATT_WIDTH), bf16),
                   jax.ShapeDtypeStruct((n * N_HEADS, HEAD_DIM), f32),
                   jax.ShapeDtypeStruct((n, ATT_WIDTH), bf16),
                   jax.ShapeDtypeStruct((n * N_HEADS, HEAD_DIM), f32),
                   jax.ShapeDtypeStruct((n, ATT_WIDTH), bf16),
                   jax.ShapeDtypeStruct((n, LANES), f32),
                   jax.ShapeDtypeStruct((n, D_CONV), f32)],
        compiler_params=_params("parallel"),
        name="in_proj",
    )(x2d, w_pack, bf_pad)


def _split2(x):
    hi = x.astype(bf16)
    return hi, (x - hi.astype(f32)).astype(bf16)


def _split3(x):
    hi = x.astype(bf16)
    r1 = x - hi.astype(f32)
    mid = r1.astype(bf16)
    lo = (r1 - mid.astype(f32)).astype(bf16)
    return hi, mid, lo


def _cumsum_kernel(x_ref, o_ref, *, reverse, split_scale):
    nc, rows, _ = x_ref.shape
    r = lax.broadcasted_iota(jnp.int32, (LANES, LANES), 0)
    c = lax.broadcasted_iota(jnp.int32, (LANES, LANES), 1)
    tri = jnp.where((r > c) if reverse else (r <= c), 1.0, 0.0).astype(bf16)
    carry = jnp.zeros((rows, 1), f32)
    for j in (range(nc - 1, -1, -1) if reverse else range(nc)):
        xj = x_ref[j]
        y = carry
        for part in _split3(xj):
            y = y + jnp.dot(part, tri, preferred_element_type=f32)
        if split_scale is None:
            o_ref[j] = y
        else:
            for n, part in enumerate(_split3(split_scale * y)):
                o_ref[n, j] = part
        carry = carry + jnp.sum(xj, axis=-1, keepdims=True)


def _cumsum_lanes(x, reverse, split_scale=None):
    rows, s = x.shape
    nc = s // LANES
    xc = x.reshape(rows, nc, LANES).transpose(1, 0, 2)
    shape = (nc, rows, LANES) if split_scale is None else (N_SPLIT, nc, rows, LANES)
    out = pl.pallas_call(
        functools.partial(_cumsum_kernel, reverse=reverse, split_scale=split_scale),
        out_shape=jax.ShapeDtypeStruct(shape, f32 if split_scale is None else bf16),
        name="cumsum_rev" if reverse else "cumsum_fwd",
    )(xc)
    if split_scale is None:
        return out.transpose(1, 0, 2).reshape(rows, s)
    return out.transpose(0, 2, 1, 3).reshape(N_SPLIT, rows, s)


def _attn_kernel(q_ref, k_ref, e_ref, vt_ref, o_ref, qx_sc, acc_sc, s0_sc, s1_sc, *, tq):
    i = pl.program_id(2)
    qt = q_ref[0].astype(f32).T
    row = lax.broadcasted_iota(jnp.int32, (LANES, tq), 0)
    for h in range(2):
        qx_sc[h, 0:LANES, :] = jnp.where(row // HEAD_DIM == h, qt, 0.0).astype(bf16)
        qx_sc[h, LANES:2 * LANES, :] = jnp.where(row // N_SPLIT == h, 1.0, 0.0).astype(bf16)
    acc_sc[...] = jnp.zeros(acc_sc.shape, f32)
    ones_rows = jnp.ones((16, tq), bf16)

    s_sc = (s0_sc, s1_sc)

    def scores(j, h):
        start = pl.multiple_of(j * tq, tq)
        kx = jnp.concatenate([k_ref[0, pl.ds(start, tq), :], e_ref[0, 0, pl.ds(start, tq), :]], axis=1)
        s_sc[h][...] = jnp.dot(kx, qx_sc[h], preferred_element_type=f32)

    def absorb(j, h, stats, masked):
        m_prev, l_prev = stats
        s = s_sc[h][...]
        if masked:
            key = lax.broadcasted_iota(jnp.int32, (tq, tq), 0)
            qry = lax.broadcasted_iota(jnp.int32, (tq, tq), 1)
            s = jnp.where(key <= qry, s, NEG)
        m_new = jnp.maximum(m_prev, jnp.max(s, axis=0, keepdims=True))
        alpha = jnp.exp2(m_prev - m_new)
        p = jnp.exp2(s - m_new).astype(bf16)
        vx = jnp.concatenate([vt_ref[0, 0, j, h * HEAD_DIM:(h + 1) * HEAD_DIM, :], ones_rows], axis=0)
        pv = jnp.dot(vx, p, preferred_element_type=f32)
        acc_sc[h] = alpha * acc_sc[h] + pv[0:HEAD_DIM, :]
        return m_new, alpha * l_prev + pv[HEAD_DIM:HEAD_DIM + 1, :]

    def body(j, stats):
        scores(j, 1)
        st0 = absorb(j, 0, stats[0], False)
        scores(j + 1, 0)
        st1 = absorb(j, 1, stats[1], False)
        return st0, st1

    init = tuple((jnp.full((1, tq), NEG, f32), jnp.zeros((1, tq), f32)) for _ in range(2))
    scores(0, 0)
    stats = lax.fori_loop(0, i // 2, lambda p, st: body(2 * p + 1, body(2 * p, st)), init)
    stats = lax.cond(i % 2 == 1, lambda st: body(i - 1, st), lambda st: st, stats)
    scores(i, 1)
    _, l0 = absorb(i, 0, stats[0], True)
    _, l1 = absorb(i, 1, stats[1], True)
    ot = jnp.concatenate([acc_sc[0] / l0, acc_sc[1] / l1], axis=0)
    o_ref[0] = ot.T.astype(bf16)


def _attn_prompt(q, kb, ext, vt, tq):
    b, s, _ = q.shape
    hp = ATT_WIDTH // LANES
    nq = s // tq
    return pl.pallas_call(
        functools.partial(_attn_kernel, tq=tq),
        grid=(b, hp, nq),
        in_specs=[pl.BlockSpec((1, tq, LANES), lambda b_, h_, i: (b_, i, h_)),
                  pl.BlockSpec((1, s, LANES), lambda b_, h_, i: (b_, 0, h_)),
                  pl.BlockSpec((1, 1, s, LANES), lambda b_, h_, i: (b_, h_, 0, 0)),
                  pl.BlockSpec((1, 1, nq, LANES, tq), lambda b_, h_, i: (b_, h_, 0, 0, 0))],
        out_specs=pl.BlockSpec((1, tq, LANES), lambda b_, h_, i: (b_, i, h_)),
        out_shape=jax.ShapeDtypeStruct((b, s, ATT_WIDTH), bf16),
        scratch_shapes=[pltpu.VMEM((2, 2 * LANES, tq), bf16), pltpu.VMEM((2, HEAD_DIM, tq), f32),
                        pltpu.VMEM((tq, tq), f32), pltpu.VMEM((tq, tq), f32)],
        compiler_params=_params("parallel", "parallel", "arbitrary"),
        name="attn_prompt",
    )(q, kb, ext, vt)


def _attn_sample_kernel(qbd_ref, ck_ref, cv_ref, kn_ref, vn_ref, bc_ref, bn_ref, o_ref, *, t_new):
    qbd = qbd_ref[0]
    s_c = jnp.dot(qbd, ck_ref[0].astype(bf16), preferred_element_type=f32) + bc_ref[0]
    s_n = jnp.dot(qbd, kn_ref[0], preferred_element_type=f32) + bn_ref[0]
    m = jnp.maximum(jnp.max(s_c, axis=-1, keepdims=True), jnp.max(s_n, axis=-1, keepdims=True))
    p_c = jnp.exp2(s_c - m)
    p_n = jnp.exp2(s_n - m)
    inv = 1.0 / (jnp.sum(p_c, axis=-1, keepdims=True) + jnp.sum(p_n, axis=-1, keepdims=True))
    p_c = (p_c * inv).astype(bf16)
    p_n = (p_n * inv).astype(bf16)
    over_positions = (((1,), (1,)), ((), ()))
    o_all = (lax.dot_general(cv_ref[0].astype(bf16), p_c, over_positions, preferred_element_type=f32)
             + lax.dot_general(vn_ref[0], p_n, over_positions, preferred_element_type=f32))
    shape = (ATT_WIDTH, LANES)
    row_head = lax.shift_right_logical(lax.broadcasted_iota(jnp.int32, shape, 0), HEAD_DIM.bit_length() - 1)
    col_head = lax.shift_right_logical(lax.broadcasted_iota(jnp.int32, shape, 1), t_new.bit_length() - 1)
    own = jnp.where(row_head == col_head, o_all, 0.0).astype(bf16)
    q_of_row = jnp.bitwise_and(lax.broadcasted_iota(jnp.int32, (LANES, LANES), 0), t_new - 1)
    pick = jnp.where(q_of_row == lax.broadcasted_iota(jnp.int32, (LANES, LANES), 1), 1.0, 0.0).astype(bf16)
    o_t = jnp.dot(own, pick, preferred_element_type=f32)
    o_ref[0] = o_t.T[0:t_new, :].astype(bf16)


def _attn_sample(qbd, ck, cv, kn, vn, bias_c, bias_n, t_new):
    b, _, p = ck.shape
    blk = lambda r, c: pl.BlockSpec((1, r, c), lambda i: (i, 0, 0))
    return pl.pallas_call(
        functools.partial(_attn_sample_kernel, t_new=t_new),
        grid=(b,),
        in_specs=[blk(LANES, ATT_WIDTH), blk(ATT_WIDTH, p), blk(ATT_WIDTH, p), blk(ATT_WIDTH, LANES),
                  blk(ATT_WIDTH, LANES), blk(LANES, p), blk(LANES, LANES)],
        out_specs=blk(t_new, ATT_WIDTH),
        out_shape=jax.ShapeDtypeStruct((b, t_new, ATT_WIDTH), bf16),
        compiler_params=_params("parallel"),
        name="attn_sample",
    )(qbd, ck, cv, kn, vn, bias_c, bias_n)


def _route(logits):
    lane = lax.broadcasted_iota(jnp.int32, logits.shape, 1).astype(f32)
    ninf = -jnp.inf
    gl = jnp.where(lane < N_GROUPS, logits, ninf)
    gmax = jnp.max(gl, axis=-1, keepdims=True)
    grp = jnp.min(jnp.where(gl == gmax, lane, float(LANES)), axis=-1, keepdims=True)
    p_sel = 1.0 / jnp.sum(jnp.exp(gl - gmax), axis=-1, keepdims=True)
    e_lo = EXPERT_LANE0 + grp * EXPERTS_PER_GROUP
    el = jnp.where(lane >= e_lo, jnp.where(lane < e_lo + EXPERTS_PER_GROUP, logits, ninf), ninf)
    t1 = jnp.max(el, axis=-1, keepdims=True)
    i1 = jnp.min(jnp.where(el == t1, lane, float(LANES)), axis=-1, keepdims=True)
    el2 = jnp.where(lane == i1, ninf, el)
    t2 = jnp.max(el2, axis=-1, keepdims=True)
    i2 = jnp.min(jnp.where(el2 == t2, lane, float(LANES)), axis=-1, keepdims=True)
    e2 = jnp.exp(t2 - t1)
    w1 = p_sel / (1.0 + e2)
    w2 = p_sel * e2 / (1.0 + e2)
    picks = (i1 - EXPERT_LANE0, i2 - EXPERT_LANE0, w1, w2)
    out = jnp.zeros(logits.shape, f32)
    for n, val in enumerate(picks):
        out = jnp.where(lane == float(n), val, out)
    return out


def _tail_kernel(x_ref, o_ref, u_ref, uprev_ref, ctx_ref, wdw_ref, bdw_ref, gcl_ref, bcl_ref,
                 wa_ref, wc_ref, wg_ref, wo_ref, g1_ref, b1_ref, wr_ref, br_ref,
                 h_ref, gates_ref, uext_sc, shift_sc, c_sc, *, cb, ct):
    i = pl.program_id(1)
    bb, t, _ = x_ref.shape
    n = bb * t

    x = x_ref[...].reshape(n, D_MODEL)
    g = _sigmoid(jnp.dot(x.astype(bf16), wg_ref[...], preferred_element_type=f32))
    branch_a = jnp.dot(o_ref[...].reshape(n, ATT_WIDTH), wa_ref[...], preferred_element_type=f32)

    uext_sc[:, 0:HALO, :] = jnp.where(i == 0, ctx_ref[...], uprev_ref[...])
    uext_sc[:, HALO:HALO + t, :] = u_ref[...]
    span = t + HALO - SUBLANES
    for r in range(1, SUBLANES):
        shift_sc[r - 1] = uext_sc[:, r:r + span, :]
    for b0 in range(0, bb, cb):
        for t0 in range(0, t, ct):
            acc = jnp.zeros((cb, ct, D_CONV), f32) + bdw_ref[...]
            for k in range(CONV_WIDTH):
                a, r = divmod(HALO_PAD + k, SUBLANES)
                r0 = t0 + a * SUBLANES
                src = uext_sc[b0:b0 + cb, r0:r0 + ct, :] if r == 0 else shift_sc[r - 1, b0:b0 + cb, r0:r0 + ct, :]
                acc = acc + wdw_ref[k:k + 1, :] * src
            c_sc[b0:b0 + cb, t0:t0 + ct, :] = acc

    c = _layernorm(c_sc[...].reshape(n, D_CONV), gcl_ref[...], bcl_ref[...])
    c = c * _sigmoid(c)
    branch_b = jnp.dot(c.astype(bf16), wc_ref[...], preferred_element_type=f32)
    merged = g[:, :D_MODEL] * branch_a + g[:, D_MODEL:] * branch_b
    mix = jnp.dot(merged.astype(bf16), wo_ref[...], preferred_element_type=f32)
    h = _layernorm(ALPHA * x + mix, g1_ref[...], b1_ref[...])
    h_ref[...] = h.reshape(bb, t, D_MODEL)
    w_hi, w_lo = _split2(wr_ref[...])
    w_parts = jnp.concatenate([w_hi, w_lo], axis=1)
    cross = sum(jnp.dot(part, w_parts, preferred_element_type=f32) for part in _split2(h))
    logits = cross[:, :LANES] + cross[:, LANES:] + br_ref[...]
    gates_ref[...] = _route(logits).reshape(bb, t, LANES)


def _tail(x, o, u, ctx, w, bb, t, cb, ct):
    b, s, _ = x.shape
    nt = s // t
    uprev = u if nt > 1 else ctx
    per_tile = t // HALO
    tile = lambda width: pl.BlockSpec((bb, t, width), lambda b_, i: (b_, i, 0))
    full = lambda a: pl.BlockSpec(a.shape, lambda b_, i: (0,) * a.ndim)
    weights = [w["w_dw"], w["b_dw"], w["g_cln"], w["b_cln"], w["w_attn_out"], w["w_conv_out"], w["w_gate"],
               w["w_out"], w["g_ln1"], w["b_ln1"], w["w_route"], w["b_route"]]
    return pl.pallas_call(
        functools.partial(_tail_kernel, cb=cb, ct=ct),
        grid=(b // bb, nt),
        in_specs=[tile(D_MODEL), tile(ATT_WIDTH), tile(D_CONV),
                  pl.BlockSpec((bb, HALO, D_CONV), lambda b_, i: (b_, jnp.maximum(i * per_tile - 1, 0), 0)),
                  pl.BlockSpec((bb, HALO, D_CONV), lambda b_, i: (b_, 0, 0))]
                 + [full(a) for a in weights],
        out_specs=[tile(D_MODEL), tile(LANES)],
        out_shape=[jax.ShapeDtypeStruct((b, s, D_MODEL), f32), jax.ShapeDtypeStruct((b, s, LANES), f32)],
        scratch_shapes=[pltpu.VMEM((bb, HALO + t, D_CONV), f32),
                        pltpu.VMEM((SUBLANES - 1, bb, t + HALO - SUBLANES, D_CONV), f32),
                        pltpu.VMEM((bb, t, D_CONV), f32)],
        compiler_params=_params("parallel", "arbitrary"),
        name="tail",
    )(x, o, u, uprev, ctx, *weights)


MOE_SUB = 256
MOE_CAP = 32
MOE_EPC = 8
MOE_KEY = 4096
_CAP_SHIFT = MOE_CAP.bit_length() - 1


def _moe_kernel(h_ref, r_ref, w1_ref, w3_ref, w2_ref, g2_ref, b2_ref, y_ref,
                acc_sc, hb_sc, key_sc, keyt_sc, nr_sc, *, n_sub):
    c = pl.program_id(1)
    slots = MOE_EPC * MOE_CAP

    @pl.when(c == 0)
    def _():
        acc_sc[...] = jnp.zeros(acc_sc.shape, f32)
        hb_sc[...] = h_ref[...].astype(bf16)
        lane = lax.broadcasted_iota(jnp.int32, (MOE_SUB, LANES), 1).astype(f32)
        ti = lax.broadcasted_iota(jnp.int32, (MOE_SUB, MOE_SUB), 0)
        tj = lax.broadcasted_iota(jnp.int32, (MOE_SUB, MOE_SUB), 1)
        earlier = jnp.where(tj < ti, 1.0, 0.0).astype(bf16)
        most = jnp.zeros((1, LANES), f32)
        for a in range(n_sub):
            r = r_ref[a * MOE_SUB:(a + 1) * MOE_SUB, :]
            e1, e2 = r[:, 0:1], r[:, 1:2]
            oh1 = jnp.where(lane == e1, 1.0, 0.0)
            oh2 = jnp.where(lane == e2, 1.0, 0.0)
            both = oh1 + oh2
            before = jnp.dot(earlier, both.astype(bf16), preferred_element_type=f32)
            key1 = e1 * MOE_KEY + jnp.sum(before * oh1, axis=-1, keepdims=True)
            key2 = e2 * MOE_KEY + jnp.sum(before * oh2, axis=-1, keepdims=True)
            info = jnp.where(lane == 0.0, key1, jnp.where(lane == 1.0, key2, r))
            key_sc[a] = info
            keyt_sc[a] = info.T[0:8, :]
            most = jnp.maximum(most, jnp.sum(both, axis=0, keepdims=True))
        nr_sc[0] = (jnp.max(most).astype(jnp.int32) + (MOE_CAP - 1)) // MOE_CAP

    e0 = c * MOE_EPC
    s_row = lax.broadcasted_iota(jnp.int32, (slots, MOE_SUB), 0)
    s_col = lax.broadcasted_iota(jnp.int32, (MOE_SUB, slots), 1)

    def slot_key(s, rd):
        expert = lax.shift_right_logical(s, _CAP_SHIFT) + e0
        return (expert * MOE_KEY + jnp.bitwise_and(s, MOE_CAP - 1) + rd * MOE_CAP).astype(f32)

    def one_round(rd, carry):
        key_r = slot_key(s_row, rd)
        key_c = slot_key(s_col, rd)
        xs = []
        for a in range(n_sub):
            k1, k2 = keyt_sc[a, 0:1, :], keyt_sc[a, 1:2, :]
            p = jnp.where(key_r == k1, 1.0, jnp.where(key_r == k2, 1.0, 0.0)).astype(bf16)
            hb = hb_sc[a * MOE_SUB:(a + 1) * MOE_SUB, :]
            xs.append(jnp.dot(p, hb, preferred_element_type=f32).astype(bf16))
        ys = [[] for _ in range(n_sub)]
        for e in range(MOE_EPC):
            xe = jnp.concatenate([x[e * MOE_CAP:(e + 1) * MOE_CAP] for x in xs], axis=0)
            a1 = jnp.dot(xe, w1_ref[e], preferred_element_type=f32)
            a3 = jnp.dot(xe, w3_ref[e], preferred_element_type=f32)
            act = (a1 * _sigmoid(a1)) * a3
            ye = jnp.dot(act.astype(bf16), w2_ref[e], preferred_element_type=f32).astype(bf16)
            for a in range(n_sub):
                ys[a].append(ye[a * MOE_CAP:(a + 1) * MOE_CAP])
        for a in range(n_sub):
            info = key_sc[a]
            pw = jnp.where(key_c == info[:, 0:1], info[:, 2:3],
                           jnp.where(key_c == info[:, 1:2], info[:, 3:4], 0.0)).astype(bf16)
            rows = slice(a * MOE_SUB, (a + 1) * MOE_SUB)
            acc_sc[rows, :] += jnp.dot(pw, jnp.concatenate(ys[a], axis=0), preferred_element_type=f32)
        return carry

    lax.fori_loop(0, nr_sc[0], one_round, 0)

    @pl.when(c == pl.num_programs(1) - 1)
    def _():
        y_ref[...] = _layernorm(ALPHA * h_ref[...] + acc_sc[...], g2_ref[...], b2_ref[...])


def _moe(h2d, route, w1, w3, w2, g2, b2, tm):
    n = h2d.shape[0]
    n_sub = tm // MOE_SUB
    expert_block = lambda a: pl.BlockSpec((MOE_EPC,) + a.shape[1:], lambda i, c: (c, 0, 0))
    return pl.pallas_call(
        functools.partial(_moe_kernel, n_sub=n_sub),
        grid=(n // tm, N_EXPERTS // MOE_EPC),
        in_specs=[pl.BlockSpec((tm, D_MODEL), lambda i, c: (i, 0)),
                  pl.BlockSpec((tm, LANES), lambda i, c: (i, 0)),
                  expert_block(w1), expert_block(w3), expert_block(w2),
                  pl.BlockSpec((1, D_MODEL), lambda i, c: (0, 0)),
                  pl.BlockSpec((1, D_MODEL), lambda i, c: (0, 0))],
        out_specs=pl.BlockSpec((tm, D_MODEL), lambda i, c: (i, 0)),
        out_shape=jax.ShapeDtypeStruct((n, D_MODEL), f32),
        scratch_shapes=[pltpu.VMEM((tm, D_MODEL), f32), pltpu.VMEM((tm, D_MODEL), bf16),
                        pltpu.VMEM((n_sub, MOE_SUB, LANES), f32), pltpu.VMEM((n_sub, 8, MOE_SUB), f32),
                        pltpu.SMEM((1,), jnp.int32)],
        compiler_params=_params("parallel", "arbitrary"),
        name="moe",
    )(h2d, route, w1, w3, w2, g2, b2)


def _pack_weights(w_in, b_forget, w_dw, b_dw, g_conv_ln, b_conv_ln, w_attn_out, w_conv_out, w_out,
                  g_ln1, b_ln1, w_group, b_group, w_router, b_router, w1, w3, w2, g_ln2, b_ln2):
    a = ATT_WIDTH
    c_f, c_glu, c_gate = 3 * a, 3 * a + N_HEADS, 3 * a + N_HEADS + 2 * D_CONV
    w_f = jnp.pad(w_in[:, c_f:c_glu], ((0, 0), (0, LANES - N_HEADS)))
    w_pack = jnp.concatenate([w_in[:, :c_f], w_in[:, c_glu:c_gate], w_f], axis=1).astype(bf16)
    pad_route = LANES - N_GROUPS - N_EXPERTS
    row = lambda v: v.reshape(1, -1)
    return dict(
        w_pack=w_pack,
        b_forget=jnp.pad(b_forget, (0, LANES - N_HEADS)).reshape(1, LANES),
        w_gate=w_in[:, c_gate:].astype(bf16),
        w_dw=jnp.pad(w_dw, ((0, HALO - CONV_WIDTH), (0, 0))), b_dw=row(b_dw),
        g_cln=row(g_conv_ln), b_cln=row(b_conv_ln),
        w_attn_out=w_attn_out.astype(bf16), w_conv_out=w_conv_out.astype(bf16), w_out=w_out.astype(bf16),
        g_ln1=row(g_ln1), b_ln1=row(b_ln1),
        w_route=jnp.pad(jnp.concatenate([w_group, w_router], axis=1), ((0, 0), (0, pad_route))),
        b_route=jnp.pad(jnp.concatenate([b_group, b_router]), (0, pad_route)).reshape(1, LANES),
        w1=w1.astype(bf16), w3=w3.astype(bf16), w2=w2.astype(bf16),
        g_ln2=row(g_ln2), b_ln2=row(b_ln2),
    )


def _finish(x, o, u, ctx, w, bb, t, cb, ct, tm_moe):
    b, s, _ = x.shape
    h, gates = _tail(x, o, u, ctx, w, bb, t, cb, ct)
    y = _moe(h.reshape(b * s, D_MODEL), gates.reshape(b * s, LANES), w["w1"], w["w3"], w["w2"],
             w["g_ln2"], w["b_ln2"], tm_moe)
    return y.reshape(b, s, D_MODEL)


def _layer(xp, xs, cache_k, cache_v, cache_logf, cache_conv, w):
    bp, sp, _ = xp.shape
    bs, ts, _ = xs.shape
    past = cache_k.shape[1]
    tq = 512

    q, k, kb, v, vb, lf, u = _in_proj(xp.reshape(bp * sp, D_MODEL), w["w_pack"], w["b_forget"], 512)
    logf = lf[:, :N_HEADS].reshape(bp, sp, N_HEADS)
    parts = _cumsum_lanes(logf.transpose(0, 2, 1).reshape(bp * N_HEADS, sp), reverse=False, split_scale=-LOG2E)
    hp = ATT_WIDTH // LANES
    parts = parts.transpose(1, 0, 2).reshape(bp, hp, 2 * N_SPLIT, sp)
    ext = jnp.pad(parts.transpose(0, 1, 3, 2), ((0, 0), (0, 0), (0, 0), (0, LANES - 2 * N_SPLIT)))
    shp = (bp, sp, ATT_WIDTH)
    vt = vb.reshape(bp, sp // tq, tq, hp, LANES).transpose(0, 3, 1, 4, 2)
    o = _attn_prompt(q.reshape(shp), kb.reshape(shp), ext, vt, tq)
    u = u.reshape(bp, sp, D_CONV)
    yp = _finish(xp, o, u, jnp.zeros((bp, HALO, D_CONV), f32), w, 1, 512, 1, 32, 4 * MOE_SUB)
    outs_p = (k.reshape(bp, sp, N_HEADS, HEAD_DIM), v.reshape(bp, sp, N_HEADS, HEAD_DIM), logf,
              u[:, sp - CONV_CTX:])

    q, k, kb, v, vb, lf, u = _in_proj(xs.reshape(bs * ts, D_MODEL), w["w_pack"], w["b_forget"], bs * ts)
    logf = lf[:, :N_HEADS].reshape(bs, ts, N_HEADS)
    lf_t = jnp.pad(logf.transpose(0, 2, 1), ((0, 0), (0, 0), (0, LANES - ts)))
    cq = _cumsum_lanes(lf_t.reshape(bs * N_HEADS, LANES), reverse=False).reshape(bs, N_HEADS, LANES)[:, :, :ts]
    r = _cumsum_lanes(cache_logf.transpose(0, 2, 1).reshape(bs * N_HEADS, past), reverse=True)
    bias_c = jnp.repeat(LOG2E * r.reshape(bs, N_HEADS, past), ts, axis=1)
    causal = jnp.arange(ts)[None, :] <= jnp.arange(ts)[:, None]
    bias_n = jnp.where(causal[None, None], -LOG2E * cq[:, :, None, :], NEG)
    bias_n = jnp.pad(bias_n.reshape(bs, LANES, ts), ((0, 0), (0, 0), (0, LANES - ts)), constant_values=NEG)
    q4 = q.reshape(bs, ts, N_HEADS, HEAD_DIM).transpose(0, 2, 1, 3)
    qbd = (q4[:, :, :, None, :] * jnp.eye(N_HEADS, dtype=bf16)[None, :, None, :, None]).reshape(bs, LANES, ATT_WIDTH)
    new_t = lambda a: jnp.pad(a.reshape(bs, ts, ATT_WIDTH).transpose(0, 2, 1), ((0, 0), (0, 0), (0, LANES - ts)))
    cache_t = lambda a: a.transpose(0, 2, 3, 1).reshape(bs, ATT_WIDTH, past)
    o = _attn_sample(qbd, cache_t(cache_k), cache_t(cache_v), new_t(kb), new_t(vb), bias_c, bias_n, ts)
    u = u.reshape(bs, ts, D_CONV)
    ctx = jnp.pad(cache_conv, ((0, 0), (HALO_PAD, 0), (0, 0)))
    ys = _finish(xs, o, u, ctx, w, bs, ts, 2, ts, bs * ts)
    u_ext = jnp.concatenate([cache_conv, u], axis=1)
    outs_s = (k.reshape(bs, ts, N_HEADS, HEAD_DIM), v.reshape(bs, ts, N_HEADS, HEAD_DIM), logf,
              u_ext[:, u_ext.shape[1] - CONV_CTX:])
    return yp, ys, outs_p, outs_s


def kernel(x_prompt, x_sample, cache_k, cache_v, cache_logf, cache_conv, w_in, b_forget, w_dw, b_dw,
           g_conv_ln, b_conv_ln, w_attn_out, w_conv_out, w_out, g_ln1, b_ln1, w_group, b_group,
           w_router, b_router, w1, w3, w2, g_ln2, b_ln2):
    xp, xs = x_prompt, x_sample
    per_layer_p, per_layer_s = [], []
    for l in range(DEPTH):
        w = _pack_weights(w_in[l], b_forget[l], w_dw[l], b_dw[l], g_conv_ln[l], b_conv_ln[l], w_attn_out[l],
                          w_conv_out[l], w_out[l], g_ln1[l], b_ln1[l], w_group[l], b_group[l], w_router[l],
                          b_router[l], w1[l], w3[l], w2[l], g_ln2[l], b_ln2[l])
        xp, xs, outs_p, outs_s = _layer(xp, xs, cache_k[l], cache_v[l], cache_logf[l], cache_conv[l], w)
        per_layer_p.append(outs_p)
        per_layer_s.append(outs_s)
    stack = lambda outs, j: jnp.stack([o[j] for o in outs])
    return (xp, xs,
            stack(per_layer_p, 0), stack(per_layer_p, 1), stack(per_layer_p, 2), stack(per_layer_p, 3),
            stack(per_layer_s, 0), stack(per_layer_s, 1), stack(per_layer_s, 2), stack(per_layer_s, 3))
```

```python
import functools

import jax
import jax.numpy as jnp
from jax import lax
from jax.experimental import pallas as pl
from jax.experimental.pallas import tpu as pltpu

f32 = jnp.float32
bf16 = jnp.bfloat16

D_MODEL = 1024
N_HEADS = 8
HEAD_DIM = 64
ATT_WIDTH = N_HEADS * HEAD_DIM
D_CONV = 512
CONV_WIDTH = 31
CONV_CTX = CONV_WIDTH - 1
N_GROUPS = 4
EXPERTS_PER_GROUP = 8
N_EXPERTS = N_GROUPS * EXPERTS_PER_GROUP
D_EXPERT = 256
DEPTH = 1
ALPHA = (2 * DEPTH) ** 0.25
LN_EPS = 1e-5
NEG = -1e30
SCALE = HEAD_DIM ** -0.5
LOG2E = 1.4426950408889634

N_SPLIT = 3
LANES = 128
SUBLANES = 8
HALO = 32
HALO_PAD = HALO - CONV_CTX
EXPERT_LANE0 = N_GROUPS
VMEM_LIMIT = 56 * 1024 * 1024

_C_Q, _C_K, _C_V, _C_GA, _C_GB, _C_F, _C_END = 0, 512, 1024, 1536, 2048, 2560, 2688


def _sigmoid(x):
    return 1.0 / (1.0 + jnp.exp(-x))


def _layernorm(x, g, b):
    mu = jnp.mean(x, axis=-1, keepdims=True)
    xc = x - mu
    var = jnp.mean(xc * xc, axis=-1, keepdims=True)
    return xc * lax.rsqrt(var + LN_EPS) * g + b


def _params(*sem):
    return pltpu.CompilerParams(dimension_semantics=sem, vmem_limit_bytes=VMEM_LIMIT)


def _store_head_major(ref, val):
    tokens = val.shape[0]
    for h in range(N_HEADS):
        ref[pl.ds(h, tokens, stride=N_HEADS), :] = val[:, h * HEAD_DIM:(h + 1) * HEAD_DIM]


def _in_proj_kernel(x_ref, w_ref, bf_ref, q_ref, k_ref, kb_ref, v_ref, vb_ref, lf_ref, u_ref):
    xb = x_ref[...].astype(bf16)

    def proj(lo, hi):
        return jnp.dot(xb, w_ref[:, lo:hi], preferred_element_type=f32)

    q_ref[...] = (proj(_C_Q, _C_K) * (SCALE * LOG2E)).astype(bf16)
    k = proj(_C_K, _C_V)
    _store_head_major(k_ref, k)
    kb_ref[...] = k.astype(bf16)
    v = proj(_C_V, _C_GA)
    _store_head_major(v_ref, v)
    vb_ref[...] = v.astype(bf16)
    u_ref[...] = proj(_C_GA, _C_GB) * _sigmoid(proj(_C_GB, _C_F))
    z = proj(_C_F, _C_END) + bf_ref[...]
    lf_ref[...] = jnp.minimum(z, 0.0) - jnp.log(1.0 + jnp.exp(-jnp.abs(z)))


def _in_proj(x2d, w_pack, bf_pad, tm):
    n = x2d.shape[0]
    row = lambda width: pl.BlockSpec((tm, width), lambda i: (i, 0))
    heads = pl.BlockSpec((tm * N_HEADS, HEAD_DIM), lambda i: (i, 0))
    return pl.pallas_call(
        _in_proj_kernel,
        grid=(n // tm,),
        in_specs=[row(D_MODEL),
                  pl.BlockSpec((D_MODEL, _C_END), lambda i: (0, 0)),
                  pl.BlockSpec((1, LANES), lambda i: (0, 0))],
        out_specs=[row(ATT_WIDTH), heads, row(ATT_WIDTH), heads, row(ATT_WIDTH),
                   row(LANES), row(D_CONV)],
        out_shape=[jax.ShapeDtypeStruct((n, ATT_WIDTH), bf16),
                   jax.ShapeDtypeStruct((n * N_HEADS, HEAD_DIM), f32),
                   jax.ShapeDtypeStruct((n, ATT_WIDTH), bf16),
                   jax.ShapeDtypeStruct((n * N_HEADS, HEAD_DIM), f32),
                   jax.ShapeDtypeStruct((n, ATT_WIDTH), bf16),
                   jax.ShapeDtypeStruct((n, LANES), f32),
                   jax.ShapeDtypeStruct((n, D_CONV), f32)],
        compiler_params=_params("parallel"),
        name="in_proj",
    )(x2d, w_pack, bf_pad)


def _split2(x):
    hi = x.astype(bf16)
    return hi, (x - hi.astype(f32)).astype(bf16)


def _split3(x):
    hi = x.astype(bf16)
    r1 = x - hi.astype(f32)
    mid = r1.astype(bf16)
    lo = (r1 - mid.astype(f32)).astype(bf16)
    return hi, mid, lo


def _cumsum_kernel(x_ref, o_ref, *, reverse, split_scale):
    nc, rows, _ = x_ref.shape
    r = lax.broadcasted_iota(jnp.int32, (LANES, LANES), 0)
    c = lax.broadcasted_iota(jnp.int32, (LANES, LANES), 1)
    tri = jnp.where((r > c) if reverse else (r <= c), 1.0, 0.0).astype(bf16)
    carry = jnp.zeros((rows, 1), f32)
    for j in (range(nc - 1, -1, -1) if reverse else range(nc)):
        xj = x_ref[j]
        y = carry
        for part in _split3(xj):
            y = y + jnp.dot(part, tri, preferred_element_type=f32)
        if split_scale is None:
            o_ref[j] = y
        else:
            for n, part in enumerate(_split3(split_scale * y)):
                o_ref[n, j] = part
        carry = carry + jnp.sum(xj, axis=-1, keepdims=True)


def _cumsum_lanes(x, reverse, split_scale=None):
    rows, s = x.shape
    nc = s // LANES
    xc = x.reshape(rows, nc, LANES).transpose(1, 0, 2)
    shape = (nc, rows, LANES) if split_scale is None else (N_SPLIT, nc, rows, LANES)
    out = pl.pallas_call(
        functools.partial(_cumsum_kernel, reverse=reverse, split_scale=split_scale),
        out_shape=jax.ShapeDtypeStruct(shape, f32 if split_scale is None else bf16),
        name="cumsum_rev" if reverse else "cumsum_fwd",
    )(xc)
    if split_scale is None:
        return out.transpose(1, 0, 2).reshape(rows, s)
    return out.transpose(0, 2, 1, 3).reshape(N_SPLIT, rows, s)


def _attn_kernel(q_ref, k_ref, e_ref, vt_ref, o_ref, qx_sc, acc_sc, s0_sc, s1_sc, *, tq, tk):
    i = pl.program_id(2)
    per_q = tq // tk
    qt = q_ref[0].astype(f32).T
    row = lax.broadcasted_iota(jnp.int32, (LANES, tq), 0)
    for h in range(2):
        qx_sc[h, 0:LANES, :] = jnp.where(row // HEAD_DIM == h, qt, 0.0).astype(bf16)
        qx_sc[h, LANES:2 * LANES, :] = jnp.where(row // N_SPLIT == h, 1.0, 0.0).astype(bf16)
    acc_sc[...] = jnp.zeros(acc_sc.shape, f32)
    ones_rows = jnp.ones((16, tk), bf16)

    s_sc = (s0_sc, s1_sc)

    def scores(j, h):
        start = pl.multiple_of(j * tk, tk)
        kx = jnp.concatenate([k_ref[0, pl.ds(start, tk), :], e_ref[0, 0, pl.ds(start, tk), :]], axis=1)
        s_sc[h][...] = jnp.dot(kx, qx_sc[h], preferred_element_type=f32)

    def absorb(j, h, stats, mask_offset=None):
        m_prev, l_prev = stats
        s = s_sc[h][...]
        if mask_offset is not None:
            key = lax.broadcasted_iota(jnp.int32, (tk, tq), 0) + mask_offset
            qry = lax.broadcasted_iota(jnp.int32, (tk, tq), 1)
            s = jnp.where(key <= qry, s, NEG)
        m_new = jnp.maximum(m_prev, jnp.max(s, axis=0, keepdims=True))
        alpha = jnp.exp2(m_prev - m_new)
        p = jnp.exp2(s - m_new).astype(bf16)
        vx = jnp.concatenate([vt_ref[0, 0, j, h * HEAD_DIM:(h + 1) * HEAD_DIM, :], ones_rows], axis=0)
        pv = jnp.dot(vx, p, preferred_element_type=f32)
        acc_sc[h] = alpha * acc_sc[h] + pv[0:HEAD_DIM, :]
        return m_new, alpha * l_prev + pv[HEAD_DIM:HEAD_DIM + 1, :]

    def body(j, stats):
        scores(j, 1)
        st0 = absorb(j, 0, stats[0])
        scores(j + 1, 0)
        st1 = absorb(j, 1, stats[1])
        return st0, st1

    init = tuple((jnp.full((1, tq), NEG, f32), jnp.zeros((1, tq), f32)) for _ in range(2))
    scores(0, 0)
    stats = lax.fori_loop(0, (per_q // 2) * i, lambda p, st: body(2 * p + 1, body(2 * p, st)), init)
    st0, st1 = stats
    j0 = per_q * i
    for d in range(per_q):
        scores(j0 + d, 1)
        st0 = absorb(j0 + d, 0, st0, d * tk)
        if d + 1 < per_q:
            scores(j0 + d + 1, 0)
        st1 = absorb(j0 + d, 1, st1, d * tk)
    l0, l1 = st0[1], st1[1]
    ot = jnp.concatenate([acc_sc[0] / l0, acc_sc[1] / l1], axis=0)
    o_ref[0] = ot.T.astype(bf16)


def _attn_prompt(q, kb, ext, vt, tq):
    b, s, _ = q.shape
    hp = ATT_WIDTH // LANES
    nq = s // tq
    nk, tk = vt.shape[2], vt.shape[4]
    assert tq % (2 * tk) == 0
    return pl.pallas_call(
        functools.partial(_attn_kernel, tq=tq, tk=tk),
        grid=(b, hp, nq),
        in_specs=[pl.BlockSpec((1, tq, LANES), lambda b_, h_, i: (b_, i, h_)),
                  pl.BlockSpec((1, s, LANES), lambda b_, h_, i: (b_, 0, h_)),
                  pl.BlockSpec((1, 1, s, LANES), lambda b_, h_, i: (b_, h_, 0, 0)),
                  pl.BlockSpec((1, 1, nk, LANES, tk), lambda b_, h_, i: (b_, h_, 0, 0, 0))],
        out_specs=pl.BlockSpec((1, tq, LANES), lambda b_, h_, i: (b_, i, h_)),
        out_shape=jax.ShapeDtypeStruct((b, s, ATT_WIDTH), bf16),
        scratch_shapes=[pltpu.VMEM((2, 2 * LANES, tq), bf16), pltpu.VMEM((2, HEAD_DIM, tq), f32),
                        pltpu.VMEM((tk, tq), f32), pltpu.VMEM((tk, tq), f32)],
        compiler_params=_params("parallel", "parallel", "arbitrary"),
        name="attn_prompt",
    )(q, kb, ext, vt)


def _attn_sample_kernel(qbd_ref, ck_ref, cv_ref, kn_ref, vn_ref, bc_ref, bn_ref, o_ref, *, t_new):
    qbd = qbd_ref[0]
    s_c = jnp.dot(qbd, ck_ref[0].astype(bf16), preferred_element_type=f32) + bc_ref[0]
    s_n = jnp.dot(qbd, kn_ref[0], preferred_element_type=f32) + bn_ref[0]
    m = jnp.maximum(jnp.max(s_c, axis=-1, keepdims=True), jnp.max(s_n, axis=-1, keepdims=True))
    p_c = jnp.exp2(s_c - m)
    p_n = jnp.exp2(s_n - m)
    inv = 1.0 / (jnp.sum(p_c, axis=-1, keepdims=True) + jnp.sum(p_n, axis=-1, keepdims=True))
    p_c = (p_c * inv).astype(bf16)
    p_n = (p_n * inv).astype(bf16)
    over_positions = (((1,), (1,)), ((), ()))
    o_all = (lax.dot_general(cv_ref[0].astype(bf16), p_c, over_positions, preferred_element_type=f32)
             + lax.dot_general(vn_ref[0], p_n, over_positions, preferred_element_type=f32))
    shape = (ATT_WIDTH, LANES)
    row_head = lax.shift_right_logical(lax.broadcasted_iota(jnp.int32, shape, 0), HEAD_DIM.bit_length() - 1)
    col_head = lax.shift_right_logical(lax.broadcasted_iota(jnp.int32, shape, 1), t_new.bit_length() - 1)
    own = jnp.where(row_head == col_head, o_all, 0.0).astype(bf16)
    q_of_row = jnp.bitwise_and(lax.broadcasted_iota(jnp.int32, (LANES, LANES), 0), t_new - 1)
    pick = jnp.where(q_of_row == lax.broadcasted_iota(jnp.int32, (LANES, LANES), 1), 1.0, 0.0).astype(bf16)
    o_t = jnp.dot(own, pick, preferred_element_type=f32)
    o_ref[0] = o_t.T[0:t_new, :].astype(bf16)


def _attn_sample(qbd, ck, cv, kn, vn, bias_c, bias_n, t_new):
    b, _, p = ck.shape
    blk = lambda r, c: pl.BlockSpec((1, r, c), lambda i: (i, 0, 0))
    return pl.pallas_call(
        functools.partial(_attn_sample_kernel, t_new=t_new),
        grid=(b,),
        in_specs=[blk(LANES, ATT_WIDTH), blk(ATT_WIDTH, p), blk(ATT_WIDTH, p), blk(ATT_WIDTH, LANES),
                  blk(ATT_WIDTH, LANES), blk(LANES, p), blk(LANES, LANES)],
        out_specs=blk(t_new, ATT_WIDTH),
        out_shape=jax.ShapeDtypeStruct((b, t_new, ATT_WIDTH), bf16),
        compiler_params=_params("parallel"),
        name="attn_sample",
    )(qbd, ck, cv, kn, vn, bias_c, bias_n)


def _route(logits):
    lane = lax.broadcasted_iota(jnp.int32, logits.shape, 1).astype(f32)
    ninf = -jnp.inf
    gl = jnp.where(lane < N_GROUPS, logits, ninf)
    gmax = jnp.max(gl, axis=-1, keepdims=True)
    grp = jnp.min(jnp.where(gl == gmax, lane, float(LANES)), axis=-1, keepdims=True)
    p_sel = 1.0 / jnp.sum(jnp.exp(gl - gmax), axis=-1, keepdims=True)
    e_lo = EXPERT_LANE0 + grp * EXPERTS_PER_GROUP
    el = jnp.where(lane >= e_lo, jnp.where(lane < e_lo + EXPERTS_PER_GROUP, logits, ninf), ninf)
    t1 = jnp.max(el, axis=-1, keepdims=True)
    i1 = jnp.min(jnp.where(el == t1, lane, float(LANES)), axis=-1, keepdims=True)
    el2 = jnp.where(lane == i1, ninf, el)
    t2 = jnp.max(el2, axis=-1, keepdims=True)
    i2 = jnp.min(jnp.where(el2 == t2, lane, float(LANES)), axis=-1, keepdims=True)
    e2 = jnp.exp(t2 - t1)
    w1 = p_sel / (1.0 + e2)
    w2 = p_sel * e2 / (1.0 + e2)
    picks = (i1 - EXPERT_LANE0, i2 - EXPERT_LANE0, w1, w2)
    out = jnp.zeros(logits.shape, f32)
    for n, val in enumerate(picks):
        out = jnp.where(lane == float(n), val, out)
    return out


def _tail_kernel(x_ref, o_ref, u_ref, uprev_ref, ctx_ref, wdw_ref, bdw_ref, gcl_ref, bcl_ref,
                 wa_ref, wc_ref, wg_ref, wo_ref, g1_ref, b1_ref, wr_ref, br_ref,
                 h_ref, gates_ref, uext_sc, shift_sc, c_sc, *, cb, ct):
    i = pl.program_id(1)
    bb, t, _ = x_ref.shape
    n = bb * t

    x = x_ref[...].reshape(n, D_MODEL)
    g = _sigmoid(jnp.dot(x.astype(bf16), wg_ref[...], preferred_element_type=f32))
    branch_a = jnp.dot(o_ref[...].reshape(n, ATT_WIDTH), wa_ref[...], preferred_element_type=f32)

    uext_sc[:, 0:HALO, :] = jnp.where(i == 0, ctx_ref[...], uprev_ref[...])
    uext_sc[:, HALO:HALO + t, :] = u_ref[...]
    span = t + HALO - SUBLANES
    for r in range(1, SUBLANES):
        shift_sc[r - 1] = uext_sc[:, r:r + span, :]
    for b0 in range(0, bb, cb):
        for t0 in range(0, t, ct):
            acc = jnp.zeros((cb, ct, D_CONV), f32) + bdw_ref[...]
            for k in range(CONV_WIDTH):
                a, r = divmod(HALO_PAD + k, SUBLANES)
                r0 = t0 + a * SUBLANES
                src = uext_sc[b0:b0 + cb, r0:r0 + ct, :] if r == 0 else shift_sc[r - 1, b0:b0 + cb, r0:r0 + ct, :]
                acc = acc + wdw_ref[k:k + 1, :] * src
            c_sc[b0:b0 + cb, t0:t0 + ct, :] = acc

    c = _layernorm(c_sc[...].reshape(n, D_CONV), gcl_ref[...], bcl_ref[...])
    c = c * _sigmoid(c)
    branch_b = jnp.dot(c.astype(bf16), wc_ref[...], preferred_element_type=f32)
    merged = g[:, :D_MODEL] * branch_a + g[:, D_MODEL:] * branch_b
    mix = jnp.dot(merged.astype(bf16), wo_ref[...], preferred_element_type=f32)
    h = _layernorm(ALPHA * x + mix, g1_ref[...], b1_ref[...])
    h_ref[...] = h.reshape(bb, t, D_MODEL)
    w_hi, w_lo = _split2(wr_ref[...])
    w_parts = jnp.concatenate([w_hi, w_lo], axis=1)
    cross = sum(jnp.dot(part, w_parts, preferred_element_type=f32) for part in _split2(h))
    logits = cross[:, :LANES] + cross[:, LANES:] + br_ref[...]
    gates_ref[...] = _route(logits).reshape(bb, t, LANES)


def _tail(x, o, u, ctx, w, bb, t, cb, ct):
    b, s, _ = x.shape
    nt = s // t
    uprev = u if nt > 1 else ctx
    per_tile = t // HALO
    tile = lambda width: pl.BlockSpec((bb, t, width), lambda b_, i: (b_, i, 0))
    full = lambda a: pl.BlockSpec(a.shape, lambda b_, i: (0,) * a.ndim)
    weights = [w["w_dw"], w["b_dw"], w["g_cln"], w["b_cln"], w["w_attn_out"], w["w_conv_out"], w["w_gate"],
               w["w_out"], w["g_ln1"], w["b_ln1"], w["w_route"], w["b_route"]]
    return pl.pallas_call(
        functools.partial(_tail_kernel, cb=cb, ct=ct),
        grid=(b // bb, nt),
        in_specs=[tile(D_MODEL), tile(ATT_WIDTH), tile(D_CONV),
                  pl.BlockSpec((bb, HALO, D_CONV), lambda b_, i: (b_, jnp.maximum(i * per_tile - 1, 0), 0)),
                  pl.BlockSpec((bb, HALO, D_CONV), lambda b_, i: (b_, 0, 0))]
                 + [full(a) for a in weights],
        out_specs=[tile(D_MODEL), tile(LANES)],
        out_shape=[jax.ShapeDtypeStruct((b, s, D_MODEL), f32), jax.ShapeDtypeStruct((b, s, LANES), f32)],
        scratch_shapes=[pltpu.VMEM((bb, HALO + t, D_CONV), f32),
                        pltpu.VMEM((SUBLANES - 1, bb, t + HALO - SUBLANES, D_CONV), f32),
                        pltpu.VMEM((bb, t, D_CONV), f32)],
        compiler_params=_params("parallel", "arbitrary"),
        name="tail",
    )(x, o, u, uprev, ctx, *weights)


MOE_SUB = 256
MOE_CAP = 32
MOE_EPC = 8
MOE_KEY = 4096
_CAP_SHIFT = MOE_CAP.bit_length() - 1


def _moe_kernel(h_ref, r_ref, w1_ref, w3_ref, w2_ref, g2_ref, b2_ref, y_ref,
                acc_sc, hb_sc, key_sc, keyt_sc, nr_sc, *, n_sub):
    c = pl.program_id(1)
    slots = MOE_EPC * MOE_CAP

    @pl.when(c == 0)
    def _():
        acc_sc[...] = jnp.zeros(acc_sc.shape, f32)
        hb_sc[...] = h_ref[...].astype(bf16)
        lane = lax.broadcasted_iota(jnp.int32, (MOE_SUB, LANES), 1).astype(f32)
        ti = lax.broadcasted_iota(jnp.int32, (MOE_SUB, MOE_SUB), 0)
        tj = lax.broadcasted_iota(jnp.int32, (MOE_SUB, MOE_SUB), 1)
        earlier = jnp.where(tj < ti, 1.0, 0.0).astype(bf16)
        most = jnp.zeros((1, LANES), f32)
        for a in range(n_sub):
            r = r_ref[a * MOE_SUB:(a + 1) * MOE_SUB, :]
            e1, e2 = r[:, 0:1], r[:, 1:2]
            oh1 = jnp.where(lane == e1, 1.0, 0.0)
            oh2 = jnp.where(lane == e2, 1.0, 0.0)
            both = oh1 + oh2
            before = jnp.dot(earlier, both.astype(bf16), preferred_element_type=f32)
            key1 = e1 * MOE_KEY + jnp.sum(before * oh1, axis=-1, keepdims=True)
            key2 = e2 * MOE_KEY + jnp.sum(before * oh2, axis=-1, keepdims=True)
            info = jnp.where(lane == 0.0, key1, jnp.where(lane == 1.0, key2, r))
            key_sc[a] = info
            keyt_sc[a] = info.T[0:8, :]
            most = jnp.maximum(most, jnp.sum(both, axis=0, keepdims=True))
        nr_sc[0] = (jnp.max(most).astype(jnp.int32) + (MOE_CAP - 1)) // MOE_CAP

    e0 = c * MOE_EPC
    s_row = lax.broadcasted_iota(jnp.int32, (slots, MOE_SUB), 0)
    s_col = lax.broadcasted_iota(jnp.int32, (MOE_SUB, slots), 1)

    def slot_key(s, rd):
        expert = lax.shift_right_logical(s, _CAP_SHIFT) + e0
        return (expert * MOE_KEY + jnp.bitwise_and(s, MOE_CAP - 1) + rd * MOE_CAP).astype(f32)

    def one_round(rd, carry):
        key_r = slot_key(s_row, rd)
        key_c = slot_key(s_col, rd)
        xs = []
        for a in range(n_sub):
            k1, k2 = keyt_sc[a, 0:1, :], keyt_sc[a, 1:2, :]
            p = jnp.where(key_r == k1, 1.0, jnp.where(key_r == k2, 1.0, 0.0)).astype(bf16)
            hb = hb_sc[a * MOE_SUB:(a + 1) * MOE_SUB, :]
            xs.append(jnp.dot(p, hb, preferred_element_type=f32).astype(bf16))
        ys = [[] for _ in range(n_sub)]
        for e in range(MOE_EPC):
            xe = jnp.concatenate([x[e * MOE_CAP:(e + 1) * MOE_CAP] for x in xs], axis=0)
            a1 = jnp.dot(xe, w1_ref[e], preferred_element_type=f32)
            a3 = jnp.dot(xe, w3_ref[e], preferred_element_type=f32)
            act = (a1 * _sigmoid(a1)) * a3
            ye = jnp.dot(act.astype(bf16), w2_ref[e], preferred_element_type=f32).astype(bf16)
            for a in range(n_sub):
                ys[a].append(ye[a * MOE_CAP:(a + 1) * MOE_CAP])
        for a in range(n_sub):
            info = key_sc[a]
            pw = jnp.where(key_c == info[:, 0:1], info[:, 2:3],
                           jnp.where(key_c == info[:, 1:2], info[:, 3:4], 0.0)).astype(bf16)
            rows = slice(a * MOE_SUB, (a + 1) * MOE_SUB)
            acc_sc[rows, :] += jnp.dot(pw, jnp.concatenate(ys[a], axis=0), preferred_element_type=f32)
        return carry

    lax.fori_loop(0, nr_sc[0], one_round, 0)

    @pl.when(c == pl.num_programs(1) - 1)
    def _():
        y_ref[...] = _layernorm(ALPHA * h_ref[...] + acc_sc[...], g2_ref[...], b2_ref[...])


def _moe(h2d, route, w1, w3, w2, g2, b2, tm):
    n = h2d.shape[0]
    n_sub = tm // MOE_SUB
    expert_block = lambda a: pl.BlockSpec((MOE_EPC,) + a.shape[1:], lambda i, c: (c, 0, 0))
    return pl.pallas_call(
        functools.partial(_moe_kernel, n_sub=n_sub),
        grid=(n // tm, N_EXPERTS // MOE_EPC),
        in_specs=[pl.BlockSpec((tm, D_MODEL), lambda i, c: (i, 0)),
                  pl.BlockSpec((tm, LANES), lambda i, c: (i, 0)),
                  expert_block(w1), expert_block(w3), expert_block(w2),
                  pl.BlockSpec((1, D_MODEL), lambda i, c: (0, 0)),
                  pl.BlockSpec((1, D_MODEL), lambda i, c: (0, 0))],
        out_specs=pl.BlockSpec((tm, D_MODEL), lambda i, c: (i, 0)),
        out_shape=jax.ShapeDtypeStruct((n, D_MODEL), f32),
        scratch_shapes=[pltpu.VMEM((tm, D_MODEL), f32), pltpu.VMEM((tm, D_MODEL), bf16),
                        pltpu.VMEM((n_sub, MOE_SUB, LANES), f32), pltpu.VMEM((n_sub, 8, MOE_SUB), f32),
                        pltpu.SMEM((1,), jnp.int32)],
        compiler_params=_params("parallel", "arbitrary"),
        name="moe",
    )(h2d, route, w1, w3, w2, g2, b2)


def _pack_weights(w_in, b_forget, w_dw, b_dw, g_conv_ln, b_conv_ln, w_attn_out, w_conv_out, w_out,
                  g_ln1, b_ln1, w_group, b_group, w_router, b_router, w1, w3, w2, g_ln2, b_ln2):
    a = ATT_WIDTH
    c_f, c_glu, c_gate = 3 * a, 3 * a + N_HEADS, 3 * a + N_HEADS + 2 * D_CONV
    w_f = jnp.pad(w_in[:, c_f:c_glu], ((0, 0), (0, LANES - N_HEADS)))
    w_pack = jnp.concatenate([w_in[:, :c_f], w_in[:, c_glu:c_gate], w_f], axis=1).astype(bf16)
    pad_route = LANES - N_GROUPS - N_EXPERTS
    row = lambda v: v.reshape(1, -1)
    return dict(
        w_pack=w_pack,
        b_forget=jnp.pad(b_forget, (0, LANES - N_HEADS)).reshape(1, LANES),
        w_gate=w_in[:, c_gate:].astype(bf16),
        w_dw=jnp.pad(w_dw, ((0, HALO - CONV_WIDTH), (0, 0))), b_dw=row(b_dw),
        g_cln=row(g_conv_ln), b_cln=row(b_conv_ln),
        w_attn_out=w_attn_out.astype(bf16), w_conv_out=w_conv_out.astype(bf16), w_out=w_out.astype(bf16),
        g_ln1=row(g_ln1), b_ln1=row(b_ln1),
        w_route=jnp.pad(jnp.concatenate([w_group, w_router], axis=1), ((0, 0), (0, pad_route))),
        b_route=jnp.pad(jnp.concatenate([b_group, b_router]), (0, pad_route)).reshape(1, LANES),
        w1=w1.astype(bf16), w3=w3.astype(bf16), w2=w2.astype(bf16),
        g_ln2=row(g_ln2), b_ln2=row(b_ln2),
    )


def _finish(x, o, u, ctx, w, bb, t, cb, ct, tm_moe):
    b, s, _ = x.shape
    h, gates = _tail(x, o, u, ctx, w, bb, t, cb, ct)
    y = _moe(h.reshape(b * s, D_MODEL), gates.reshape(b * s, LANES), w["w1"], w["w3"], w["w2"],
             w["g_ln2"], w["b_ln2"], tm_moe)
    return y.reshape(b, s, D_MODEL)


def _layer(xp, xs, cache_k, cache_v, cache_logf, cache_conv, w):
    bp, sp, _ = xp.shape
    bs, ts, _ = xs.shape
    past = cache_k.shape[1]
    tq, tk = 1024, 512

    q, k, kb, v, vb, lf, u = _in_proj(xp.reshape(bp * sp, D_MODEL), w["w_pack"], w["b_forget"], 512)
    logf = lf[:, :N_HEADS].reshape(bp, sp, N_HEADS)
    parts = _cumsum_lanes(logf.transpose(0, 2, 1).reshape(bp * N_HEADS, sp), reverse=False, split_scale=-LOG2E)
    hp = ATT_WIDTH // LANES
    parts = parts.transpose(1, 0, 2).reshape(bp, hp, 2 * N_SPLIT, sp)
    ext = jnp.pad(parts.transpose(0, 1, 3, 2), ((0, 0), (0, 0), (0, 0), (0, LANES - 2 * N_SPLIT)))
    shp = (bp, sp, ATT_WIDTH)
    vt = vb.reshape(bp, sp // tk, tk, hp, LANES).transpose(0, 3, 1, 4, 2)
    o = _attn_prompt(q.reshape(shp), kb.reshape(shp), ext, vt, tq)
    u = u.reshape(bp, sp, D_CONV)
    yp = _finish(xp, o, u, jnp.zeros((bp, HALO, D_CONV), f32), w, 1, 512, 1, 32, 4 * MOE_SUB)
    outs_p = (k.reshape(bp, sp, N_HEADS, HEAD_DIM), v.reshape(bp, sp, N_HEADS, HEAD_DIM), logf,
              u[:, sp - CONV_CTX:])

    q, k, kb, v, vb, lf, u = _in_proj(xs.reshape(bs * ts, D_MODEL), w["w_pack"], w["b_forget"], bs * ts)
    logf = lf[:, :N_HEADS].reshape(bs, ts, N_HEADS)
    lf_t = jnp.pad(logf.transpose(0, 2, 1), ((0, 0), (0, 0), (0, LANES - ts)))
    cq = _cumsum_lanes(lf_t.reshape(bs * N_HEADS, LANES), reverse=False).reshape(bs, N_HEADS, LANES)[:, :, :ts]
    r = _cumsum_lanes(cache_logf.transpose(0, 2, 1).reshape(bs * N_HEADS, past), reverse=True)
    bias_c = jnp.repeat(LOG2E * r.reshape(bs, N_HEADS, past), ts, axis=1)
    causal = jnp.arange(ts)[None, :] <= jnp.arange(ts)[:, None]
    bias_n = jnp.where(causal[None, None], -LOG2E * cq[:, :, None, :], NEG)
    bias_n = jnp.pad(bias_n.reshape(bs, LANES, ts), ((0, 0), (0, 0), (0, LANES - ts)), constant_values=NEG)
    q4 = q.reshape(bs, ts, N_HEADS, HEAD_DIM).transpose(0, 2, 1, 3)
    qbd = (q4[:, :, :, None, :] * jnp.eye(N_HEADS, dtype=bf16)[None, :, None, :, None]).reshape(bs, LANES, ATT_WIDTH)
    new_t = lambda a: jnp.pad(a.reshape(bs, ts, ATT_WIDTH).transpose(0, 2, 1), ((0, 0), (0, 0), (0, LANES - ts)))
    cache_t = lambda a: a.transpose(0, 2, 3, 1).reshape(bs, ATT_WIDTH, past)
    o = _attn_sample(qbd, cache_t(cache_k), cache_t(cache_v), new_t(kb), new_t(vb), bias_c, bias_n, ts)
    u = u.reshape(bs, ts, D_CONV)
    ctx = jnp.pad(cache_conv, ((0, 0), (HALO_PAD, 0), (0, 0)))
    ys = _finish(xs, o, u, ctx, w, bs, ts, 2, ts, bs * ts)
    u_ext = jnp.concatenate([cache_conv, u], axis=1)
    outs_s = (k.reshape(bs, ts, N_HEADS, HEAD_DIM), v.reshape(bs, ts, N_HEADS, HEAD_DIM), logf,
              u_ext[:, u_ext.shape[1] - CONV_CTX:])
    return yp, ys, outs_p, outs_s


def kernel(x_prompt, x_sample, cache_k, cache_v, cache_logf, cache_conv, w_in, b_forget, w_dw, b_dw,
           g_conv_ln, b_conv_ln, w_attn_out, w_conv_out, w_out, g_ln1, b_ln1, w_group, b_group,
           w_router, b_router, w1, w3, w2, g_ln2, b_ln2):
    xp, xs = x_prompt, x_sample
    per_layer_p, per_layer_s = [], []
    for l in range(DEPTH):
        w = _pack_weights(w_in[l], b_forget[l], w_dw[l], b_dw[l], g_conv_ln[l], b_conv_ln[l], w_attn_out[l],
                          w_conv_out[l], w_out[l], g_ln1[l], b_ln1[l], w_group[l], b_group[l], w_router[l],
                          b_router[l], w1[l], w3[l], w2[l], g_ln2[l], b_ln2[l])
        xp, xs, outs_p, outs_s = _layer(xp, xs, cache_k[l], cache_v[l], cache_logf[l], cache_conv[l], w)
        per_layer_p.append(outs_p)
        per_layer_s.append(outs_s)
    stack = lambda outs, j: jnp.stack([o[j] for o in outs])
    return (xp, xs,
            stack(per_layer_p, 0), stack(per_layer_p, 1), stack(per_layer_p, 2), stack(per_layer_p, 3),
            stack(per_layer_s, 0), stack(per_layer_s, 1), stack(per_layer_s, 2), stack(per_layer_s, 3))
```

```python
import functools

import jax
import jax.numpy as jnp
from jax import lax
from jax.experimental import pallas as pl
from jax.experimental.pallas import tpu as pltpu

f32 = jnp.float32
bf16 = jnp.bfloat16

D_MODEL = 1024
N_HEADS = 8
HEAD_DIM = 64
ATT_WIDTH = N_HEADS * HEAD_DIM
D_CONV = 512
CONV_WIDTH = 31
CONV_CTX = CONV_WIDTH - 1
N_GROUPS = 4
EXPERTS_PER_GROUP = 8
N_EXPERTS = N_GROUPS * EXPERTS_PER_GROUP
D_EXPERT = 256
DEPTH = 1
ALPHA = (2 * DEPTH) ** 0.25
LN_EPS = 1e-5
NEG = -1e30
SCALE = HEAD_DIM ** -0.5
LOG2E = 1.4426950408889634

N_SPLIT = 3
LANES = 128
SUBLANES = 8
HALO = 32
HALO_PAD = HALO - CONV_CTX
EXPERT_LANE0 = N_GROUPS
VMEM_LIMIT = 56 * 1024 * 1024

TOKEN_TILE = 512
ATTN_Q_TILE = 1024
CONV_ROWS = 32
MOE_TILE = 1024

_C_Q, _C_K, _C_V, _C_GA, _C_GB, _C_F, _C_END = 0, 512, 1024, 1536, 2048, 2560, 2688


def _sigmoid(x):
    return 1.0 / (1.0 + jnp.exp(-x))


def _layernorm(x, g, b):
    mu = jnp.mean(x, axis=-1, keepdims=True)
    xc = x - mu
    var = jnp.mean(xc * xc, axis=-1, keepdims=True)
    return xc * lax.rsqrt(var + LN_EPS) * g + b


def _params(*sem):
    return pltpu.CompilerParams(dimension_semantics=sem, vmem_limit_bytes=VMEM_LIMIT)


def _store_head_major(ref, val):
    tokens = val.shape[0]
    for h in range(N_HEADS):
        ref[pl.ds(h, tokens, stride=N_HEADS), :] = val[:, h * HEAD_DIM:(h + 1) * HEAD_DIM]


def _in_proj_kernel(x_ref, w_ref, bf_ref, q_ref, k_ref, kb_ref, v_ref, vt_ref, vb_ref, lf_ref, u_ref):
    tm = x_ref.shape[0]
    xb = x_ref[...].astype(bf16)

    def proj(lo, hi):
        return jnp.dot(xb, w_ref[:, lo:hi], preferred_element_type=f32)

    q_ref[...] = (proj(_C_Q, _C_K) * (SCALE * LOG2E)).astype(bf16)
    k = proj(_C_K, _C_V)
    _store_head_major(k_ref, k)
    kb_ref[...] = k.astype(bf16)
    v = proj(_C_V, _C_GA)
    _store_head_major(v_ref, v)
    vb_ref[...] = v.astype(bf16)
    vt_ref[0, :, 0] = v.T.astype(bf16).reshape(ATT_WIDTH // LANES, LANES, tm)
    u_ref[...] = proj(_C_GA, _C_GB) * _sigmoid(proj(_C_GB, _C_F))
    z = proj(_C_F, _C_END) + bf_ref[...]
    log_f = jnp.minimum(z, 0.0) - jnp.log(1.0 + jnp.exp(-jnp.abs(z)))
    lf_ref[0] = log_f.T[0:N_HEADS, :]


def _in_proj(x2d, w_pack, bf_pad, tm, groups):
    n = x2d.shape[0]
    cols = n // groups
    tpg = cols // tm
    row = lambda width: pl.BlockSpec((tm, width), lambda i: (i, 0))
    heads = pl.BlockSpec((tm * N_HEADS, HEAD_DIM), lambda i: (i, 0))
    hp = ATT_WIDTH // LANES
    return pl.pallas_call(
        _in_proj_kernel,
        grid=(n // tm,),
        in_specs=[row(D_MODEL),
                  pl.BlockSpec((D_MODEL, _C_END), lambda i: (0, 0)),
                  pl.BlockSpec((1, LANES), lambda i: (0, 0))],
        out_specs=[row(ATT_WIDTH), heads, row(ATT_WIDTH), heads,
                   pl.BlockSpec((1, hp, 1, LANES, tm), lambda i: (i // tpg, 0, i % tpg, 0, 0)),
                   row(ATT_WIDTH),
                   pl.BlockSpec((1, N_HEADS, tm), lambda i: (i // tpg, 0, i % tpg)),
                   row(D_CONV)],
        out_shape=[jax.ShapeDtypeStruct((n, ATT_WIDTH), bf16),
                   jax.ShapeDtypeStruct((n * N_HEADS, HEAD_DIM), f32),
                   jax.ShapeDtypeStruct((n, ATT_WIDTH), bf16),
                   jax.ShapeDtypeStruct((n * N_HEADS, HEAD_DIM), f32),
                   jax.ShapeDtypeStruct((groups, hp, tpg, LANES, tm), bf16),
                   jax.ShapeDtypeStruct((n, ATT_WIDTH), bf16),
                   jax.ShapeDtypeStruct((groups, N_HEADS, cols), f32),
                   jax.ShapeDtypeStruct((n, D_CONV), f32)],
        compiler_params=_params("parallel"),
        name="in_proj",
    )(x2d, w_pack, bf_pad)


def _split2(x):
    hi = x.astype(bf16)
    return hi, (x - hi.astype(f32)).astype(bf16)


def _split3(x):
    hi = x.astype(bf16)
    r1 = x - hi.astype(f32)
    mid = r1.astype(bf16)
    lo = (r1 - mid.astype(f32)).astype(bf16)
    return hi, mid, lo


def _cumsum_kernel(x_ref, o_ref, *, reverse, split_scale):
    nc, rows, _ = x_ref.shape
    r = lax.broadcasted_iota(jnp.int32, (LANES, LANES), 0)
    c = lax.broadcasted_iota(jnp.int32, (LANES, LANES), 1)
    tri = jnp.where((r > c) if reverse else (r <= c), 1.0, 0.0).astype(bf16)
    carry = jnp.zeros((rows, 1), f32)
    for j in (range(nc - 1, -1, -1) if reverse else range(nc)):
        xj = x_ref[j]
        y = carry
        for part in _split3(xj):
            y = y + jnp.dot(part, tri, preferred_element_type=f32)
        if split_scale is None:
            o_ref[j] = y
        else:
            for n, part in enumerate(_split3(split_scale * y)):
                o_ref[n, j] = part.astype(f32)
        carry = carry + jnp.sum(xj, axis=-1, keepdims=True)


def _cumsum_lanes(x, reverse, split_scale=None):
    rows, s = x.shape
    nc = s // LANES
    xc = x.reshape(rows, nc, LANES).transpose(1, 0, 2)
    shape = (nc, rows, LANES) if split_scale is None else (N_SPLIT, nc, rows, LANES)
    out = pl.pallas_call(
        functools.partial(_cumsum_kernel, reverse=reverse, split_scale=split_scale),
        out_shape=jax.ShapeDtypeStruct(shape, f32),
        name="cumsum_rev" if reverse else "cumsum_fwd",
    )(xc)
    if split_scale is None:
        return out.transpose(1, 0, 2).reshape(rows, s)
    return out.transpose(0, 2, 1, 3).reshape(N_SPLIT, rows, s)


def _attn_kernel(q_ref, k_ref, p_ref, vt_ref, o_ref, qx_sc, acc_sc, s0_sc, s1_sc, e_sc, *, tq, tk):
    i = pl.program_id(2)
    per_q = tq // tk

    @pl.when(i == 0)
    def _():
        row_k = lax.broadcasted_iota(jnp.int32, (LANES, tk), 0)
        for c in range(e_sc.shape[0] // tk):
            chans = jnp.zeros((LANES, tk), f32)
            for hh in range(2):
                for n in range(N_SPLIT):
                    part = p_ref[n, 0, 0, hh:hh + 1, c * tk:(c + 1) * tk]
                    chans = jnp.where(row_k == N_SPLIT * hh + n, part, chans)
            e_sc[c * tk:(c + 1) * tk, :] = chans.T.astype(bf16)

    qt = q_ref[0].astype(f32).T
    row = lax.broadcasted_iota(jnp.int32, (LANES, tq), 0)
    for h in range(2):
        qx_sc[h, 0:LANES, :] = jnp.where(row // HEAD_DIM == h, qt, 0.0).astype(bf16)
        qx_sc[h, LANES:2 * LANES, :] = jnp.where(row // N_SPLIT == h, 1.0, 0.0).astype(bf16)
    acc_sc[...] = jnp.zeros(acc_sc.shape, f32)
    ones_rows = jnp.ones((16, tk), bf16)

    s_sc = (s0_sc, s1_sc)

    def scores(j, h):
        start = pl.multiple_of(j * tk, tk)
        kx = jnp.concatenate([k_ref[0, pl.ds(start, tk), :], e_sc[pl.ds(start, tk), :]], axis=1)
        s_sc[h][...] = jnp.dot(kx, qx_sc[h], preferred_element_type=f32)

    def absorb(j, h, stats, mask_offset=None):
        m_prev, l_prev = stats
        s = s_sc[h][...]
        if mask_offset is not None:
            key = lax.broadcasted_iota(jnp.int32, (tk, tq), 0) + mask_offset
            qry = lax.broadcasted_iota(jnp.int32, (tk, tq), 1)
            s = jnp.where(key <= qry, s, NEG)
        m_new = jnp.maximum(m_prev, jnp.max(s, axis=0, keepdims=True))
        alpha = jnp.exp2(m_prev - m_new)
        p = jnp.exp2(s - m_new).astype(bf16)
        vx = jnp.concatenate([vt_ref[0, 0, j, h * HEAD_DIM:(h + 1) * HEAD_DIM, :], ones_rows], axis=0)
        pv = jnp.dot(vx, p, preferred_element_type=f32)
        acc_sc[h] = alpha * acc_sc[h] + pv[0:HEAD_DIM, :]
        return m_new, alpha * l_prev + pv[HEAD_DIM:HEAD_DIM + 1, :]

    def body(j, stats):
        scores(j, 1)
        st0 = absorb(j, 0, stats[0])
        scores(j + 1, 0)
        st1 = absorb(j, 1, stats[1])
        return st0, st1

    init = tuple((jnp.full((1, tq), NEG, f32), jnp.zeros((1, tq), f32)) for _ in range(2))
    scores(0, 0)
    stats = lax.fori_loop(0, (per_q // 2) * i, lambda p, st: body(2 * p + 1, body(2 * p, st)), init)
    st0, st1 = stats
    j0 = per_q * i
    for d in range(per_q):
        scores(j0 + d, 1)
        st0 = absorb(j0 + d, 0, st0, d * tk)
        if d + 1 < per_q:
            scores(j0 + d + 1, 0)
        st1 = absorb(j0 + d, 1, st1, d * tk)
    l0, l1 = st0[1], st1[1]
    ot = jnp.concatenate([acc_sc[0] / l0, acc_sc[1] / l1], axis=0)
    o_ref[0] = ot.T.astype(bf16)


def _attn_prompt(q, kb, parts, vt, tq):
    b, s, _ = q.shape
    hp = ATT_WIDTH // LANES
    nq = s // tq
    nk, tk = vt.shape[2], vt.shape[4]
    assert tq % (2 * tk) == 0
    return pl.pallas_call(
        functools.partial(_attn_kernel, tq=tq, tk=tk),
        grid=(b, hp, nq),
        in_specs=[pl.BlockSpec((1, tq, LANES), lambda b_, h_, i: (b_, i, h_)),
                  pl.BlockSpec((1, s, LANES), lambda b_, h_, i: (b_, 0, h_)),
                  pl.BlockSpec((N_SPLIT, 1, 1, 2, s), lambda b_, h_, i: (0, b_, h_, 0, 0)),
                  pl.BlockSpec((1, 1, nk, LANES, tk), lambda b_, h_, i: (b_, h_, 0, 0, 0))],
        out_specs=pl.BlockSpec((1, tq, LANES), lambda b_, h_, i: (b_, i, h_)),
        out_shape=jax.ShapeDtypeStruct((b, s, ATT_WIDTH), bf16),
        scratch_shapes=[pltpu.VMEM((2, 2 * LANES, tq), bf16), pltpu.VMEM((2, HEAD_DIM, tq), f32),
                        pltpu.VMEM((tk, tq), f32), pltpu.VMEM((tk, tq), f32), pltpu.VMEM((s, LANES), bf16)],
        compiler_params=_params("parallel", "parallel", "arbitrary"),
        name="attn_prompt",
    )(q, kb, parts, vt)


def _attn_sample_kernel(qbd_ref, ck_ref, cv_ref, kn_ref, vn_ref, bc_ref, bn_ref, o_ref, *, t_new):
    qbd = qbd_ref[0]
    s_c = jnp.dot(qbd, ck_ref[0].astype(bf16), preferred_element_type=f32) + bc_ref[0]
    s_n = jnp.dot(qbd, kn_ref[0], preferred_element_type=f32) + bn_ref[0]
    m = jnp.maximum(jnp.max(s_c, axis=-1, keepdims=True), jnp.max(s_n, axis=-1, keepdims=True))
    p_c = jnp.exp2(s_c - m)
    p_n = jnp.exp2(s_n - m)
    inv = 1.0 / (jnp.sum(p_c, axis=-1, keepdims=True) + jnp.sum(p_n, axis=-1, keepdims=True))
    p_c = (p_c * inv).astype(bf16)
    p_n = (p_n * inv).astype(bf16)
    over_positions = (((1,), (1,)), ((), ()))
    o_all = (lax.dot_general(cv_ref[0].astype(bf16), p_c, over_positions, preferred_element_type=f32)
             + lax.dot_general(vn_ref[0], p_n, over_positions, preferred_element_type=f32))
    shape = (ATT_WIDTH, LANES)
    row_head = lax.shift_right_logical(lax.broadcasted_iota(jnp.int32, shape, 0), HEAD_DIM.bit_length() - 1)
    col_head = lax.shift_right_logical(lax.broadcasted_iota(jnp.int32, shape, 1), t_new.bit_length() - 1)
    own = jnp.where(row_head == col_head, o_all, 0.0).astype(bf16)
    q_of_row = jnp.bitwise_and(lax.broadcasted_iota(jnp.int32, (LANES, LANES), 0), t_new - 1)
    pick = jnp.where(q_of_row == lax.broadcasted_iota(jnp.int32, (LANES, LANES), 1), 1.0, 0.0).astype(bf16)
    o_t = jnp.dot(own, pick, preferred_element_type=f32)
    o_ref[0] = o_t.T[0:t_new, :].astype(bf16)


def _attn_sample(qbd, ck, cv, kn, vn, bias_c, bias_n, t_new):
    b, _, p = ck.shape
    blk = lambda r, c: pl.BlockSpec((1, r, c), lambda i: (i, 0, 0))
    return pl.pallas_call(
        functools.partial(_attn_sample_kernel, t_new=t_new),
        grid=(b,),
        in_specs=[blk(LANES, ATT_WIDTH), blk(ATT_WIDTH, p), blk(ATT_WIDTH, p), blk(ATT_WIDTH, LANES),
                  blk(ATT_WIDTH, LANES), blk(LANES, p), blk(LANES, LANES)],
        out_specs=blk(t_new, ATT_WIDTH),
        out_shape=jax.ShapeDtypeStruct((b, t_new, ATT_WIDTH), bf16),
        compiler_params=_params("parallel"),
        name="attn_sample",
    )(qbd, ck, cv, kn, vn, bias_c, bias_n)


def _route(logits):
    lane = lax.broadcasted_iota(jnp.int32, logits.shape, 1).astype(f32)
    ninf = -jnp.inf
    gl = jnp.where(lane < N_GROUPS, logits, ninf)
    gmax = jnp.max(gl, axis=-1, keepdims=True)
    grp = jnp.min(jnp.where(gl == gmax, lane, float(LANES)), axis=-1, keepdims=True)
    p_sel = 1.0 / jnp.sum(jnp.exp(gl - gmax), axis=-1, keepdims=True)
    e_lo = EXPERT_LANE0 + grp * EXPERTS_PER_GROUP
    el = jnp.where(lane >= e_lo, jnp.where(lane < e_lo + EXPERTS_PER_GROUP, logits, ninf), ninf)
    t1 = jnp.max(el, axis=-1, keepdims=True)
    i1 = jnp.min(jnp.where(el == t1, lane, float(LANES)), axis=-1, keepdims=True)
    el2 = jnp.where(lane == i1, ninf, el)
    t2 = jnp.max(el2, axis=-1, keepdims=True)
    i2 = jnp.min(jnp.where(el2 == t2, lane, float(LANES)), axis=-1, keepdims=True)
    e2 = jnp.exp(t2 - t1)
    w1 = p_sel / (1.0 + e2)
    w2 = p_sel * e2 / (1.0 + e2)
    picks = (i1 - EXPERT_LANE0, i2 - EXPERT_LANE0, w1, w2)
    out = jnp.zeros(logits.shape, f32)
    for n, val in enumerate(picks):
        out = jnp.where(lane == float(n), val, out)
    return out


def _tail_kernel(x_ref, o_ref, u_ref, uprev_ref, ctx_ref, wdw_ref, bdw_ref, gcl_ref, bcl_ref,
                 wa_ref, wc_ref, wg_ref, wo_ref, g1_ref, b1_ref, wr_ref, br_ref,
                 h_ref, gates_ref, uext_sc, shift_sc, c_sc, *, cb, ct):
    i = pl.program_id(1)
    bb, t, _ = x_ref.shape
    n = bb * t

    x = x_ref[...].reshape(n, D_MODEL)
    g = _sigmoid(jnp.dot(x.astype(bf16), wg_ref[...], preferred_element_type=f32))
    branch_a = jnp.dot(o_ref[...].reshape(n, ATT_WIDTH), wa_ref[...], preferred_element_type=f32)

    uext_sc[:, 0:HALO, :] = jnp.where(i == 0, ctx_ref[...], uprev_ref[...])
    uext_sc[:, HALO:HALO + t, :] = u_ref[...]
    span = t + HALO - SUBLANES
    for r in range(1, SUBLANES):
        shift_sc[r - 1] = uext_sc[:, r:r + span, :]
    for b0 in range(0, bb, cb):
        for t0 in range(0, t, ct):
            acc = jnp.zeros((cb, ct, D_CONV), f32) + bdw_ref[...]
            for k in range(CONV_WIDTH):
                a, r = divmod(HALO_PAD + k, SUBLANES)
                r0 = t0 + a * SUBLANES
                src = uext_sc[b0:b0 + cb, r0:r0 + ct, :] if r == 0 else shift_sc[r - 1, b0:b0 + cb, r0:r0 + ct, :]
                acc = acc + wdw_ref[k:k + 1, :] * src
            c_sc[b0:b0 + cb, t0:t0 + ct, :] = acc

    c = _layernorm(c_sc[...].reshape(n, D_CONV), gcl_ref[...], bcl_ref[...])
    c = c * _sigmoid(c)
    branch_b = jnp.dot(c.astype(bf16), wc_ref[...], preferred_element_type=f32)
    merged = g[:, :D_MODEL] * branch_a + g[:, D_MODEL:] * branch_b
    mix = jnp.dot(merged.astype(bf16), wo_ref[...], preferred_element_type=f32)
    h = _layernorm(ALPHA * x + mix, g1_ref[...], b1_ref[...])
    h_ref[...] = h.reshape(bb, t, D_MODEL)
    w_hi, w_lo = _split2(wr_ref[...])
    w_parts = jnp.concatenate([w_hi, w_lo], axis=1)
    cross = sum(jnp.dot(part, w_parts, preferred_element_type=f32) for part in _split2(h))
    logits = cross[:, :LANES] + cross[:, LANES:] + br_ref[...]
    gates_ref[...] = _route(logits).reshape(bb, t, LANES)


def _tail(x, o, u, ctx, w, bb, t, cb, ct):
    b, s, _ = x.shape
    nt = s // t
    uprev = u if nt > 1 else ctx
    per_tile = t // HALO
    tile = lambda width: pl.BlockSpec((bb, t, width), lambda b_, i: (b_, i, 0))
    full = lambda a: pl.BlockSpec(a.shape, lambda b_, i: (0,) * a.ndim)
    weights = [w["w_dw"], w["b_dw"], w["g_cln"], w["b_cln"], w["w_attn_out"], w["w_conv_out"], w["w_gate"],
               w["w_out"], w["g_ln1"], w["b_ln1"], w["w_route"], w["b_route"]]
    return pl.pallas_call(
        functools.partial(_tail_kernel, cb=cb, ct=ct),
        grid=(b // bb, nt),
        in_specs=[tile(D_MODEL), tile(ATT_WIDTH), tile(D_CONV),
                  pl.BlockSpec((bb, HALO, D_CONV), lambda b_, i: (b_, jnp.maximum(i * per_tile - 1, 0), 0)),
                  pl.BlockSpec((bb, HALO, D_CONV), lambda b_, i: (b_, 0, 0))]
                 + [full(a) for a in weights],
        out_specs=[tile(D_MODEL), tile(LANES)],
        out_shape=[jax.ShapeDtypeStruct((b, s, D_MODEL), f32), jax.ShapeDtypeStruct((b, s, LANES), f32)],
        scratch_shapes=[pltpu.VMEM((bb, HALO + t, D_CONV), f32),
                        pltpu.VMEM((SUBLANES - 1, bb, t + HALO - SUBLANES, D_CONV), f32),
                        pltpu.VMEM((bb, t, D_CONV), f32)],
        compiler_params=_params("parallel", "arbitrary"),
        name="tail",
    )(x, o, u, uprev, ctx, *weights)


MOE_SUB = 256
MOE_CAP = 32
MOE_EPC = 8
MOE_KEY = 4096
_CAP_SHIFT = MOE_CAP.bit_length() - 1


def _moe_kernel(h_ref, r_ref, w1_ref, w3_ref, w2_ref, g2_ref, b2_ref, y_ref,
                acc_sc, hb_sc, key_sc, keyt_sc, nr_sc, *, n_sub):
    c = pl.program_id(1)
    slots = MOE_EPC * MOE_CAP

    @pl.when(c == 0)
    def _():
        acc_sc[...] = jnp.zeros(acc_sc.shape, f32)
        hb_sc[...] = h_ref[...].astype(bf16)
        lane = lax.broadcasted_iota(jnp.int32, (MOE_SUB, LANES), 1).astype(f32)
        ti = lax.broadcasted_iota(jnp.int32, (MOE_SUB, MOE_SUB), 0)
        tj = lax.broadcasted_iota(jnp.int32, (MOE_SUB, MOE_SUB), 1)
        earlier = jnp.where(tj < ti, 1.0, 0.0).astype(bf16)
        most = jnp.zeros((1, LANES), f32)
        for a in range(n_sub):
            r = r_ref[a * MOE_SUB:(a + 1) * MOE_SUB, :]
            e1, e2 = r[:, 0:1], r[:, 1:2]
            oh1 = jnp.where(lane == e1, 1.0, 0.0)
            oh2 = jnp.where(lane == e2, 1.0, 0.0)
            both = oh1 + oh2
            before = jnp.dot(earlier, both.astype(bf16), preferred_element_type=f32)
            key1 = e1 * MOE_KEY + jnp.sum(before * oh1, axis=-1, keepdims=True)
            key2 = e2 * MOE_KEY + jnp.sum(before * oh2, axis=-1, keepdims=True)
            info = jnp.where(lane == 0.0, key1, jnp.where(lane == 1.0, key2, r))
            key_sc[a] = info
            keyt_sc[a] = info.T[0:8, :]
            most = jnp.maximum(most, jnp.sum(both, axis=0, keepdims=True))
        nr_sc[0] = (jnp.max(most).astype(jnp.int32) + (MOE_CAP - 1)) // MOE_CAP

    e0 = c * MOE_EPC
    s_row = lax.broadcasted_iota(jnp.int32, (slots, MOE_SUB), 0)
    s_col = lax.broadcasted_iota(jnp.int32, (MOE_SUB, slots), 1)

    def slot_key(s, rd):
        expert = lax.shift_right_logical(s, _CAP_SHIFT) + e0
        return (expert * MOE_KEY + jnp.bitwise_and(s, MOE_CAP - 1) + rd * MOE_CAP).astype(f32)

    def one_round(rd, carry):
        key_r = slot_key(s_row, rd)
        key_c = slot_key(s_col, rd)
        xs = []
        for a in range(n_sub):
            k1, k2 = keyt_sc[a, 0:1, :], keyt_sc[a, 1:2, :]
            p = jnp.where(key_r == k1, 1.0, jnp.where(key_r == k2, 1.0, 0.0)).astype(bf16)
            hb = hb_sc[a * MOE_SUB:(a + 1) * MOE_SUB, :]
            xs.append(jnp.dot(p, hb, preferred_element_type=f32).astype(bf16))
        ys = [[] for _ in range(n_sub)]
        for e in range(MOE_EPC):
            xe = jnp.concatenate([x[e * MOE_CAP:(e + 1) * MOE_CAP] for x in xs], axis=0)
            a1 = jnp.dot(xe, w1_ref[e], preferred_element_type=f32)
            a3 = jnp.dot(xe, w3_ref[e], preferred_element_type=f32)
            act = (a1 * _sigmoid(a1)) * a3
            ye = jnp.dot(act.astype(bf16), w2_ref[e], preferred_element_type=f32).astype(bf16)
            for a in range(n_sub):
                ys[a].append(ye[a * MOE_CAP:(a + 1) * MOE_CAP])
        for a in range(n_sub):
            info = key_sc[a]
            pw = jnp.where(key_c == info[:, 0:1], info[:, 2:3],
                           jnp.where(key_c == info[:, 1:2], info[:, 3:4], 0.0)).astype(bf16)
            rows = slice(a * MOE_SUB, (a + 1) * MOE_SUB)
            acc_sc[rows, :] += jnp.dot(pw, jnp.concatenate(ys[a], axis=0), preferred_element_type=f32)
        return carry

    lax.fori_loop(0, nr_sc[0], one_round, 0)

    @pl.when(c == pl.num_programs(1) - 1)
    def _():
        y_ref[...] = _layernorm(ALPHA * h_ref[...] + acc_sc[...], g2_ref[...], b2_ref[...])


def _moe(h2d, route, w1, w3, w2, g2, b2, tm):
    n = h2d.shape[0]
    n_sub = tm // MOE_SUB
    expert_block = lambda a: pl.BlockSpec((MOE_EPC,) + a.shape[1:], lambda i, c: (c, 0, 0))
    return pl.pallas_call(
        functools.partial(_moe_kernel, n_sub=n_sub),
        grid=(n // tm, N_EXPERTS // MOE_EPC),
        in_specs=[pl.BlockSpec((tm, D_MODEL), lambda i, c: (i, 0)),
                  pl.BlockSpec((tm, LANES), lambda i, c: (i, 0)),
                  expert_block(w1), expert_block(w3), expert_block(w2),
                  pl.BlockSpec((1, D_MODEL), lambda i, c: (0, 0)),
                  pl.BlockSpec((1, D_MODEL), lambda i, c: (0, 0))],
        out_specs=pl.BlockSpec((tm, D_MODEL), lambda i, c: (i, 0)),
        out_shape=jax.ShapeDtypeStruct((n, D_MODEL), f32),
        scratch_shapes=[pltpu.VMEM((tm, D_MODEL), f32), pltpu.VMEM((tm, D_MODEL), bf16),
                        pltpu.VMEM((n_sub, MOE_SUB, LANES), f32), pltpu.VMEM((n_sub, 8, MOE_SUB), f32),
                        pltpu.SMEM((1,), jnp.int32)],
        compiler_params=_params("parallel", "arbitrary"),
        name="moe",
    )(h2d, route, w1, w3, w2, g2, b2)


def _pack_weights(w_in, b_forget, w_dw, b_dw, g_conv_ln, b_conv_ln, w_attn_out, w_conv_out, w_out,
                  g_ln1, b_ln1, w_group, b_group, w_router, b_router, w1, w3, w2, g_ln2, b_ln2):
    a = ATT_WIDTH
    c_f, c_glu, c_gate = 3 * a, 3 * a + N_HEADS, 3 * a + N_HEADS + 2 * D_CONV
    w_f = jnp.pad(w_in[:, c_f:c_glu], ((0, 0), (0, LANES - N_HEADS)))
    w_pack = jnp.concatenate([w_in[:, :c_f], w_in[:, c_glu:c_gate], w_f], axis=1).astype(bf16)
    pad_route = LANES - N_GROUPS - N_EXPERTS
    row = lambda v: v.reshape(1, -1)
    return dict(
        w_pack=w_pack,
        b_forget=jnp.pad(b_forget, (0, LANES - N_HEADS)).reshape(1, LANES),
        w_gate=w_in[:, c_gate:].astype(bf16),
        w_dw=jnp.pad(w_dw, ((0, HALO - CONV_WIDTH), (0, 0))), b_dw=row(b_dw),
        g_cln=row(g_conv_ln), b_cln=row(b_conv_ln),
        w_attn_out=w_attn_out.astype(bf16), w_conv_out=w_conv_out.astype(bf16), w_out=w_out.astype(bf16),
        g_ln1=row(g_ln1), b_ln1=row(b_ln1),
        w_route=jnp.pad(jnp.concatenate([w_group, w_router], axis=1), ((0, 0), (0, pad_route))),
        b_route=jnp.pad(jnp.concatenate([b_group, b_router]), (0, pad_route)).reshape(1, LANES),
        w1=w1.astype(bf16), w3=w3.astype(bf16), w2=w2.astype(bf16),
        g_ln2=row(g_ln2), b_ln2=row(b_ln2),
    )


def _finish(x, o, u, ctx, w, bb, t, cb, ct, tm_moe):
    b, s, _ = x.shape
    h, gates = _tail(x, o, u, ctx, w, bb, t, cb, ct)
    y = _moe(h.reshape(b * s, D_MODEL), gates.reshape(b * s, LANES), w["w1"], w["w3"], w["w2"],
             w["g_ln2"], w["b_ln2"], tm_moe)
    return y.reshape(b, s, D_MODEL)


def _layer(xp, xs, cache_k, cache_v, cache_logf, cache_conv, w):
    bp, sp, _ = xp.shape
    bs, ts, _ = xs.shape
    past = cache_k.shape[1]
    tq, tk = ATTN_Q_TILE, TOKEN_TILE

    q, k, kb, v, vt, vb, lf, u = _in_proj(xp.reshape(bp * sp, D_MODEL), w["w_pack"], w["b_forget"], tk, bp)
    logf = lf.transpose(0, 2, 1)
    parts = _cumsum_lanes(lf.reshape(bp * N_HEADS, sp), reverse=False, split_scale=-LOG2E)
    parts = parts.reshape(N_SPLIT, bp, ATT_WIDTH // LANES, 2, sp)
    shp = (bp, sp, ATT_WIDTH)
    o = _attn_prompt(q.reshape(shp), kb.reshape(shp), parts, vt, tq)
    u = u.reshape(bp, sp, D_CONV)
    yp = _finish(xp, o, u, jnp.zeros((bp, HALO, D_CONV), f32), w, 1, TOKEN_TILE, 1, CONV_ROWS, MOE_TILE)
    outs_p = (k.reshape(bp, sp, N_HEADS, HEAD_DIM), v.reshape(bp, sp, N_HEADS, HEAD_DIM), logf,
              u[:, sp - CONV_CTX:])

    q, k, kb, v, _, vb, lf, u = _in_proj(xs.reshape(bs * ts, D_MODEL), w["w_pack"], w["b_forget"], bs * ts, 1)
    lf = lf.reshape(N_HEADS, bs, ts)
    logf = lf.transpose(1, 2, 0)
    lf_t = jnp.pad(lf.transpose(1, 0, 2), ((0, 0), (0, 0), (0, LANES - ts)))
    cq = _cumsum_lanes(lf_t.reshape(bs * N_HEADS, LANES), reverse=False).reshape(bs, N_HEADS, LANES)[:, :, :ts]
    r = _cumsum_lanes(cache_logf.transpose(0, 2, 1).reshape(bs * N_HEADS, past), reverse=True)
    bias_c = jnp.repeat(LOG2E * r.reshape(bs, N_HEADS, past), ts, axis=1)
    causal = jnp.arange(ts)[None, :] <= jnp.arange(ts)[:, None]
    bias_n = jnp.where(causal[None, None], -LOG2E * cq[:, :, None, :], NEG)
    bias_n = jnp.pad(bias_n.reshape(bs, LANES, ts), ((0, 0), (0, 0), (0, LANES - ts)), constant_values=NEG)
    q4 = q.reshape(bs, ts, N_HEADS, HEAD_DIM).transpose(0, 2, 1, 3)
    qbd = (q4[:, :, :, None, :] * jnp.eye(N_HEADS, dtype=bf16)[None, :, None, :, None]).reshape(bs, LANES, ATT_WIDTH)
    new_t = lambda a: jnp.pad(a.reshape(bs, ts, ATT_WIDTH).transpose(0, 2, 1), ((0, 0), (0, 0), (0, LANES - ts)))
    cache_t = lambda a: a.transpose(0, 2, 3, 1).reshape(bs, ATT_WIDTH, past)
    o = _attn_sample(qbd, cache_t(cache_k), cache_t(cache_v), new_t(kb), new_t(vb), bias_c, bias_n, ts)
    u = u.reshape(bs, ts, D_CONV)
    ctx = jnp.pad(cache_conv, ((0, 0), (HALO_PAD, 0), (0, 0)))
    ys = _finish(xs, o, u, ctx, w, bs, ts, 2, ts, bs * ts)
    u_ext = jnp.concatenate([cache_conv, u], axis=1)
    outs_s = (k.reshape(bs, ts, N_HEADS, HEAD_DIM), v.reshape(bs, ts, N_HEADS, HEAD_DIM), logf,
              u_ext[:, u_ext.shape[1] - CONV_CTX:])
    return yp, ys, outs_p, outs_s


def kernel(x_prompt, x_sample, cache_k, cache_v, cache_logf, cache_conv, w_in, b_forget, w_dw, b_dw,
           g_conv_ln, b_conv_ln, w_attn_out, w_conv_out, w_out, g_ln1, b_ln1, w_group, b_group,
           w_router, b_router, w1, w3, w2, g_ln2, b_ln2):
    xp, xs = x_prompt, x_sample
    per_layer_p, per_layer_s = [], []
    for l in range(DEPTH):
        w = _pack_weights(w_in[l], b_forget[l], w_dw[l], b_dw[l], g_conv_ln[l], b_conv_ln[l], w_attn_out[l],
                          w_conv_out[l], w_out[l], g_ln1[l], b_ln1[l], w_group[l], b_group[l], w_router[l],
                          b_router[l], w1[l], w3[l], w2[l], g_ln2[l], b_ln2[l])
        xp, xs, outs_p, outs_s = _layer(xp, xs, cache_k[l], cache_v[l], cache_logf[l], cache_conv[l], w)
        per_layer_p.append(outs_p)
        per_layer_s.append(outs_s)
    stack = lambda outs, j: jnp.stack([o[j] for o in outs])
    return (xp, xs,
            stack(per_layer_p, 0), stack(per_layer_p, 1), stack(per_layer_p, 2), stack(per_layer_p, 3),
            stack(per_layer_s, 0), stack(per_layer_s, 1), stack(per_layer_s, 2), stack(per_layer_s, 3))
```

```python
import functools

import jax
import jax.numpy as jnp
from jax import lax
from jax.experimental import pallas as pl
from jax.experimental.pallas import tpu as pltpu

f32 = jnp.float32
bf16 = jnp.bfloat16

D_MODEL = 1024
N_HEADS = 8
HEAD_DIM = 64
ATT_WIDTH = N_HEADS * HEAD_DIM
D_CONV = 512
CONV_WIDTH = 31
CONV_CTX = CONV_WIDTH - 1
N_GROUPS = 4
EXPERTS_PER_GROUP = 8
N_EXPERTS = N_GROUPS * EXPERTS_PER_GROUP
D_EXPERT = 256
DEPTH = 1
ALPHA = (2 * DEPTH) ** 0.25
LN_EPS = 1e-5
NEG = -1e30
SCALE = HEAD_DIM ** -0.5
LOG2E = 1.4426950408889634

N_SPLIT = 3
LANES = 128
SUBLANES = 8
HALO = 32
HALO_PAD = HALO - CONV_CTX
EXPERT_LANE0 = N_GROUPS
VMEM_LIMIT = 56 * 1024 * 1024

TOKEN_TILE = 512
ATTN_Q_TILE = 1024
CONV_ROWS = 32
MOE_TILE = 1024

_C_Q, _C_K, _C_V, _C_GA, _C_GB, _C_F, _C_END = 0, 512, 1024, 1536, 2048, 2560, 2688


def _sigmoid(x):
    return 1.0 / (1.0 + jnp.exp(-x))


def _layernorm(x, g, b):
    mu = jnp.mean(x, axis=-1, keepdims=True)
    xc = x - mu
    var = jnp.mean(xc * xc, axis=-1, keepdims=True)
    return xc * lax.rsqrt(var + LN_EPS) * g + b


def _params(*sem):
    return pltpu.CompilerParams(dimension_semantics=sem, vmem_limit_bytes=VMEM_LIMIT)


def _store_head_major(ref, val):
    tokens = val.shape[0]
    for h in range(N_HEADS):
        ref[pl.ds(h, tokens, stride=N_HEADS), :] = val[:, h * HEAD_DIM:(h + 1) * HEAD_DIM]


def _in_proj_kernel(x_ref, w_ref, bf_ref, q_ref, k_ref, kb_ref, v_ref, vt_ref, vb_ref, lf_ref, u_ref):
    tm = x_ref.shape[0]
    xb = x_ref[...].astype(bf16)

    def proj(lo, hi):
        return jnp.dot(xb, w_ref[:, lo:hi], preferred_element_type=f32)

    q_ref[...] = (proj(_C_Q, _C_K) * (SCALE * LOG2E)).astype(bf16)
    k = proj(_C_K, _C_V)
    _store_head_major(k_ref, k)
    kb_ref[...] = k.astype(bf16)
    v = proj(_C_V, _C_GA)
    _store_head_major(v_ref, v)
    vb_ref[...] = v.astype(bf16)
    vt_ref[0, :, 0] = v.T.astype(bf16).reshape(ATT_WIDTH // LANES, LANES, tm)
    u_ref[...] = proj(_C_GA, _C_GB) * _sigmoid(proj(_C_GB, _C_F))
    z = proj(_C_F, _C_END) + bf_ref[...]
    log_f = jnp.minimum(z, 0.0) - jnp.log(1.0 + jnp.exp(-jnp.abs(z)))
    lf_ref[0] = log_f.T[0:N_HEADS, :]


def _in_proj(x2d, w_pack, bf_pad, tm, groups):
    n = x2d.shape[0]
    cols = n // groups
    tpg = cols // tm
    row = lambda width: pl.BlockSpec((tm, width), lambda i: (i, 0))
    heads = pl.BlockSpec((tm * N_HEADS, HEAD_DIM), lambda i: (i, 0))
    hp = ATT_WIDTH // LANES
    return pl.pallas_call(
        _in_proj_kernel,
        grid=(n // tm,),
        in_specs=[row(D_MODEL),
                  pl.BlockSpec((D_MODEL, _C_END), lambda i: (0, 0)),
                  pl.BlockSpec((1, LANES), lambda i: (0, 0))],
        out_specs=[row(ATT_WIDTH), heads, row(ATT_WIDTH), heads,
                   pl.BlockSpec((1, hp, 1, LANES, tm), lambda i: (i // tpg, 0, i % tpg, 0, 0)),
                   row(ATT_WIDTH),
                   pl.BlockSpec((1, N_HEADS, tm), lambda i: (i // tpg, 0, i % tpg)),
                   row(D_CONV)],
        out_shape=[jax.ShapeDtypeStruct((n, ATT_WIDTH), bf16),
                   jax.ShapeDtypeStruct((n * N_HEADS, HEAD_DIM), f32),
                   jax.ShapeDtypeStruct((n, ATT_WIDTH), bf16),
                   jax.ShapeDtypeStruct((n * N_HEADS, HEAD_DIM), f32),
                   jax.ShapeDtypeStruct((groups, hp, tpg, LANES, tm), bf16),
                   jax.ShapeDtypeStruct((n, ATT_WIDTH), bf16),
                   jax.ShapeDtypeStruct((groups, N_HEADS, cols), f32),
                   jax.ShapeDtypeStruct((n, D_CONV), f32)],
        compiler_params=_params("parallel"),
        name="in_proj",
    )(x2d, w_pack, bf_pad)


def _split2(x):
    hi = x.astype(bf16)
    return hi, (x - hi.astype(f32)).astype(bf16)


def _split3(x):
    hi = x.astype(bf16)
    r1 = x - hi.astype(f32)
    mid = r1.astype(bf16)
    lo = (r1 - mid.astype(f32)).astype(bf16)
    return hi, mid, lo


def _cumsum_kernel(x_ref, o_ref, *, reverse, split_scale):
    nc, rows, _ = x_ref.shape
    r = lax.broadcasted_iota(jnp.int32, (LANES, LANES), 0)
    c = lax.broadcasted_iota(jnp.int32, (LANES, LANES), 1)
    tri = jnp.where((r > c) if reverse else (r <= c), 1.0, 0.0).astype(bf16)
    carry = jnp.zeros((rows, 1), f32)
    for j in (range(nc - 1, -1, -1) if reverse else range(nc)):
        xj = x_ref[j]
        y = carry
        for part in _split3(xj):
            y = y + jnp.dot(part, tri, preferred_element_type=f32)
        if split_scale is None:
            o_ref[j] = y
        else:
            for n, part in enumerate(_split3(split_scale * y)):
                o_ref[n, j] = part.astype(f32)
        carry = carry + jnp.sum(xj, axis=-1, keepdims=True)


def _cumsum_lanes(x, reverse, split_scale=None):
    rows, s = x.shape
    nc = s // LANES
    xc = x.reshape(rows, nc, LANES).transpose(1, 0, 2)
    shape = (nc, rows, LANES) if split_scale is None else (N_SPLIT, nc, rows, LANES)
    out = pl.pallas_call(
        functools.partial(_cumsum_kernel, reverse=reverse, split_scale=split_scale),
        out_shape=jax.ShapeDtypeStruct(shape, f32),
        name="cumsum_rev" if reverse else "cumsum_fwd",
    )(xc)
    if split_scale is None:
        return out.transpose(1, 0, 2).reshape(rows, s)
    return out.transpose(0, 2, 1, 3).reshape(N_SPLIT, rows, s)


def _attn_kernel(q_ref, k_ref, p_ref, vt_ref, o_ref, qx_sc, acc_sc, s0_sc, s1_sc, e_sc, *, tq, tk):
    i = pl.program_id(2)
    per_q = tq // tk

    @pl.when(i == 0)
    def _():
        row_k = lax.broadcasted_iota(jnp.int32, (LANES, tk), 0)
        for c in range(e_sc.shape[0] // tk):
            chans = jnp.zeros((LANES, tk), f32)
            for hh in range(2):
                for n in range(N_SPLIT):
                    part = p_ref[n, 0, 0, hh:hh + 1, c * tk:(c + 1) * tk]
                    chans = jnp.where(row_k == N_SPLIT * hh + n, part, chans)
            e_sc[c * tk:(c + 1) * tk, :] = chans.T.astype(bf16)

    qt = q_ref[0].astype(f32).T
    row = lax.broadcasted_iota(jnp.int32, (LANES, tq), 0)
    for h in range(2):
        qx_sc[h, 0:LANES, :] = jnp.where(row // HEAD_DIM == h, qt, 0.0).astype(bf16)
        qx_sc[h, LANES:2 * LANES, :] = jnp.where(row // N_SPLIT == h, 1.0, 0.0).astype(bf16)
    acc_sc[...] = jnp.zeros(acc_sc.shape, f32)
    ones_rows = jnp.ones((16, tk), bf16)

    s_sc = (s0_sc, s1_sc)

    def scores(j, h, lo=0):
        start = pl.multiple_of(j * tk, tk)
        kx = jnp.concatenate([k_ref[0, pl.ds(start, tk), :], e_sc[pl.ds(start, tk), :]], axis=1)
        s_sc[h][:, lo:] = jnp.dot(kx, qx_sc[h, :, lo:], preferred_element_type=f32)

    def absorb(j, h, stats, lo=0, masked=False):
        m_all, l_all = stats
        m_prev, l_prev = m_all[:, lo:], l_all[:, lo:]
        s = s_sc[h][:, lo:]
        if masked:
            key = lax.broadcasted_iota(jnp.int32, s.shape, 0)
            qry = lax.broadcasted_iota(jnp.int32, s.shape, 1)
            s = jnp.where(key <= qry, s, NEG)
        m_new = jnp.maximum(m_prev, jnp.max(s, axis=0, keepdims=True))
        alpha = jnp.exp2(m_prev - m_new)
        p = jnp.exp2(s - m_new).astype(bf16)
        vx = jnp.concatenate([vt_ref[0, 0, j, h * HEAD_DIM:(h + 1) * HEAD_DIM, :], ones_rows], axis=0)
        pv = jnp.dot(vx, p, preferred_element_type=f32)
        acc_sc[h, :, lo:] = alpha * acc_sc[h, :, lo:] + pv[0:HEAD_DIM, :]
        l_new = alpha * l_prev + pv[HEAD_DIM:HEAD_DIM + 1, :]
        if lo:
            m_new = jnp.concatenate([m_all[:, :lo], m_new], axis=1)
            l_new = jnp.concatenate([l_all[:, :lo], l_new], axis=1)
        return m_new, l_new

    def body(j, stats):
        scores(j, 1)
        st0 = absorb(j, 0, stats[0])
        scores(j + 1, 0)
        st1 = absorb(j, 1, stats[1])
        return st0, st1

    init = tuple((jnp.full((1, tq), NEG, f32), jnp.zeros((1, tq), f32)) for _ in range(2))
    scores(0, 0)
    stats = lax.fori_loop(0, (per_q // 2) * i, lambda p, st: body(2 * p + 1, body(2 * p, st)), init)
    st0, st1 = stats
    j0 = per_q * i
    for d in range(per_q):
        scores(j0 + d, 1, d * tk)
        st0 = absorb(j0 + d, 0, st0, d * tk, masked=True)
        if d + 1 < per_q:
            scores(j0 + d + 1, 0, (d + 1) * tk)
        st1 = absorb(j0 + d, 1, st1, d * tk, masked=True)
    l0, l1 = st0[1], st1[1]
    ot = jnp.concatenate([acc_sc[0] / l0, acc_sc[1] / l1], axis=0)
    o_ref[0] = ot.T.astype(bf16)


def _attn_prompt(q, kb, parts, vt, tq):
    b, s, _ = q.shape
    hp = ATT_WIDTH // LANES
    nq = s // tq
    nk, tk = vt.shape[2], vt.shape[4]
    assert tq % (2 * tk) == 0
    return pl.pallas_call(
        functools.partial(_attn_kernel, tq=tq, tk=tk),
        grid=(b, hp, nq),
        in_specs=[pl.BlockSpec((1, tq, LANES), lambda b_, h_, i: (b_, i, h_)),
                  pl.BlockSpec((1, s, LANES), lambda b_, h_, i: (b_, 0, h_)),
                  pl.BlockSpec((N_SPLIT, 1, 1, 2, s), lambda b_, h_, i: (0, b_, h_, 0, 0)),
                  pl.BlockSpec((1, 1, nk, LANES, tk), lambda b_, h_, i: (b_, h_, 0, 0, 0))],
        out_specs=pl.BlockSpec((1, tq, LANES), lambda b_, h_, i: (b_, i, h_)),
        out_shape=jax.ShapeDtypeStruct((b, s, ATT_WIDTH), bf16),
        scratch_shapes=[pltpu.VMEM((2, 2 * LANES, tq), bf16), pltpu.VMEM((2, HEAD_DIM, tq), f32),
                        pltpu.VMEM((tk, tq), f32), pltpu.VMEM((tk, tq), f32), pltpu.VMEM((s, LANES), bf16)],
        compiler_params=_params("parallel", "parallel", "arbitrary"),
        name="attn_prompt",
    )(q, kb, parts, vt)


def _attn_sample_kernel(qbd_ref, ck_ref, cv_ref, kn_ref, vn_ref, bc_ref, bn_ref, o_ref, *, t_new):
    qbd = qbd_ref[0]
    s_c = jnp.dot(qbd, ck_ref[0].astype(bf16), preferred_element_type=f32) + bc_ref[0]
    s_n = jnp.dot(qbd, kn_ref[0], preferred_element_type=f32) + bn_ref[0]
    m = jnp.maximum(jnp.max(s_c, axis=-1, keepdims=True), jnp.max(s_n, axis=-1, keepdims=True))
    p_c = jnp.exp2(s_c - m)
    p_n = jnp.exp2(s_n - m)
    inv = 1.0 / (jnp.sum(p_c, axis=-1, keepdims=True) + jnp.sum(p_n, axis=-1, keepdims=True))
    p_c = (p_c * inv).astype(bf16)
    p_n = (p_n * inv).astype(bf16)
    over_positions = (((1,), (1,)), ((), ()))
    o_all = (lax.dot_general(cv_ref[0].astype(bf16), p_c, over_positions, preferred_element_type=f32)
             + lax.dot_general(vn_ref[0], p_n, over_positions, preferred_element_type=f32))
    shape = (ATT_WIDTH, LANES)
    row_head = lax.shift_right_logical(lax.broadcasted_iota(jnp.int32, shape, 0), HEAD_DIM.bit_length() - 1)
    col_head = lax.shift_right_logical(lax.broadcasted_iota(jnp.int32, shape, 1), t_new.bit_length() - 1)
    own = jnp.where(row_head == col_head, o_all, 0.0).astype(bf16)
    q_of_row = jnp.bitwise_and(lax.broadcasted_iota(jnp.int32, (LANES, LANES), 0), t_new - 1)
    pick = jnp.where(q_of_row == lax.broadcasted_iota(jnp.int32, (LANES, LANES), 1), 1.0, 0.0).astype(bf16)
    o_t = jnp.dot(own, pick, preferred_element_type=f32)
    o_ref[0] = o_t.T[0:t_new, :].astype(bf16)


def _attn_sample(qbd, ck, cv, kn, vn, bias_c, bias_n, t_new):
    b, _, p = ck.shape
    blk = lambda r, c: pl.BlockSpec((1, r, c), lambda i: (i, 0, 0))
    return pl.pallas_call(
        functools.partial(_attn_sample_kernel, t_new=t_new),
        grid=(b,),
        in_specs=[blk(LANES, ATT_WIDTH), blk(ATT_WIDTH, p), blk(ATT_WIDTH, p), blk(ATT_WIDTH, LANES),
                  blk(ATT_WIDTH, LANES), blk(LANES, p), blk(LANES, LANES)],
        out_specs=blk(t_new, ATT_WIDTH),
        out_shape=jax.ShapeDtypeStruct((b, t_new, ATT_WIDTH), bf16),
        compiler_params=_params("parallel"),
        name="attn_sample",
    )(qbd, ck, cv, kn, vn, bias_c, bias_n)


def _route(logits):
    lane = lax.broadcasted_iota(jnp.int32, logits.shape, 1).astype(f32)
    ninf = -jnp.inf
    gl = jnp.where(lane < N_GROUPS, logits, ninf)
    gmax = jnp.max(gl, axis=-1, keepdims=True)
    grp = jnp.min(jnp.where(gl == gmax, lane, float(LANES)), axis=-1, keepdims=True)
    p_sel = 1.0 / jnp.sum(jnp.exp(gl - gmax), axis=-1, keepdims=True)
    e_lo = EXPERT_LANE0 + grp * EXPERTS_PER_GROUP
    el = jnp.where(lane >= e_lo, jnp.where(lane < e_lo + EXPERTS_PER_GROUP, logits, ninf), ninf)
    t1 = jnp.max(el, axis=-1, keepdims=True)
    i1 = jnp.min(jnp.where(el == t1, lane, float(LANES)), axis=-1, keepdims=True)
    el2 = jnp.where(lane == i1, ninf, el)
    t2 = jnp.max(el2, axis=-1, keepdims=True)
    i2 = jnp.min(jnp.where(el2 == t2, lane, float(LANES)), axis=-1, keepdims=True)
    e2 = jnp.exp(t2 - t1)
    w1 = p_sel / (1.0 + e2)
    w2 = p_sel * e2 / (1.0 + e2)
    picks = (i1 - EXPERT_LANE0, i2 - EXPERT_LANE0, w1, w2)
    out = jnp.zeros(logits.shape, f32)
    for n, val in enumerate(picks):
        out = jnp.where(lane == float(n), val, out)
    return out


def _tail_kernel(x_ref, o_ref, u_ref, uprev_ref, ctx_ref, wdw_ref, bdw_ref, gcl_ref, bcl_ref,
                 wa_ref, wc_ref, wg_ref, wo_ref, g1_ref, b1_ref, wr_ref, br_ref,
                 h_ref, gates_ref, uext_sc, shift_sc, c_sc, *, cb, ct):
    i = pl.program_id(1)
    bb, t, _ = x_ref.shape
    n = bb * t

    x = x_ref[...].reshape(n, D_MODEL)
    g = _sigmoid(jnp.dot(x.astype(bf16), wg_ref[...], preferred_element_type=f32))
    branch_a = jnp.dot(o_ref[...].reshape(n, ATT_WIDTH), wa_ref[...], preferred_element_type=f32)

    uext_sc[:, 0:HALO, :] = jnp.where(i == 0, ctx_ref[...], uprev_ref[...])
    uext_sc[:, HALO:HALO + t, :] = u_ref[...]
    span = t + HALO - SUBLANES
    for r in range(1, SUBLANES):
        shift_sc[r - 1] = uext_sc[:, r:r + span, :]
    for b0 in range(0, bb, cb):
        for t0 in range(0, t, ct):
            groups = cb * ct // SUBLANES
            acc = jnp.zeros((groups, SUBLANES, D_CONV), f32) + bdw_ref[...]
            for k in range(CONV_WIDTH):
                a, r = divmod(HALO_PAD + k, SUBLANES)
                r0 = t0 + a * SUBLANES
                src = uext_sc[b0:b0 + cb, r0:r0 + ct, :] if r == 0 else shift_sc[r - 1, b0:b0 + cb, r0:r0 + ct, :]
                acc = acc + wdw_ref[k] * src.reshape(groups, SUBLANES, D_CONV)
            c_sc[b0:b0 + cb, t0:t0 + ct, :] = acc.reshape(cb, ct, D_CONV)

    c = _layernorm(c_sc[...].reshape(n, D_CONV), gcl_ref[...], bcl_ref[...])
    c = c * _sigmoid(c)
    branch_b = jnp.dot(c.astype(bf16), wc_ref[...], preferred_element_type=f32)
    merged = g[:, :D_MODEL] * branch_a + g[:, D_MODEL:] * branch_b
    mix = jnp.dot(merged.astype(bf16), wo_ref[...], preferred_element_type=f32)
    h = _layernorm(ALPHA * x + mix, g1_ref[...], b1_ref[...])
    h_ref[...] = h.reshape(bb, t, D_MODEL)
    w_hi, w_lo = _split2(wr_ref[...])
    w_parts = jnp.concatenate([w_hi, w_lo], axis=1)
    cross = sum(jnp.dot(part, w_parts, preferred_element_type=f32) for part in _split2(h))
    logits = cross[:, :LANES] + cross[:, LANES:] + br_ref[...]
    gates_ref[...] = _route(logits).reshape(bb, t, LANES)


def _tail(x, o, u, ctx, w, bb, t, cb, ct):
    b, s, _ = x.shape
    nt = s // t
    uprev = u if nt > 1 else ctx
    per_tile = t // HALO
    tile = lambda width: pl.BlockSpec((bb, t, width), lambda b_, i: (b_, i, 0))
    full = lambda a: pl.BlockSpec(a.shape, lambda b_, i: (0,) * a.ndim)
    weights = [w["w_dw"], w["b_dw"], w["g_cln"], w["b_cln"], w["w_attn_out"], w["w_conv_out"], w["w_gate"],
               w["w_out"], w["g_ln1"], w["b_ln1"], w["w_route"], w["b_route"]]
    return pl.pallas_call(
        functools.partial(_tail_kernel, cb=cb, ct=ct),
        grid=(b // bb, nt),
        in_specs=[tile(D_MODEL), tile(ATT_WIDTH), tile(D_CONV),
                  pl.BlockSpec((bb, HALO, D_CONV), lambda b_, i: (b_, jnp.maximum(i * per_tile - 1, 0), 0)),
                  pl.BlockSpec((bb, HALO, D_CONV), lambda b_, i: (b_, 0, 0))]
                 + [full(a) for a in weights],
        out_specs=[tile(D_MODEL), tile(LANES)],
        out_shape=[jax.ShapeDtypeStruct((b, s, D_MODEL), f32), jax.ShapeDtypeStruct((b, s, LANES), f32)],
        scratch_shapes=[pltpu.VMEM((bb, HALO + t, D_CONV), f32),
                        pltpu.VMEM((SUBLANES - 1, bb, t + HALO - SUBLANES, D_CONV), f32),
                        pltpu.VMEM((bb, t, D_CONV), f32)],
        compiler_params=_params("parallel", "arbitrary"),
        name="tail",
    )(x, o, u, uprev, ctx, *weights)


MOE_SUB = 256
MOE_CAP = 32
MOE_EPC = 8
MOE_KEY = 4096
_CAP_SHIFT = MOE_CAP.bit_length() - 1


def _moe_kernel(h_ref, r_ref, w1_ref, w3_ref, w2_ref, g2_ref, b2_ref, y_ref,
                acc_sc, hb_sc, key_sc, keyt_sc, nr_sc, *, n_sub):
    c = pl.program_id(1)
    slots = MOE_EPC * MOE_CAP

    @pl.when(c == 0)
    def _():
        acc_sc[...] = jnp.zeros(acc_sc.shape, f32)
        hb_sc[...] = h_ref[...].astype(bf16)
        lane = lax.broadcasted_iota(jnp.int32, (MOE_SUB, LANES), 1).astype(f32)
        ti = lax.broadcasted_iota(jnp.int32, (MOE_SUB, MOE_SUB), 0)
        tj = lax.broadcasted_iota(jnp.int32, (MOE_SUB, MOE_SUB), 1)
        earlier = jnp.where(tj < ti, 1.0, 0.0).astype(bf16)
        most = jnp.zeros((1, LANES), f32)
        for a in range(n_sub):
            r = r_ref[a * MOE_SUB:(a + 1) * MOE_SUB, :]
            e1, e2 = r[:, 0:1], r[:, 1:2]
            oh1 = jnp.where(lane == e1, 1.0, 0.0)
            oh2 = jnp.where(lane == e2, 1.0, 0.0)
            both = oh1 + oh2
            before = jnp.dot(earlier, both.astype(bf16), preferred_element_type=f32)
            key1 = e1 * MOE_KEY + jnp.sum(before * oh1, axis=-1, keepdims=True)
            key2 = e2 * MOE_KEY + jnp.sum(before * oh2, axis=-1, keepdims=True)
            info = jnp.where(lane == 0.0, key1, jnp.where(lane == 1.0, key2, r))
            key_sc[a] = info
            keyt_sc[a] = info.T[0:8, :]
            most = jnp.maximum(most, jnp.sum(both, axis=0, keepdims=True))
        nr_sc[0] = (jnp.max(most).astype(jnp.int32) + (MOE_CAP - 1)) // MOE_CAP

    e0 = c * MOE_EPC
    s_row = lax.broadcasted_iota(jnp.int32, (slots, MOE_SUB), 0)
    s_col = lax.broadcasted_iota(jnp.int32, (MOE_SUB, slots), 1)

    def slot_key(s, rd):
        expert = lax.shift_right_logical(s, _CAP_SHIFT) + e0
        return (expert * MOE_KEY + jnp.bitwise_and(s, MOE_CAP - 1) + rd * MOE_CAP).astype(f32)

    def one_round(rd, carry):
        key_r = slot_key(s_row, rd)
        key_c = slot_key(s_col, rd)
        xs = []
        for a in range(n_sub):
            k1, k2 = keyt_sc[a, 0:1, :], keyt_sc[a, 1:2, :]
            p = jnp.where(key_r == k1, 1.0, jnp.where(key_r == k2, 1.0, 0.0)).astype(bf16)
            hb = hb_sc[a * MOE_SUB:(a + 1) * MOE_SUB, :]
            xs.append(jnp.dot(p, hb, preferred_element_type=f32).astype(bf16))
        ys = [[] for _ in range(n_sub)]
        for e in range(MOE_EPC):
            xe = jnp.concatenate([x[e * MOE_CAP:(e + 1) * MOE_CAP] for x in xs], axis=0)
            a1 = jnp.dot(xe, w1_ref[e], preferred_element_type=f32)
            a3 = jnp.dot(xe, w3_ref[e], preferred_element_type=f32)
            act = (a1 * _sigmoid(a1)) * a3
            ye = jnp.dot(act.astype(bf16), w2_ref[e], preferred_element_type=f32).astype(bf16)
            for a in range(n_sub):
                ys[a].append(ye[a * MOE_CAP:(a + 1) * MOE_CAP])
        for a in range(n_sub):
            info = key_sc[a]
            pw = jnp.where(key_c == info[:, 0:1], info[:, 2:3],
                           jnp.where(key_c == info[:, 1:2], info[:, 3:4], 0.0)).astype(bf16)
            rows = slice(a * MOE_SUB, (a + 1) * MOE_SUB)
            acc_sc[rows, :] += jnp.dot(pw, jnp.concatenate(ys[a], axis=0), preferred_element_type=f32)
        return carry

    lax.fori_loop(0, nr_sc[0], one_round, 0)

    @pl.when(c == pl.num_programs(1) - 1)
    def _():
        y_ref[...] = _layernorm(ALPHA * h_ref[...] + acc_sc[...], g2_ref[...], b2_ref[...])


def _moe(h2d, route, w1, w3, w2, g2, b2, tm):
    n = h2d.shape[0]
    n_sub = tm // MOE_SUB
    expert_block = lambda a: pl.BlockSpec((MOE_EPC,) + a.shape[1:], lambda i, c: (c, 0, 0))
    return pl.pallas_call(
        functools.partial(_moe_kernel, n_sub=n_sub),
        grid=(n // tm, N_EXPERTS // MOE_EPC),
        in_specs=[pl.BlockSpec((tm, D_MODEL), lambda i, c: (i, 0)),
                  pl.BlockSpec((tm, LANES), lambda i, c: (i, 0)),
                  expert_block(w1), expert_block(w3), expert_block(w2),
                  pl.BlockSpec((1, D_MODEL), lambda i, c: (0, 0)),
                  pl.BlockSpec((1, D_MODEL), lambda i, c: (0, 0))],
        out_specs=pl.BlockSpec((tm, D_MODEL), lambda i, c: (i, 0)),
        out_shape=jax.ShapeDtypeStruct((n, D_MODEL), f32),
        scratch_shapes=[pltpu.VMEM((tm, D_MODEL), f32), pltpu.VMEM((tm, D_MODEL), bf16),
                        pltpu.VMEM((n_sub, MOE_SUB, LANES), f32), pltpu.VMEM((n_sub, 8, MOE_SUB), f32),
                        pltpu.SMEM((1,), jnp.int32)],
        compiler_params=_params("parallel", "arbitrary"),
        name="moe",
    )(h2d, route, w1, w3, w2, g2, b2)


def _pack_weights(w_in, b_forget, w_dw, b_dw, g_conv_ln, b_conv_ln, w_attn_out, w_conv_out, w_out,
                  g_ln1, b_ln1, w_group, b_group, w_router, b_router, w1, w3, w2, g_ln2, b_ln2):
    a = ATT_WIDTH
    c_f, c_glu, c_gate = 3 * a, 3 * a + N_HEADS, 3 * a + N_HEADS + 2 * D_CONV
    w_f = jnp.pad(w_in[:, c_f:c_glu], ((0, 0), (0, LANES - N_HEADS)))
    w_pack = jnp.concatenate([w_in[:, :c_f], w_in[:, c_glu:c_gate], w_f], axis=1).astype(bf16)
    pad_route = LANES - N_GROUPS - N_EXPERTS
    row = lambda v: v.reshape(1, -1)
    return dict(
        w_pack=w_pack,
        b_forget=jnp.pad(b_forget, (0, LANES - N_HEADS)).reshape(1, LANES),
        w_gate=w_in[:, c_gate:].astype(bf16),
        w_dw=jnp.broadcast_to(w_dw[:, None, :], (CONV_WIDTH, SUBLANES, D_CONV)),
        b_dw=jnp.broadcast_to(b_dw[None, :], (SUBLANES, D_CONV)),
        g_cln=row(g_conv_ln), b_cln=row(b_conv_ln),
        w_attn_out=w_attn_out.astype(bf16), w_conv_out=w_conv_out.astype(bf16), w_out=w_out.astype(bf16),
        g_ln1=row(g_ln1), b_ln1=row(b_ln1),
        w_route=jnp.pad(jnp.concatenate([w_group, w_router], axis=1), ((0, 0), (0, pad_route))),
        b_route=jnp.pad(jnp.concatenate([b_group, b_router]), (0, pad_route)).reshape(1, LANES),
        w1=w1.astype(bf16), w3=w3.astype(bf16), w2=w2.astype(bf16),
        g_ln2=row(g_ln2), b_ln2=row(b_ln2),
    )


def _finish(x, o, u, ctx, w, bb, t, cb, ct, tm_moe):
    b, s, _ = x.shape
    h, gates = _tail(x, o, u, ctx, w, bb, t, cb, ct)
    y = _moe(h.reshape(b * s, D_MODEL), gates.reshape(b * s, LANES), w["w1"], w["w3"], w["w2"],
             w["g_ln2"], w["b_ln2"], tm_moe)
    return y.reshape(b, s, D_MODEL)


def _layer(xp, xs, cache_k, cache_v, cache_logf, cache_conv, w):
    bp, sp, _ = xp.shape
    bs, ts, _ = xs.shape
    past = cache_k.shape[1]
    tq, tk = ATTN_Q_TILE, TOKEN_TILE

    q, k, kb, v, vt, vb, lf, u = _in_proj(xp.reshape(bp * sp, D_MODEL), w["w_pack"], w["b_forget"], tk, bp)
    logf = lf.transpose(0, 2, 1)
    parts = _cumsum_lanes(lf.reshape(bp * N_HEADS, sp), reverse=False, split_scale=-LOG2E)
    parts = parts.reshape(N_SPLIT, bp, ATT_WIDTH // LANES, 2, sp)
    shp = (bp, sp, ATT_WIDTH)
    o = _attn_prompt(q.reshape(shp), kb.reshape(shp), parts, vt, tq)
    u = u.reshape(bp, sp, D_CONV)
    yp = _finish(xp, o, u, jnp.zeros((bp, HALO, D_CONV), f32), w, 1, TOKEN_TILE, 1, CONV_ROWS, MOE_TILE)
    outs_p = (k.reshape(bp, sp, N_HEADS, HEAD_DIM), v.reshape(bp, sp, N_HEADS, HEAD_DIM), logf,
              u[:, sp - CONV_CTX:])

    q, k, kb, v, _, vb, lf, u = _in_proj(xs.reshape(bs * ts, D_MODEL), w["w_pack"], w["b_forget"], bs * ts, 1)
    lf = lf.reshape(N_HEADS, bs, ts)
    logf = lf.transpose(1, 2, 0)
    lf_t = jnp.pad(lf.transpose(1, 0, 2), ((0, 0), (0, 0), (0, LANES - ts)))
    cq = _cumsum_lanes(lf_t.reshape(bs * N_HEADS, LANES), reverse=False).reshape(bs, N_HEADS, LANES)[:, :, :ts]
    r = _cumsum_lanes(cache_logf.transpose(0, 2, 1).reshape(bs * N_HEADS, past), reverse=True)
    bias_c = jnp.repeat(LOG2E * r.reshape(bs, N_HEADS, past), ts, axis=1)
    causal = jnp.arange(ts)[None, :] <= jnp.arange(ts)[:, None]
    bias_n = jnp.where(causal[None, None], -LOG2E * cq[:, :, None, :], NEG)
    bias_n = jnp.pad(bias_n.reshape(bs, LANES, ts), ((0, 0), (0, 0), (0, LANES - ts)), constant_values=NEG)
    q4 = q.reshape(bs, ts, N_HEADS, HEAD_DIM).transpose(0, 2, 1, 3)
    qbd = (q4[:, :, :, None, :] * jnp.eye(N_HEADS, dtype=bf16)[None, :, None, :, None]).reshape(bs, LANES, ATT_WIDTH)
    new_t = lambda a: jnp.pad(a.reshape(bs, ts, ATT_WIDTH).transpose(0, 2, 1), ((0, 0), (0, 0), (0, LANES - ts)))
    cache_t = lambda a: a.transpose(0, 2, 3, 1).reshape(bs, ATT_WIDTH, past)
    o = _attn_sample(qbd, cache_t(cache_k), cache_t(cache_v), new_t(kb), new_t(vb), bias_c, bias_n, ts)
    u = u.reshape(bs, ts, D_CONV)
    ctx = jnp.pad(cache_conv, ((0, 0), (HALO_PAD, 0), (0, 0)))
    ys = _finish(xs, o, u, ctx, w, bs, ts, 2, ts, bs * ts)
    u_ext = jnp.concatenate([cache_conv, u], axis=1)
    outs_s = (k.reshape(bs, ts, N_HEADS, HEAD_DIM), v.reshape(bs, ts, N_HEADS, HEAD_DIM), logf,
              u_ext[:, u_ext.shape[1] - CONV_CTX:])
    return yp, ys, outs_p, outs_s


def kernel(x_prompt, x_sample, cache_k, cache_v, cache_logf, cache_conv, w_in, b_forget, w_dw, b_dw,
           g_conv_ln, b_conv_ln, w_attn_out, w_conv_out, w_out, g_ln1, b_ln1, w_group, b_group,
           w_router, b_router, w1, w3, w2, g_ln2, b_ln2):
    xp, xs = x_prompt, x_sample
    per_layer_p, per_layer_s = [], []
    for l in range(DEPTH):
        w = _pack_weights(w_in[l], b_forget[l], w_dw[l], b_dw[l], g_conv_ln[l], b_conv_ln[l], w_attn_out[l],
                          w_conv_out[l], w_out[l], g_ln1[l], b_ln1[l], w_group[l], b_group[l], w_router[l],
                          b_router[l], w1[l], w3[l], w2[l], g_ln2[l], b_ln2[l])
        xp, xs, outs_p, outs_s = _layer(xp, xs, cache_k[l], cache_v[l], cache_logf[l], cache_conv[l], w)
        per_layer_p.append(outs_p)
        per_layer_s.append(outs_s)
    stack = lambda outs, j: jnp.stack([o[j] for o in outs])
    return (xp, xs,
            stack(per_layer_p, 0), stack(per_layer_p, 1), stack(per_layer_p, 2), stack(per_layer_p, 3),
            stack(per_layer_s, 0), stack(per_layer_s, 1), stack(per_layer_s, 2), stack(per_layer_s, 3))
```

```python
import functools

import jax
import jax.numpy as jnp
from jax import lax
from jax.experimental import pallas as pl
from jax.experimental.pallas import tpu as pltpu

f32 = jnp.float32
bf16 = jnp.bfloat16

D_MODEL = 1024
N_HEADS = 8
HEAD_DIM = 64
ATT_WIDTH = N_HEADS * HEAD_DIM
D_CONV = 512
CONV_WIDTH = 31
CONV_CTX = CONV_WIDTH - 1
N_GROUPS = 4
EXPERTS_PER_GROUP = 8
N_EXPERTS = N_GROUPS * EXPERTS_PER_GROUP
D_EXPERT = 256
DEPTH = 1
ALPHA = (2 * DEPTH) ** 0.25
LN_EPS = 1e-5
NEG = -1e30
SCALE = HEAD_DIM ** -0.5
LOG2E = 1.4426950408889634

N_SPLIT = 3
LANES = 128
SUBLANES = 8
HALO = 32
HALO_PAD = HALO - CONV_CTX
EXPERT_LANE0 = N_GROUPS
VMEM_LIMIT = 56 * 1024 * 1024

TOKEN_TILE = 512
ATTN_Q_TILE = 1024
CONV_ROWS = 32
MOE_TILE = 1024

_C_Q, _C_K, _C_V, _C_GA, _C_GB, _C_F, _C_END = 0, 512, 1024, 1536, 2048, 2560, 2688


def _sigmoid(x):
    return 1.0 / (1.0 + jnp.exp(-x))


def _layernorm(x, g, b):
    mu = jnp.mean(x, axis=-1, keepdims=True)
    xc = x - mu
    var = jnp.mean(xc * xc, axis=-1, keepdims=True)
    return xc * lax.rsqrt(var + LN_EPS) * g + b


def _params(*sem):
    return pltpu.CompilerParams(dimension_semantics=sem, vmem_limit_bytes=VMEM_LIMIT)


def _store_head_major(ref, val):
    tokens = val.shape[0]
    for h in range(N_HEADS):
        ref[pl.ds(h, tokens, stride=N_HEADS), :] = val[:, h * HEAD_DIM:(h + 1) * HEAD_DIM]


def _in_proj_kernel(x_ref, w_ref, bf_ref, q_ref, k_ref, kb_ref, v_ref, vt_ref, vb_ref, lf_ref, u_ref):
    tm = x_ref.shape[0]
    xb = x_ref[...].astype(bf16)

    def proj(lo, hi):
        return jnp.dot(xb, w_ref[:, lo:hi], preferred_element_type=f32)

    q_ref[...] = (proj(_C_Q, _C_K) * (SCALE * LOG2E)).astype(bf16)
    k = proj(_C_K, _C_V)
    _store_head_major(k_ref, k)
    kb_ref[...] = k.astype(bf16)
    v = proj(_C_V, _C_GA)
    _store_head_major(v_ref, v)
    vb_ref[...] = v.astype(bf16)
    vt_ref[0, :, 0] = v.T.astype(bf16).reshape(ATT_WIDTH // LANES, LANES, tm)
    u_ref[...] = proj(_C_GA, _C_GB) * _sigmoid(proj(_C_GB, _C_F))
    z = proj(_C_F, _C_END) + bf_ref[...]
    log_f = jnp.minimum(z, 0.0) - jnp.log(1.0 + jnp.exp(-jnp.abs(z)))
    lf_ref[0] = log_f.T[0:N_HEADS, :]


def _in_proj(x2d, w_pack, bf_pad, tm, groups):
    n = x2d.shape[0]
    cols = n // groups
    tpg = cols // tm
    row = lambda width: pl.BlockSpec((tm, width), lambda i: (i, 0))
    heads = pl.BlockSpec((tm * N_HEADS, HEAD_DIM), lambda i: (i, 0))
    hp = ATT_WIDTH // LANES
    return pl.pallas_call(
        _in_proj_kernel,
        grid=(n // tm,),
        in_specs=[row(D_MODEL),
                  pl.BlockSpec((D_MODEL, _C_END), lambda i: (0, 0)),
                  pl.BlockSpec((1, LANES), lambda i: (0, 0))],
        out_specs=[row(ATT_WIDTH), heads, row(ATT_WIDTH), heads,
                   pl.BlockSpec((1, hp, 1, LANES, tm), lambda i: (i // tpg, 0, i % tpg, 0, 0)),
                   row(ATT_WIDTH),
                   pl.BlockSpec((1, N_HEADS, tm), lambda i: (i // tpg, 0, i % tpg)),
                   row(D_CONV)],
        out_shape=[jax.ShapeDtypeStruct((n, ATT_WIDTH), bf16),
                   jax.ShapeDtypeStruct((n * N_HEADS, HEAD_DIM), f32),
                   jax.ShapeDtypeStruct((n, ATT_WIDTH), bf16),
                   jax.ShapeDtypeStruct((n * N_HEADS, HEAD_DIM), f32),
                   jax.ShapeDtypeStruct((groups, hp, tpg, LANES, tm), bf16),
                   jax.ShapeDtypeStruct((n, ATT_WIDTH), bf16),
                   jax.ShapeDtypeStruct((groups, N_HEADS, cols), f32),
                   jax.ShapeDtypeStruct((n, D_CONV), f32)],
        compiler_params=_params("parallel"),
        name="in_proj",
    )(x2d, w_pack, bf_pad)


def _split2(x):
    hi = x.astype(bf16)
    return hi, (x - hi.astype(f32)).astype(bf16)


def _split3(x):
    hi = x.astype(bf16)
    r1 = x - hi.astype(f32)
    mid = r1.astype(bf16)
    lo = (r1 - mid.astype(f32)).astype(bf16)
    return hi, mid, lo


def _cumsum_kernel(x_ref, o_ref, *, reverse, split_scale):
    nc, rows, _ = x_ref.shape
    r = lax.broadcasted_iota(jnp.int32, (LANES, LANES), 0)
    c = lax.broadcasted_iota(jnp.int32, (LANES, LANES), 1)
    tri = jnp.where((r > c) if reverse else (r <= c), 1.0, 0.0).astype(bf16)
    carry = jnp.zeros((rows, 1), f32)
    for j in (range(nc - 1, -1, -1) if reverse else range(nc)):
        xj = x_ref[j]
        y = carry
        for part in _split3(xj):
            y = y + jnp.dot(part, tri, preferred_element_type=f32)
        if split_scale is None:
            o_ref[j] = y
        else:
            for n, part in enumerate(_split3(split_scale * y)):
                o_ref[n, j] = part.astype(f32)
        carry = carry + jnp.sum(xj, axis=-1, keepdims=True)


def _cumsum_lanes(x, reverse, split_scale=None):
    rows, s = x.shape
    nc = s // LANES
    xc = x.reshape(rows, nc, LANES).transpose(1, 0, 2)
    shape = (nc, rows, LANES) if split_scale is None else (N_SPLIT, nc, rows, LANES)
    out = pl.pallas_call(
        functools.partial(_cumsum_kernel, reverse=reverse, split_scale=split_scale),
        out_shape=jax.ShapeDtypeStruct(shape, f32),
        name="cumsum_rev" if reverse else "cumsum_fwd",
    )(xc)
    if split_scale is None:
        return out.transpose(1, 0, 2).reshape(rows, s)
    return out.transpose(0, 2, 1, 3).reshape(N_SPLIT, rows, s)


def _attn_kernel(q_ref, k_ref, p_ref, vt_ref, o_ref, qx_sc, acc_sc, s0_sc, s1_sc, e_sc, *, tq, tk):
    i = pl.program_id(2)
    per_q = tq // tk

    @pl.when(i == 0)
    def _():
        row_k = lax.broadcasted_iota(jnp.int32, (LANES, tk), 0)
        for c in range(e_sc.shape[0] // tk):
            chans = jnp.zeros((LANES, tk), f32)
            for hh in range(2):
                for n in range(N_SPLIT):
                    part = p_ref[n, 0, 0, hh:hh + 1, c * tk:(c + 1) * tk]
                    chans = jnp.where(row_k == N_SPLIT * hh + n, part, chans)
            e_sc[c * tk:(c + 1) * tk, :] = chans.T.astype(bf16)

    qt = q_ref[0].astype(f32).T
    row = lax.broadcasted_iota(jnp.int32, (LANES, tq), 0)
    for h in range(2):
        qx_sc[h, 0:LANES, :] = jnp.where(row // HEAD_DIM == h, qt, 0.0).astype(bf16)
        qx_sc[h, LANES:2 * LANES, :] = jnp.where(row // N_SPLIT == h, 1.0, 0.0).astype(bf16)
    acc_sc[...] = jnp.zeros(acc_sc.shape, f32)
    ones_rows = jnp.ones((16, tk), bf16)

    s_sc = (s0_sc, s1_sc)

    def scores(j, h, lo=0):
        start = pl.multiple_of(j * tk, tk)
        kx = jnp.concatenate([k_ref[0, pl.ds(start, tk), :], e_sc[pl.ds(start, tk), :]], axis=1)
        s_sc[h][:, lo:] = jnp.dot(kx, qx_sc[h, :, lo:], preferred_element_type=f32)

    def absorb(j, h, stats, lo=0, masked=False):
        m_all, l_all = stats
        m_prev, l_prev = m_all[:, lo:], l_all[:, lo:]
        s = s_sc[h][:, lo:]
        if masked:
            key = lax.broadcasted_iota(jnp.int32, s.shape, 0)
            qry = lax.broadcasted_iota(jnp.int32, s.shape, 1)
            s = jnp.where(key <= qry, s, NEG)
        m_new = jnp.maximum(m_prev, jnp.max(s, axis=0, keepdims=True))
        alpha = jnp.exp2(m_prev - m_new)
        p = jnp.exp2(s - m_new).astype(bf16)
        vx = jnp.concatenate([vt_ref[0, 0, j, h * HEAD_DIM:(h + 1) * HEAD_DIM, :], ones_rows], axis=0)
        pv = jnp.dot(vx, p, preferred_element_type=f32)
        acc_sc[h, :, lo:] = alpha * acc_sc[h, :, lo:] + pv[0:HEAD_DIM, :]
        l_new = alpha * l_prev + pv[HEAD_DIM:HEAD_DIM + 1, :]
        if lo:
            m_new = jnp.concatenate([m_all[:, :lo], m_new], axis=1)
            l_new = jnp.concatenate([l_all[:, :lo], l_new], axis=1)
        return m_new, l_new

    def body(j, stats):
        scores(j, 1)
        st0 = absorb(j, 0, stats[0])
        scores(j + 1, 0)
        st1 = absorb(j, 1, stats[1])
        return st0, st1

    init = tuple((jnp.full((1, tq), NEG, f32), jnp.zeros((1, tq), f32)) for _ in range(2))
    scores(0, 0)
    stats = lax.fori_loop(0, (per_q // 2) * i, lambda p, st: body(2 * p + 1, body(2 * p, st)), init)
    st0, st1 = stats
    j0 = per_q * i
    for d in range(per_q):
        scores(j0 + d, 1, d * tk)
        st0 = absorb(j0 + d, 0, st0, d * tk, masked=True)
        if d + 1 < per_q:
            scores(j0 + d + 1, 0, (d + 1) * tk)
        st1 = absorb(j0 + d, 1, st1, d * tk, masked=True)
    l0, l1 = st0[1], st1[1]
    ot = jnp.concatenate([acc_sc[0] / l0, acc_sc[1] / l1], axis=0)
    o_ref[0] = ot.T.astype(bf16)


def _attn_prompt(q, kb, parts, vt, tq):
    b, s, _ = q.shape
    hp = ATT_WIDTH // LANES
    nq = s // tq
    nk, tk = vt.shape[2], vt.shape[4]
    assert tq % (2 * tk) == 0
    return pl.pallas_call(
        functools.partial(_attn_kernel, tq=tq, tk=tk),
        grid=(b, hp, nq),
        in_specs=[pl.BlockSpec((1, tq, LANES), lambda b_, h_, i: (b_, i, h_)),
                  pl.BlockSpec((1, s, LANES), lambda b_, h_, i: (b_, 0, h_)),
                  pl.BlockSpec((N_SPLIT, 1, 1, 2, s), lambda b_, h_, i: (0, b_, h_, 0, 0)),
                  pl.BlockSpec((1, 1, nk, LANES, tk), lambda b_, h_, i: (b_, h_, 0, 0, 0))],
        out_specs=pl.BlockSpec((1, tq, LANES), lambda b_, h_, i: (b_, i, h_)),
        out_shape=jax.ShapeDtypeStruct((b, s, ATT_WIDTH), bf16),
        scratch_shapes=[pltpu.VMEM((2, 2 * LANES, tq), bf16), pltpu.VMEM((2, HEAD_DIM, tq), f32),
                        pltpu.VMEM((tk, tq), f32), pltpu.VMEM((tk, tq), f32), pltpu.VMEM((s, LANES), bf16)],
        compiler_params=_params("parallel", "parallel", "arbitrary"),
        name="attn_prompt",
    )(q, kb, parts, vt)


def _attn_sample_kernel(q_ref, ck_ref, cv_ref, kn_ref, vn_ref, dc_ref, dn_ref, o_ref, *, t_new):
    rows_head = lambda shape: lax.shift_right_logical(lax.broadcasted_iota(jnp.int32, shape, 0), t_new.bit_length() - 1)
    per_query = lambda a: jnp.broadcast_to(a[:, None, :], (N_HEADS, t_new, a.shape[-1])).reshape(LANES, a.shape[-1])
    shape = (LANES, ATT_WIDTH)
    col_head = lax.shift_right_logical(lax.broadcasted_iota(jnp.int32, shape, 1), HEAD_DIM.bit_length() - 1)
    q_rows = jnp.tile(q_ref[0].astype(f32), (N_HEADS, 1))
    qbd = jnp.where(rows_head(shape) == col_head, q_rows, 0.0).astype(bf16)
    query = jnp.bitwise_and(lax.broadcasted_iota(jnp.int32, (LANES, LANES), 0), t_new - 1)
    new_key = lax.broadcasted_iota(jnp.int32, (LANES, LANES), 1)
    bias_n = jnp.where(new_key <= query, per_query(dn_ref[0]), NEG)
    s_c = jnp.dot(qbd, ck_ref[0].astype(bf16), preferred_element_type=f32) + per_query(dc_ref[0])
    s_n = jnp.dot(qbd, kn_ref[0], preferred_element_type=f32) + bias_n
    m = jnp.maximum(jnp.max(s_c, axis=-1, keepdims=True), jnp.max(s_n, axis=-1, keepdims=True))
    p_c = jnp.exp2(s_c - m)
    p_n = jnp.exp2(s_n - m)
    inv = 1.0 / (jnp.sum(p_c, axis=-1, keepdims=True) + jnp.sum(p_n, axis=-1, keepdims=True))
    p_c = (p_c * inv).astype(bf16)
    p_n = (p_n * inv).astype(bf16)
    over_positions = (((1,), (1,)), ((), ()))
    o_all = (lax.dot_general(cv_ref[0].astype(bf16), p_c, over_positions, preferred_element_type=f32)
             + lax.dot_general(vn_ref[0], p_n, over_positions, preferred_element_type=f32))
    shape = (ATT_WIDTH, LANES)
    row_head = lax.shift_right_logical(lax.broadcasted_iota(jnp.int32, shape, 0), HEAD_DIM.bit_length() - 1)
    col_head = lax.shift_right_logical(lax.broadcasted_iota(jnp.int32, shape, 1), t_new.bit_length() - 1)
    own = jnp.where(row_head == col_head, o_all, 0.0).astype(bf16)
    q_of_row = jnp.bitwise_and(lax.broadcasted_iota(jnp.int32, (LANES, LANES), 0), t_new - 1)
    pick = jnp.where(q_of_row == lax.broadcasted_iota(jnp.int32, (LANES, LANES), 1), 1.0, 0.0).astype(bf16)
    o_t = jnp.dot(own, pick, preferred_element_type=f32)
    o_ref[0] = o_t.T[0:t_new, :].astype(bf16)


def _attn_sample(q, ck, cv, kn, vn, decay_c, decay_n, t_new):
    b, _, p = ck.shape
    assert N_HEADS * t_new == LANES
    blk = lambda r, c: pl.BlockSpec((1, r, c), lambda i: (i, 0, 0))
    return pl.pallas_call(
        functools.partial(_attn_sample_kernel, t_new=t_new),
        grid=(b,),
        in_specs=[blk(t_new, ATT_WIDTH), blk(ATT_WIDTH, p), blk(ATT_WIDTH, p), blk(ATT_WIDTH, LANES),
                  blk(ATT_WIDTH, LANES), blk(N_HEADS, p), blk(N_HEADS, LANES)],
        out_specs=blk(t_new, ATT_WIDTH),
        out_shape=jax.ShapeDtypeStruct((b, t_new, ATT_WIDTH), bf16),
        compiler_params=_params("parallel"),
        name="attn_sample",
    )(q, ck, cv, kn, vn, decay_c, decay_n)


def _route(logits):
    lane = lax.broadcasted_iota(jnp.int32, logits.shape, 1).astype(f32)
    ninf = -jnp.inf
    gl = jnp.where(lane < N_GROUPS, logits, ninf)
    gmax = jnp.max(gl, axis=-1, keepdims=True)
    grp = jnp.min(jnp.where(gl == gmax, lane, float(LANES)), axis=-1, keepdims=True)
    p_sel = 1.0 / jnp.sum(jnp.exp(gl - gmax), axis=-1, keepdims=True)
    e_lo = EXPERT_LANE0 + grp * EXPERTS_PER_GROUP
    el = jnp.where(lane >= e_lo, jnp.where(lane < e_lo + EXPERTS_PER_GROUP, logits, ninf), ninf)
    t1 = jnp.max(el, axis=-1, keepdims=True)
    i1 = jnp.min(jnp.where(el == t1, lane, float(LANES)), axis=-1, keepdims=True)
    el2 = jnp.where(lane == i1, ninf, el)
    t2 = jnp.max(el2, axis=-1, keepdims=True)
    i2 = jnp.min(jnp.where(el2 == t2, lane, float(LANES)), axis=-1, keepdims=True)
    e2 = jnp.exp(t2 - t1)
    w1 = p_sel / (1.0 + e2)
    w2 = p_sel * e2 / (1.0 + e2)
    picks = (i1 - EXPERT_LANE0, i2 - EXPERT_LANE0, w1, w2)
    out = jnp.zeros(logits.shape, f32)
    for n, val in enumerate(picks):
        out = jnp.where(lane == float(n), val, out)
    return out


def _tail_kernel(x_ref, o_ref, u_ref, uprev_ref, ctx_ref, wdw_ref, bdw_ref, gcl_ref, bcl_ref,
                 wa_ref, wc_ref, wg_ref, wo_ref, g1_ref, b1_ref, wr_ref, br_ref,
                 h_ref, gates_ref, uext_sc, shift_sc, c_sc, *, cb, ct):
    i = pl.program_id(1)
    bb, t, _ = x_ref.shape
    n = bb * t

    x = x_ref[...].reshape(n, D_MODEL)
    g = _sigmoid(jnp.dot(x.astype(bf16), wg_ref[...], preferred_element_type=f32))
    branch_a = jnp.dot(o_ref[...].reshape(n, ATT_WIDTH), wa_ref[...], preferred_element_type=f32)

    uext_sc[:, 0:HALO, :] = jnp.where(i == 0, ctx_ref[...], uprev_ref[...])
    uext_sc[:, HALO:HALO + t, :] = u_ref[...]
    span = t + HALO - SUBLANES
    for r in range(1, SUBLANES):
        shift_sc[r - 1] = uext_sc[:, r:r + span, :]
    for b0 in range(0, bb, cb):
        for t0 in range(0, t, ct):
            groups = cb * ct // SUBLANES
            acc = jnp.zeros((groups, SUBLANES, D_CONV), f32) + bdw_ref[...]
            for k in range(CONV_WIDTH):
                a, r = divmod(HALO_PAD + k, SUBLANES)
                r0 = t0 + a * SUBLANES
                src = uext_sc[b0:b0 + cb, r0:r0 + ct, :] if r == 0 else shift_sc[r - 1, b0:b0 + cb, r0:r0 + ct, :]
                acc = acc + wdw_ref[k] * src.reshape(groups, SUBLANES, D_CONV)
            c_sc[b0:b0 + cb, t0:t0 + ct, :] = acc.reshape(cb, ct, D_CONV)

    c = _layernorm(c_sc[...].reshape(n, D_CONV), gcl_ref[...], bcl_ref[...])
    c = c * _sigmoid(c)
    branch_b = jnp.dot(c.astype(bf16), wc_ref[...], preferred_element_type=f32)
    merged = g[:, :D_MODEL] * branch_a + g[:, D_MODEL:] * branch_b
    mix = jnp.dot(merged.astype(bf16), wo_ref[...], preferred_element_type=f32)
    h = _layernorm(ALPHA * x + mix, g1_ref[...], b1_ref[...])
    h_ref[...] = h.reshape(bb, t, D_MODEL)
    w_hi, w_lo = _split2(wr_ref[...])
    w_parts = jnp.concatenate([w_hi, w_lo], axis=1)
    cross = sum(jnp.dot(part, w_parts, preferred_element_type=f32) for part in _split2(h))
    logits = cross[:, :LANES] + cross[:, LANES:] + br_ref[...]
    gates_ref[...] = _route(logits).reshape(bb, t, LANES)


def _tail(x, o, u, ctx, w, bb, t, cb, ct):
    b, s, _ = x.shape
    nt = s // t
    uprev = u if nt > 1 else ctx
    per_tile = t // HALO
    tile = lambda width: pl.BlockSpec((bb, t, width), lambda b_, i: (b_, i, 0))
    full = lambda a: pl.BlockSpec(a.shape, lambda b_, i: (0,) * a.ndim)
    weights = [w["w_dw"], w["b_dw"], w["g_cln"], w["b_cln"], w["w_attn_out"], w["w_conv_out"], w["w_gate"],
               w["w_out"], w["g_ln1"], w["b_ln1"], w["w_route"], w["b_route"]]
    return pl.pallas_call(
        functools.partial(_tail_kernel, cb=cb, ct=ct),
        grid=(b // bb, nt),
        in_specs=[tile(D_MODEL), tile(ATT_WIDTH), tile(D_CONV),
                  pl.BlockSpec((bb, HALO, D_CONV), lambda b_, i: (b_, jnp.maximum(i * per_tile - 1, 0), 0)),
                  pl.BlockSpec((bb, HALO, D_CONV), lambda b_, i: (b_, 0, 0))]
                 + [full(a) for a in weights],
        out_specs=[tile(D_MODEL), tile(LANES)],
        out_shape=[jax.ShapeDtypeStruct((b, s, D_MODEL), f32), jax.ShapeDtypeStruct((b, s, LANES), f32)],
        scratch_shapes=[pltpu.VMEM((bb, HALO + t, D_CONV), f32),
                        pltpu.VMEM((SUBLANES - 1, bb, t + HALO - SUBLANES, D_CONV), f32),
                        pltpu.VMEM((bb, t, D_CONV), f32)],
        compiler_params=_params("parallel", "arbitrary"),
        name="tail",
    )(x, o, u, uprev, ctx, *weights)


MOE_SUB = 256
MOE_CAP = 32
MOE_EPC = 8
MOE_KEY = 4096
_CAP_SHIFT = MOE_CAP.bit_length() - 1


def _moe_kernel(h_ref, r_ref, w1_ref, w3_ref, w2_ref, g2_ref, b2_ref, y_ref,
                acc_sc, hb_sc, key_sc, keyt_sc, nr_sc, *, n_sub):
    c = pl.program_id(1)
    slots = MOE_EPC * MOE_CAP

    @pl.when(c == 0)
    def _():
        acc_sc[...] = jnp.zeros(acc_sc.shape, f32)
        hb_sc[...] = h_ref[...].astype(bf16)
        lane = lax.broadcasted_iota(jnp.int32, (MOE_SUB, LANES), 1).astype(f32)
        ti = lax.broadcasted_iota(jnp.int32, (MOE_SUB, MOE_SUB), 0)
        tj = lax.broadcasted_iota(jnp.int32, (MOE_SUB, MOE_SUB), 1)
        earlier = jnp.where(tj < ti, 1.0, 0.0).astype(bf16)
        li = lax.broadcasted_iota(jnp.int32, (LANES, LANES), 0)
        from_lane = [jnp.where(li == n, 1.0, 0.0).astype(bf16) for n in range(2)]
        lane_sum = jnp.ones((LANES, LANES), bf16)
        spread = lambda x, m: jnp.dot(x.astype(bf16), m, preferred_element_type=f32)
        most = jnp.zeros((1, LANES), f32)
        for a in range(n_sub):
            r = r_ref[a * MOE_SUB:(a + 1) * MOE_SUB, :]
            picks = jnp.where(lane < 2.0, r, 0.0)
            e1, e2 = spread(picks, from_lane[0]), spread(picks, from_lane[1])
            oh1 = jnp.where(lane == e1, 1.0, 0.0)
            oh2 = jnp.where(lane == e2, 1.0, 0.0)
            both = oh1 + oh2
            before = jnp.dot(earlier, both.astype(bf16), preferred_element_type=f32)
            key1 = e1 * MOE_KEY + spread(before * oh1, lane_sum)
            key2 = e2 * MOE_KEY + spread(before * oh2, lane_sum)
            info = jnp.where(lane == 0.0, key1, jnp.where(lane == 1.0, key2, r))
            key_sc[a] = info
            keyt_sc[a] = info.T[0:8, :]
            most = jnp.maximum(most, jnp.sum(both, axis=0, keepdims=True))
        nr_sc[0] = (jnp.max(most).astype(jnp.int32) + (MOE_CAP - 1)) // MOE_CAP

    e0 = c * MOE_EPC
    s_row = lax.broadcasted_iota(jnp.int32, (slots, MOE_SUB), 0)
    s_col = lax.broadcasted_iota(jnp.int32, (MOE_SUB, slots), 1)

    def slot_key(s, rd):
        expert = lax.shift_right_logical(s, _CAP_SHIFT) + e0
        return (expert * MOE_KEY + jnp.bitwise_and(s, MOE_CAP - 1) + rd * MOE_CAP).astype(f32)

    def one_round(rd, carry):
        key_r = slot_key(s_row, rd)
        key_c = slot_key(s_col, rd)
        xs = []
        for a in range(n_sub):
            k1, k2 = keyt_sc[a, 0:1, :], keyt_sc[a, 1:2, :]
            p = jnp.where(key_r == k1, 1.0, jnp.where(key_r == k2, 1.0, 0.0)).astype(bf16)
            hb = hb_sc[a * MOE_SUB:(a + 1) * MOE_SUB, :]
            xs.append(jnp.dot(p, hb, preferred_element_type=f32).astype(bf16))
        ys = [[] for _ in range(n_sub)]
        for e in range(MOE_EPC):
            xe = jnp.concatenate([x[e * MOE_CAP:(e + 1) * MOE_CAP] for x in xs], axis=0)
            a1 = jnp.dot(xe, w1_ref[e], preferred_element_type=f32)
            a3 = jnp.dot(xe, w3_ref[e], preferred_element_type=f32)
            act = (a1 * _sigmoid(a1)) * a3
            ye = jnp.dot(act.astype(bf16), w2_ref[e], preferred_element_type=f32).astype(bf16)
            for a in range(n_sub):
                ys[a].append(ye[a * MOE_CAP:(a + 1) * MOE_CAP])
        for a in range(n_sub):
            info = key_sc[a]
            pw = jnp.where(key_c == info[:, 0:1], info[:, 2:3],
                           jnp.where(key_c == info[:, 1:2], info[:, 3:4], 0.0)).astype(bf16)
            rows = slice(a * MOE_SUB, (a + 1) * MOE_SUB)
            acc_sc[rows, :] += jnp.dot(pw, jnp.concatenate(ys[a], axis=0), preferred_element_type=f32)
        return carry

    lax.fori_loop(0, nr_sc[0], one_round, 0)

    @pl.when(c == pl.num_programs(1) - 1)
    def _():
        y_ref[...] = _layernorm(ALPHA * h_ref[...] + acc_sc[...], g2_ref[...], b2_ref[...])


def _moe(h2d, route, w1, w3, w2, g2, b2, tm):
    n = h2d.shape[0]
    n_sub = tm // MOE_SUB
    expert_block = lambda a: pl.BlockSpec((MOE_EPC,) + a.shape[1:], lambda i, c: (c, 0, 0))
    return pl.pallas_call(
        functools.partial(_moe_kernel, n_sub=n_sub),
        grid=(n // tm, N_EXPERTS // MOE_EPC),
        in_specs=[pl.BlockSpec((tm, D_MODEL), lambda i, c: (i, 0)),
                  pl.BlockSpec((tm, LANES), lambda i, c: (i, 0)),
                  expert_block(w1), expert_block(w3), expert_block(w2),
                  pl.BlockSpec((1, D_MODEL), lambda i, c: (0, 0)),
                  pl.BlockSpec((1, D_MODEL), lambda i, c: (0, 0))],
        out_specs=pl.BlockSpec((tm, D_MODEL), lambda i, c: (i, 0)),
        out_shape=jax.ShapeDtypeStruct((n, D_MODEL), f32),
        scratch_shapes=[pltpu.VMEM((tm, D_MODEL), f32), pltpu.VMEM((tm, D_MODEL), bf16),
                        pltpu.VMEM((n_sub, MOE_SUB, LANES), f32), pltpu.VMEM((n_sub, 8, MOE_SUB), f32),
                        pltpu.SMEM((1,), jnp.int32)],
        compiler_params=_params("parallel", "arbitrary"),
        name="moe",
    )(h2d, route, w1, w3, w2, g2, b2)


def _pack_weights(w_in, b_forget, w_dw, b_dw, g_conv_ln, b_conv_ln, w_attn_out, w_conv_out, w_out,
                  g_ln1, b_ln1, w_group, b_group, w_router, b_router, w1, w3, w2, g_ln2, b_ln2):
    a = ATT_WIDTH
    c_f, c_glu, c_gate = 3 * a, 3 * a + N_HEADS, 3 * a + N_HEADS + 2 * D_CONV
    w_f = jnp.pad(w_in[:, c_f:c_glu], ((0, 0), (0, LANES - N_HEADS)))
    w_pack = jnp.concatenate([w_in[:, :c_f], w_in[:, c_glu:c_gate], w_f], axis=1).astype(bf16)
    pad_route = LANES - N_GROUPS - N_EXPERTS
    row = lambda v: v.reshape(1, -1)
    return dict(
        w_pack=w_pack,
        b_forget=jnp.pad(b_forget, (0, LANES - N_HEADS)).reshape(1, LANES),
        w_gate=w_in[:, c_gate:].astype(bf16),
        w_dw=jnp.broadcast_to(w_dw[:, None, :], (CONV_WIDTH, SUBLANES, D_CONV)),
        b_dw=jnp.broadcast_to(b_dw[None, :], (SUBLANES, D_CONV)),
        g_cln=row(g_conv_ln), b_cln=row(b_conv_ln),
        w_attn_out=w_attn_out.astype(bf16), w_conv_out=w_conv_out.astype(bf16), w_out=w_out.astype(bf16),
        g_ln1=row(g_ln1), b_ln1=row(b_ln1),
        w_route=jnp.pad(jnp.concatenate([w_group, w_router], axis=1), ((0, 0), (0, pad_route))),
        b_route=jnp.pad(jnp.concatenate([b_group, b_router]), (0, pad_route)).reshape(1, LANES),
        w1=w1.astype(bf16), w3=w3.astype(bf16), w2=w2.astype(bf16),
        g_ln2=row(g_ln2), b_ln2=row(b_ln2),
    )


def _finish(x, o, u, ctx, w, bb, t, cb, ct, tm_moe):
    b, s, _ = x.shape
    h, gates = _tail(x, o, u, ctx, w, bb, t, cb, ct)
    y = _moe(h.reshape(b * s, D_MODEL), gates.reshape(b * s, LANES), w["w1"], w["w3"], w["w2"],
             w["g_ln2"], w["b_ln2"], tm_moe)
    return y.reshape(b, s, D_MODEL)


def _layer(xp, xs, cache_k, cache_v, cache_logf, cache_conv, w):
    bp, sp, _ = xp.shape
    bs, ts, _ = xs.shape
    past = cache_k.shape[1]
    tq, tk = ATTN_Q_TILE, TOKEN_TILE

    q, k, kb, v, vt, vb, lf, u = _in_proj(xp.reshape(bp * sp, D_MODEL), w["w_pack"], w["b_forget"], tk, bp)
    logf = lf.transpose(0, 2, 1)
    parts = _cumsum_lanes(lf.reshape(bp * N_HEADS, sp), reverse=False, split_scale=-LOG2E)
    parts = parts.reshape(N_SPLIT, bp, ATT_WIDTH // LANES, 2, sp)
    shp = (bp, sp, ATT_WIDTH)
    o = _attn_prompt(q.reshape(shp), kb.reshape(shp), parts, vt, tq)
    u = u.reshape(bp, sp, D_CONV)
    yp = _finish(xp, o, u, jnp.zeros((bp, HALO, D_CONV), f32), w, 1, TOKEN_TILE, 1, CONV_ROWS, MOE_TILE)
    outs_p = (k.reshape(bp, sp, N_HEADS, HEAD_DIM), v.reshape(bp, sp, N_HEADS, HEAD_DIM), logf,
              u[:, sp - CONV_CTX:])

    q, k, kb, v, _, vb, lf, u = _in_proj(xs.reshape(bs * ts, D_MODEL), w["w_pack"], w["b_forget"], bs * ts, 1)
    lf = lf.reshape(N_HEADS, bs, ts)
    logf = lf.transpose(1, 2, 0)
    lf_t = jnp.pad(lf.transpose(1, 0, 2), ((0, 0), (0, 0), (0, LANES - ts)))
    cq = _cumsum_lanes(lf_t.reshape(bs * N_HEADS, LANES), reverse=False).reshape(bs, N_HEADS, LANES)
    r = _cumsum_lanes(cache_logf.transpose(0, 2, 1).reshape(bs * N_HEADS, past), reverse=True)
    decay_c = LOG2E * r.reshape(bs, N_HEADS, past)
    decay_n = -LOG2E * cq
    new_t = lambda a: jnp.pad(a.reshape(bs, ts, ATT_WIDTH).transpose(0, 2, 1), ((0, 0), (0, 0), (0, LANES - ts)))
    cache_t = lambda a: a.transpose(0, 2, 3, 1).reshape(bs, ATT_WIDTH, past)
    o = _attn_sample(q.reshape(bs, ts, ATT_WIDTH), cache_t(cache_k), cache_t(cache_v), new_t(kb), new_t(vb),
                     decay_c, decay_n, ts)
    u = u.reshape(bs, ts, D_CONV)
    ctx = jnp.pad(cache_conv, ((0, 0), (HALO_PAD, 0), (0, 0)))
    ys = _finish(xs, o, u, ctx, w, bs, ts, 2, ts, bs * ts)
    u_ext = jnp.concatenate([cache_conv, u], axis=1)
    outs_s = (k.reshape(bs, ts, N_HEADS, HEAD_DIM), v.reshape(bs, ts, N_HEADS, HEAD_DIM), logf,
              u_ext[:, u_ext.shape[1] - CONV_CTX:])
    return yp, ys, outs_p, outs_s


def kernel(x_prompt, x_sample, cache_k, cache_v, cache_logf, cache_conv, w_in, b_forget, w_dw, b_dw,
           g_conv_ln, b_conv_ln, w_attn_out, w_conv_out, w_out, g_ln1, b_ln1, w_group, b_group,
           w_router, b_router, w1, w3, w2, g_ln2, b_ln2):
    xp, xs = x_prompt, x_sample
    per_layer_p, per_layer_s = [], []
    for l in range(DEPTH):
        w = _pack_weights(w_in[l], b_forget[l], w_dw[l], b_dw[l], g_conv_ln[l], b_conv_ln[l], w_attn_out[l],
                          w_conv_out[l], w_out[l], g_ln1[l], b_ln1[l], w_group[l], b_group[l], w_router[l],
                          b_router[l], w1[l], w3[l], w2[l], g_ln2[l], b_ln2[l])
        xp, xs, outs_p, outs_s = _layer(xp, xs, cache_k[l], cache_v[l], cache_logf[l], cache_conv[l], w)
        per_layer_p.append(outs_p)
        per_layer_s.append(outs_s)
    stack = lambda outs, j: jnp.stack([o[j] for o in outs])
    return (xp, xs,
            stack(per_layer_p, 0), stack(per_layer_p, 1), stack(per_layer_p, 2), stack(per_layer_p, 3),
            stack(per_layer_s, 0), stack(per_layer_s, 1), stack(per_layer_s, 2), stack(per_layer_s, 3))
```

```python
import functools

import jax
import jax.numpy as jnp
from jax import lax
from jax.experimental import pallas as pl
from jax.experimental.pallas import tpu as pltpu

f32 = jnp.float32
bf16 = jnp.bfloat16

D_MODEL = 1024
N_HEADS = 8
HEAD_DIM = 64
ATT_WIDTH = N_HEADS * HEAD_DIM
D_CONV = 512
CONV_WIDTH = 31
CONV_CTX = CONV_WIDTH - 1
N_GROUPS = 4
EXPERTS_PER_GROUP = 8
N_EXPERTS = N_GROUPS * EXPERTS_PER_GROUP
D_EXPERT = 256
DEPTH = 1
ALPHA = (2 * DEPTH) ** 0.25
LN_EPS = 1e-5
NEG = -1e30
SCALE = HEAD_DIM ** -0.5
LOG2E = 1.4426950408889634

N_SPLIT = 3
LANES = 128
SUBLANES = 8
HALO = 32
HALO_PAD = HALO - CONV_CTX
EXPERT_LANE0 = N_GROUPS
VMEM_LIMIT = 56 * 1024 * 1024

TOKEN_TILE = 512
ATTN_Q_TILE = 1024
CONV_ROWS = 32
MOE_TILE = 1024

_C_Q, _C_K, _C_V, _C_GA, _C_GB, _C_F, _C_END = 0, 512, 1024, 1536, 2048, 2560, 2688


def _sigmoid(x):
    return 1.0 / (1.0 + jnp.exp(-x))


def _layernorm(x, g, b):
    mu = jnp.mean(x, axis=-1, keepdims=True)
    xc = x - mu
    var = jnp.mean(xc * xc, axis=-1, keepdims=True)
    return xc * lax.rsqrt(var + LN_EPS) * g + b


def _params(*sem):
    return pltpu.CompilerParams(dimension_semantics=sem, vmem_limit_bytes=VMEM_LIMIT)


def _store_head_major(ref, val):
    tokens = val.shape[0]
    for h in range(N_HEADS):
        ref[pl.ds(h, tokens, stride=N_HEADS), :] = val[:, h * HEAD_DIM:(h + 1) * HEAD_DIM]


def _in_proj_kernel(x_ref, w_ref, bf_ref, q_ref, k_ref, kb_ref, v_ref, vt_ref, vb_ref, lf_ref, u_ref):
    tm = x_ref.shape[0]
    xb = x_ref[...].astype(bf16)

    def proj(lo, hi):
        return jnp.dot(xb, w_ref[:, lo:hi], preferred_element_type=f32)

    q_ref[...] = (proj(_C_Q, _C_K) * (SCALE * LOG2E)).astype(bf16)
    k = proj(_C_K, _C_V)
    _store_head_major(k_ref, k)
    kb_ref[...] = k.astype(bf16)
    v = proj(_C_V, _C_GA)
    _store_head_major(v_ref, v)
    vb_ref[...] = v.astype(bf16)
    vt_ref[0, :, 0] = v.T.astype(bf16).reshape(ATT_WIDTH // LANES, LANES, tm)
    u_ref[...] = proj(_C_GA, _C_GB) * _sigmoid(proj(_C_GB, _C_F))
    z = proj(_C_F, _C_END) + bf_ref[...]
    log_f = jnp.minimum(z, 0.0) - jnp.log(1.0 + jnp.exp(-jnp.abs(z)))
    lf_ref[0] = log_f.T[0:N_HEADS, :]


def _in_proj(x2d, w_pack, bf_pad, tm, groups):
    n = x2d.shape[0]
    cols = n // groups
    tpg = cols // tm
    row = lambda width: pl.BlockSpec((tm, width), lambda i: (i, 0))
    heads = pl.BlockSpec((tm * N_HEADS, HEAD_DIM), lambda i: (i, 0))
    hp = ATT_WIDTH // LANES
    return pl.pallas_call(
        _in_proj_kernel,
        grid=(n // tm,),
        in_specs=[row(D_MODEL),
                  pl.BlockSpec((D_MODEL, _C_END), lambda i: (0, 0)),
                  pl.BlockSpec((1, LANES), lambda i: (0, 0))],
        out_specs=[row(ATT_WIDTH), heads, row(ATT_WIDTH), heads,
                   pl.BlockSpec((1, hp, 1, LANES, tm), lambda i: (i // tpg, 0, i % tpg, 0, 0)),
                   row(ATT_WIDTH),
                   pl.BlockSpec((1, N_HEADS, tm), lambda i: (i // tpg, 0, i % tpg)),
                   row(D_CONV)],
        out_shape=[jax.ShapeDtypeStruct((n, ATT_WIDTH), bf16),
                   jax.ShapeDtypeStruct((n * N_HEADS, HEAD_DIM), f32),
                   jax.ShapeDtypeStruct((n, ATT_WIDTH), bf16),
                   jax.ShapeDtypeStruct((n * N_HEADS, HEAD_DIM), f32),
                   jax.ShapeDtypeStruct((groups, hp, tpg, LANES, tm), bf16),
                   jax.ShapeDtypeStruct((n, ATT_WIDTH), bf16),
                   jax.ShapeDtypeStruct((groups, N_HEADS, cols), f32),
                   jax.ShapeDtypeStruct((n, D_CONV), f32)],
        compiler_params=_params("parallel"),
        name="in_proj",
    )(x2d, w_pack, bf_pad)


def _split2(x):
    hi = x.astype(bf16)
    return hi, (x - hi.astype(f32)).astype(bf16)


def _split3(x):
    hi = x.astype(bf16)
    r1 = x - hi.astype(f32)
    mid = r1.astype(bf16)
    lo = (r1 - mid.astype(f32)).astype(bf16)
    return hi, mid, lo


def _cumsum_kernel(x_ref, o_ref, *, reverse, split_scale):
    nc, rows, _ = x_ref.shape
    r = lax.broadcasted_iota(jnp.int32, (LANES, LANES), 0)
    c = lax.broadcasted_iota(jnp.int32, (LANES, LANES), 1)
    tri = jnp.where((r > c) if reverse else (r <= c), 1.0, 0.0).astype(bf16)
    carry = jnp.zeros((rows, 1), f32)
    for j in (range(nc - 1, -1, -1) if reverse else range(nc)):
        xj = x_ref[j]
        y = carry
        for part in _split3(xj):
            y = y + jnp.dot(part, tri, preferred_element_type=f32)
        if split_scale is None:
            o_ref[j] = y
        else:
            for n, part in enumerate(_split3(split_scale * y)):
                o_ref[n, j] = part.astype(f32)
        carry = carry + jnp.sum(xj, axis=-1, keepdims=True)


def _cumsum_lanes(x, reverse, split_scale=None):
    rows, s = x.shape
    nc = s // LANES
    xc = x.reshape(rows, nc, LANES).transpose(1, 0, 2)
    shape = (nc, rows, LANES) if split_scale is None else (N_SPLIT, nc, rows, LANES)
    out = pl.pallas_call(
        functools.partial(_cumsum_kernel, reverse=reverse, split_scale=split_scale),
        out_shape=jax.ShapeDtypeStruct(shape, f32),
        name="cumsum_rev" if reverse else "cumsum_fwd",
    )(xc)
    if split_scale is None:
        return out.transpose(1, 0, 2).reshape(rows, s)
    return out.transpose(0, 2, 1, 3).reshape(N_SPLIT, rows, s)


def _attn_kernel(q_ref, k_ref, p_ref, vt_ref, o_ref, qx_sc, acc_sc, s0_sc, s1_sc, e_sc, *, tq, tk):
    i = pl.program_id(2)
    per_q = tq // tk

    @pl.when(i == 0)
    def _():
        row_k = lax.broadcasted_iota(jnp.int32, (LANES, tk), 0)
        for c in range(e_sc.shape[0] // tk):
            chans = jnp.zeros((LANES, tk), f32)
            for hh in range(2):
                for n in range(N_SPLIT):
                    part = p_ref[n, 0, 0, hh:hh + 1, c * tk:(c + 1) * tk]
                    chans = jnp.where(row_k == N_SPLIT * hh + n, part, chans)
            e_sc[c * tk:(c + 1) * tk, :] = chans.T.astype(bf16)

    qt = q_ref[0].astype(f32).T
    row = lax.broadcasted_iota(jnp.int32, (LANES, tq), 0)
    for h in range(2):
        qx_sc[h, 0:LANES, :] = jnp.where(row // HEAD_DIM == h, qt, 0.0).astype(bf16)
        qx_sc[h, LANES:2 * LANES, :] = jnp.where(row // N_SPLIT == h, 1.0, 0.0).astype(bf16)
    acc_sc[...] = jnp.zeros(acc_sc.shape, f32)
    ones_rows = jnp.ones((16, tk), bf16)

    s_sc = (s0_sc, s1_sc)

    def scores(j, h, lo=0):
        start = pl.multiple_of(j * tk, tk)
        kx = jnp.concatenate([k_ref[0, pl.ds(start, tk), :], e_sc[pl.ds(start, tk), :]], axis=1)
        s_sc[h][:, lo:] = jnp.dot(kx, qx_sc[h, :, lo:], preferred_element_type=f32)

    def absorb(j, h, stats, lo=0, masked=False):
        m_all, l_all = stats
        m_prev, l_prev = m_all[:, lo:], l_all[:, lo:]
        s = s_sc[h][:, lo:]
        if masked:
            key = lax.broadcasted_iota(jnp.int32, s.shape, 0)
            qry = lax.broadcasted_iota(jnp.int32, s.shape, 1)
            s = jnp.where(key <= qry, s, NEG)
        m_new = jnp.maximum(m_prev, jnp.max(s, axis=0, keepdims=True))
        alpha = jnp.exp2(m_prev - m_new)
        p = jnp.exp2(s - m_new).astype(bf16)
        vx = jnp.concatenate([vt_ref[0, 0, j, h * HEAD_DIM:(h + 1) * HEAD_DIM, :], ones_rows], axis=0)
        pv = jnp.dot(vx, p, preferred_element_type=f32)
        acc_sc[h, :, lo:] = alpha * acc_sc[h, :, lo:] + pv[0:HEAD_DIM, :]
        l_new = alpha * l_prev + pv[HEAD_DIM:HEAD_DIM + 1, :]
        if lo:
            m_new = jnp.concatenate([m_all[:, :lo], m_new], axis=1)
            l_new = jnp.concatenate([l_all[:, :lo], l_new], axis=1)
        return m_new, l_new

    def body(j, stats):
        scores(j, 1)
        st0 = absorb(j, 0, stats[0])
        scores(j + 1, 0)
        st1 = absorb(j, 1, stats[1])
        return st0, st1

    init = tuple((jnp.full((1, tq), NEG, f32), jnp.zeros((1, tq), f32)) for _ in range(2))
    scores(0, 0)
    pair = lambda p, st: body(2 * p + 1, body(2 * p, st))
    n_pairs = (per_q // 2) * i
    stats = lax.fori_loop(0, n_pairs // 2, lambda u, st: pair(2 * u + 1, pair(2 * u, st)), init)
    stats = lax.cond(n_pairs % 2 == 1, lambda st: pair(n_pairs - 1, st), lambda st: st, stats)
    st0, st1 = stats
    j0 = per_q * i
    for d in range(per_q):
        scores(j0 + d, 1, d * tk)
        st0 = absorb(j0 + d, 0, st0, d * tk, masked=True)
        if d + 1 < per_q:
            scores(j0 + d + 1, 0, (d + 1) * tk)
        st1 = absorb(j0 + d, 1, st1, d * tk, masked=True)
    l0, l1 = st0[1], st1[1]
    ot = jnp.concatenate([acc_sc[0] / l0, acc_sc[1] / l1], axis=0)
    o_ref[0] = ot.T.astype(bf16)


def _attn_prompt(q, kb, parts, vt, tq):
    b, s, _ = q.shape
    hp = ATT_WIDTH // LANES
    nq = s // tq
    nk, tk = vt.shape[2], vt.shape[4]
    assert tq % (2 * tk) == 0
    return pl.pallas_call(
        functools.partial(_attn_kernel, tq=tq, tk=tk),
        grid=(b, hp, nq),
        in_specs=[pl.BlockSpec((1, tq, LANES), lambda b_, h_, i: (b_, i, h_)),
                  pl.BlockSpec((1, s, LANES), lambda b_, h_, i: (b_, 0, h_)),
                  pl.BlockSpec((N_SPLIT, 1, 1, 2, s), lambda b_, h_, i: (0, b_, h_, 0, 0)),
                  pl.BlockSpec((1, 1, nk, LANES, tk), lambda b_, h_, i: (b_, h_, 0, 0, 0))],
        out_specs=pl.BlockSpec((1, tq, LANES), lambda b_, h_, i: (b_, i, h_)),
        out_shape=jax.ShapeDtypeStruct((b, s, ATT_WIDTH), bf16),
        scratch_shapes=[pltpu.VMEM((2, 2 * LANES, tq), bf16), pltpu.VMEM((2, HEAD_DIM, tq), f32),
                        pltpu.VMEM((tk, tq), f32), pltpu.VMEM((tk, tq), f32), pltpu.VMEM((s, LANES), bf16)],
        compiler_params=_params("parallel", "parallel", "arbitrary"),
        name="attn_prompt",
    )(q, kb, parts, vt)


def _attn_sample_kernel(q_ref, ck_ref, cv_ref, kn_ref, vn_ref, dc_ref, dn_ref, o_ref, *, t_new):
    rows_head = lambda shape: lax.shift_right_logical(lax.broadcasted_iota(jnp.int32, shape, 0), t_new.bit_length() - 1)
    per_query = lambda a: jnp.broadcast_to(a[:, None, :], (N_HEADS, t_new, a.shape[-1])).reshape(LANES, a.shape[-1])
    shape = (LANES, ATT_WIDTH)
    col_head = lax.shift_right_logical(lax.broadcasted_iota(jnp.int32, shape, 1), HEAD_DIM.bit_length() - 1)
    q_rows = jnp.tile(q_ref[0].astype(f32), (N_HEADS, 1))
    qbd = jnp.where(rows_head(shape) == col_head, q_rows, 0.0).astype(bf16)
    query = jnp.bitwise_and(lax.broadcasted_iota(jnp.int32, (LANES, LANES), 0), t_new - 1)
    new_key = lax.broadcasted_iota(jnp.int32, (LANES, LANES), 1)
    bias_n = jnp.where(new_key <= query, per_query(dn_ref[0]), NEG)
    s_c = jnp.dot(qbd, ck_ref[0].astype(bf16), preferred_element_type=f32) + per_query(dc_ref[0])
    s_n = jnp.dot(qbd, kn_ref[0], preferred_element_type=f32) + bias_n
    m = jnp.maximum(jnp.max(s_c, axis=-1, keepdims=True), jnp.max(s_n, axis=-1, keepdims=True))
    p_c = jnp.exp2(s_c - m)
    p_n = jnp.exp2(s_n - m)
    inv = 1.0 / (jnp.sum(p_c, axis=-1, keepdims=True) + jnp.sum(p_n, axis=-1, keepdims=True))
    p_c = (p_c * inv).astype(bf16)
    p_n = (p_n * inv).astype(bf16)
    over_positions = (((1,), (1,)), ((), ()))
    o_all = (lax.dot_general(cv_ref[0].astype(bf16), p_c, over_positions, preferred_element_type=f32)
             + lax.dot_general(vn_ref[0], p_n, over_positions, preferred_element_type=f32))
    shape = (ATT_WIDTH, LANES)
    row_head = lax.shift_right_logical(lax.broadcasted_iota(jnp.int32, shape, 0), HEAD_DIM.bit_length() - 1)
    col_head = lax.shift_right_logical(lax.broadcasted_iota(jnp.int32, shape, 1), t_new.bit_length() - 1)
    own = jnp.where(row_head == col_head, o_all, 0.0).astype(bf16)
    q_of_row = jnp.bitwise_and(lax.broadcasted_iota(jnp.int32, (LANES, LANES), 0), t_new - 1)
    pick = jnp.where(q_of_row == lax.broadcasted_iota(jnp.int32, (LANES, LANES), 1), 1.0, 0.0).astype(bf16)
    o_t = jnp.dot(own, pick, preferred_element_type=f32)
    o_ref[0] = o_t.T[0:t_new, :].astype(bf16)


def _attn_sample(q, ck, cv, kn, vn, decay_c, decay_n, t_new):
    b, _, p = ck.shape
    assert N_HEADS * t_new == LANES
    blk = lambda r, c: pl.BlockSpec((1, r, c), lambda i: (i, 0, 0))
    return pl.pallas_call(
        functools.partial(_attn_sample_kernel, t_new=t_new),
        grid=(b,),
        in_specs=[blk(t_new, ATT_WIDTH), blk(ATT_WIDTH, p), blk(ATT_WIDTH, p), blk(ATT_WIDTH, LANES),
                  blk(ATT_WIDTH, LANES), blk(N_HEADS, p), blk(N_HEADS, LANES)],
        out_specs=blk(t_new, ATT_WIDTH),
        out_shape=jax.ShapeDtypeStruct((b, t_new, ATT_WIDTH), bf16),
        compiler_params=_params("parallel"),
        name="attn_sample",
    )(q, ck, cv, kn, vn, decay_c, decay_n)


def _route(logits):
    lane = lax.broadcasted_iota(jnp.int32, logits.shape, 1).astype(f32)
    ninf = -jnp.inf
    gl = jnp.where(lane < N_GROUPS, logits, ninf)
    gmax = jnp.max(gl, axis=-1, keepdims=True)
    grp = jnp.min(jnp.where(gl == gmax, lane, float(LANES)), axis=-1, keepdims=True)
    p_sel = 1.0 / jnp.sum(jnp.exp(gl - gmax), axis=-1, keepdims=True)
    e_lo = EXPERT_LANE0 + grp * EXPERTS_PER_GROUP
    el = jnp.where(lane >= e_lo, jnp.where(lane < e_lo + EXPERTS_PER_GROUP, logits, ninf), ninf)
    t1 = jnp.max(el, axis=-1, keepdims=True)
    i1 = jnp.min(jnp.where(el == t1, lane, float(LANES)), axis=-1, keepdims=True)
    el2 = jnp.where(lane == i1, ninf, el)
    t2 = jnp.max(el2, axis=-1, keepdims=True)
    i2 = jnp.min(jnp.where(el2 == t2, lane, float(LANES)), axis=-1, keepdims=True)
    e2 = jnp.exp(t2 - t1)
    w1 = p_sel / (1.0 + e2)
    w2 = p_sel * e2 / (1.0 + e2)
    picks = (i1 - EXPERT_LANE0, i2 - EXPERT_LANE0, w1, w2)
    out = jnp.zeros(logits.shape, f32)
    for n, val in enumerate(picks):
        out = jnp.where(lane == float(n), val, out)
    return out


def _tail_kernel(x_ref, o_ref, u_ref, uprev_ref, ctx_ref, wdw_ref, bdw_ref, gcl_ref, bcl_ref,
                 wa_ref, wc_ref, wg_ref, wo_ref, g1_ref, b1_ref, wr_ref, br_ref,
                 h_ref, gates_ref, uext_sc, shift_sc, c_sc, *, cb, ct):
    i = pl.program_id(1)
    bb, t, _ = x_ref.shape
    n = bb * t

    x = x_ref[...].reshape(n, D_MODEL)
    g = _sigmoid(jnp.dot(x.astype(bf16), wg_ref[...], preferred_element_type=f32))
    branch_a = jnp.dot(o_ref[...].reshape(n, ATT_WIDTH), wa_ref[...], preferred_element_type=f32)

    uext_sc[:, 0:HALO, :] = jnp.where(i == 0, ctx_ref[...], uprev_ref[...])
    uext_sc[:, HALO:HALO + t, :] = u_ref[...]
    span = t + HALO - SUBLANES
    for r in range(1, SUBLANES):
        shift_sc[r - 1] = uext_sc[:, r:r + span, :]
    for b0 in range(0, bb, cb):
        for t0 in range(0, t, ct):
            groups = cb * ct // SUBLANES
            acc = jnp.zeros((groups, SUBLANES, D_CONV), f32) + bdw_ref[...]
            for k in range(CONV_WIDTH):
                a, r = divmod(HALO_PAD + k, SUBLANES)
                r0 = t0 + a * SUBLANES
                src = uext_sc[b0:b0 + cb, r0:r0 + ct, :] if r == 0 else shift_sc[r - 1, b0:b0 + cb, r0:r0 + ct, :]
                acc = acc + wdw_ref[k] * src.reshape(groups, SUBLANES, D_CONV)
            c_sc[b0:b0 + cb, t0:t0 + ct, :] = acc.reshape(cb, ct, D_CONV)

    c = _layernorm(c_sc[...].reshape(n, D_CONV), gcl_ref[...], bcl_ref[...])
    c = c * _sigmoid(c)
    branch_b = jnp.dot(c.astype(bf16), wc_ref[...], preferred_element_type=f32)
    merged = g[:, :D_MODEL] * branch_a + g[:, D_MODEL:] * branch_b
    mix = jnp.dot(merged.astype(bf16), wo_ref[...], preferred_element_type=f32)
    h = _layernorm(ALPHA * x + mix, g1_ref[...], b1_ref[...])
    h_ref[...] = h.reshape(bb, t, D_MODEL)
    w_hi, w_lo = _split2(wr_ref[...])
    w_parts = jnp.concatenate([w_hi, w_lo], axis=1)
    cross = sum(jnp.dot(part, w_parts, preferred_element_type=f32) for part in _split2(h))
    logits = cross[:, :LANES] + cross[:, LANES:] + br_ref[...]
    gates_ref[...] = _route(logits).reshape(bb, t, LANES)


def _tail(x, o, u, ctx, w, bb, t, cb, ct):
    b, s, _ = x.shape
    nt = s // t
    uprev = u if nt > 1 else ctx
    per_tile = t // HALO
    tile = lambda width: pl.BlockSpec((bb, t, width), lambda b_, i: (b_, i, 0))
    full = lambda a: pl.BlockSpec(a.shape, lambda b_, i: (0,) * a.ndim)
    weights = [w["w_dw"], w["b_dw"], w["g_cln"], w["b_cln"], w["w_attn_out"], w["w_conv_out"], w["w_gate"],
               w["w_out"], w["g_ln1"], w["b_ln1"], w["w_route"], w["b_route"]]
    return pl.pallas_call(
        functools.partial(_tail_kernel, cb=cb, ct=ct),
        grid=(b // bb, nt),
        in_specs=[tile(D_MODEL), tile(ATT_WIDTH), tile(D_CONV),
                  pl.BlockSpec((bb, HALO, D_CONV), lambda b_, i: (b_, jnp.maximum(i * per_tile - 1, 0), 0)),
                  pl.BlockSpec((bb, HALO, D_CONV), lambda b_, i: (b_, 0, 0))]
                 + [full(a) for a in weights],
        out_specs=[tile(D_MODEL), tile(LANES)],
        out_shape=[jax.ShapeDtypeStruct((b, s, D_MODEL), f32), jax.ShapeDtypeStruct((b, s, LANES), f32)],
        scratch_shapes=[pltpu.VMEM((bb, HALO + t, D_CONV), f32),
                        pltpu.VMEM((SUBLANES - 1, bb, t + HALO - SUBLANES, D_CONV), f32),
                        pltpu.VMEM((bb, t, D_CONV), f32)],
        compiler_params=_params("parallel", "arbitrary"),
        name="tail",
    )(x, o, u, uprev, ctx, *weights)


MOE_SUB = 256
MOE_CAP = 32
MOE_EPC = 8
MOE_KEY = 4096
_CAP_SHIFT = MOE_CAP.bit_length() - 1


def _moe_kernel(h_ref, r_ref, w1_ref, w3_ref, w2_ref, g2_ref, b2_ref, y_ref,
                acc_sc, hb_sc, key_sc, keyt_sc, nr_sc, *, n_sub):
    c = pl.program_id(1)
    slots = MOE_EPC * MOE_CAP

    @pl.when(c == 0)
    def _():
        acc_sc[...] = jnp.zeros(acc_sc.shape, f32)
        hb_sc[...] = h_ref[...].astype(bf16)
        lane = lax.broadcasted_iota(jnp.int32, (MOE_SUB, LANES), 1).astype(f32)
        ti = lax.broadcasted_iota(jnp.int32, (MOE_SUB, MOE_SUB), 0)
        tj = lax.broadcasted_iota(jnp.int32, (MOE_SUB, MOE_SUB), 1)
        earlier = jnp.where(tj < ti, 1.0, 0.0).astype(bf16)
        li = lax.broadcasted_iota(jnp.int32, (LANES, LANES), 0)
        from_lane = [jnp.where(li == n, 1.0, 0.0).astype(bf16) for n in range(2)]
        lane_sum = jnp.ones((LANES, LANES), bf16)
        spread = lambda x, m: jnp.dot(x.astype(bf16), m, preferred_element_type=f32)
        most = jnp.zeros((1, LANES), f32)
        for a in range(n_sub):
            r = r_ref[a * MOE_SUB:(a + 1) * MOE_SUB, :]
            picks = jnp.where(lane < 2.0, r, 0.0)
            e1, e2 = spread(picks, from_lane[0]), spread(picks, from_lane[1])
            oh1 = jnp.where(lane == e1, 1.0, 0.0)
            oh2 = jnp.where(lane == e2, 1.0, 0.0)
            both = oh1 + oh2
            before = jnp.dot(earlier, both.astype(bf16), preferred_element_type=f32)
            key1 = e1 * MOE_KEY + spread(before * oh1, lane_sum)
            key2 = e2 * MOE_KEY + spread(before * oh2, lane_sum)
            info = jnp.where(lane == 0.0, key1, jnp.where(lane == 1.0, key2, r))
            key_sc[a] = info
            keyt_sc[a] = info.T[0:8, :]
            most = jnp.maximum(most, jnp.sum(both, axis=0, keepdims=True))
        nr_sc[0] = (jnp.max(most).astype(jnp.int32) + (MOE_CAP - 1)) // MOE_CAP

    e0 = c * MOE_EPC
    s_row = lax.broadcasted_iota(jnp.int32, (slots, MOE_SUB), 0)
    s_col = lax.broadcasted_iota(jnp.int32, (MOE_SUB, slots), 1)

    def slot_key(s, rd):
        expert = lax.shift_right_logical(s, _CAP_SHIFT) + e0
        return (expert * MOE_KEY + jnp.bitwise_and(s, MOE_CAP - 1) + rd * MOE_CAP).astype(f32)

    def one_round(rd, carry):
        key_r = slot_key(s_row, rd)
        key_c = slot_key(s_col, rd)
        xs = []
        for a in range(n_sub):
            k1, k2 = keyt_sc[a, 0:1, :], keyt_sc[a, 1:2, :]
            p = jnp.where(key_r == k1, 1.0, jnp.where(key_r == k2, 1.0, 0.0)).astype(bf16)
            hb = hb_sc[a * MOE_SUB:(a + 1) * MOE_SUB, :]
            xs.append(jnp.dot(p, hb, preferred_element_type=f32).astype(bf16))
        ys = [[] for _ in range(n_sub)]
        for e in range(MOE_EPC):
            xe = jnp.concatenate([x[e * MOE_CAP:(e + 1) * MOE_CAP] for x in xs], axis=0)
            a1 = jnp.dot(xe, w1_ref[e], preferred_element_type=f32)
            a3 = jnp.dot(xe, w3_ref[e], preferred_element_type=f32)
            act = (a1 * _sigmoid(a1)) * a3
            ye = jnp.dot(act.astype(bf16), w2_ref[e], preferred_element_type=f32).astype(bf16)
            for a in range(n_sub):
                ys[a].append(ye[a * MOE_CAP:(a + 1) * MOE_CAP])
        for a in range(n_sub):
            info = key_sc[a]
            pw = jnp.where(key_c == info[:, 0:1], info[:, 2:3],
                           jnp.where(key_c == info[:, 1:2], info[:, 3:4], 0.0)).astype(bf16)
            rows = slice(a * MOE_SUB, (a + 1) * MOE_SUB)
            acc_sc[rows, :] += jnp.dot(pw, jnp.concatenate(ys[a], axis=0), preferred_element_type=f32)
        return carry

    lax.fori_loop(0, nr_sc[0], one_round, 0)

    @pl.when(c == pl.num_programs(1) - 1)
    def _():
        y_ref[...] = _layernorm(ALPHA * h_ref[...] + acc_sc[...], g2_ref[...], b2_ref[...])


def _moe(h2d, route, w1, w3, w2, g2, b2, tm):
    n = h2d.shape[0]
    n_sub = tm // MOE_SUB
    expert_block = lambda a: pl.BlockSpec((MOE_EPC,) + a.shape[1:], lambda i, c: (c, 0, 0))
    return pl.pallas_call(
        functools.partial(_moe_kernel, n_sub=n_sub),
        grid=(n // tm, N_EXPERTS // MOE_EPC),
        in_specs=[pl.BlockSpec((tm, D_MODEL), lambda i, c: (i, 0)),
                  pl.BlockSpec((tm, LANES), lambda i, c: (i, 0)),
                  expert_block(w1), expert_block(w3), expert_block(w2),
                  pl.BlockSpec((1, D_MODEL), lambda i, c: (0, 0)),
                  pl.BlockSpec((1, D_MODEL), lambda i, c: (0, 0))],
        out_specs=pl.BlockSpec((tm, D_MODEL), lambda i, c: (i, 0)),
        out_shape=jax.ShapeDtypeStruct((n, D_MODEL), f32),
        scratch_shapes=[pltpu.VMEM((tm, D_MODEL), f32), pltpu.VMEM((tm, D_MODEL), bf16),
                        pltpu.VMEM((n_sub, MOE_SUB, LANES), f32), pltpu.VMEM((n_sub, 8, MOE_SUB), f32),
                        pltpu.SMEM((1,), jnp.int32)],
        compiler_params=_params("parallel", "arbitrary"),
        name="moe",
    )(h2d, route, w1, w3, w2, g2, b2)


def _pack_weights(w_in, b_forget, w_dw, b_dw, g_conv_ln, b_conv_ln, w_attn_out, w_conv_out, w_out,
                  g_ln1, b_ln1, w_group, b_group, w_router, b_router, w1, w3, w2, g_ln2, b_ln2):
    a = ATT_WIDTH
    c_f, c_glu, c_gate = 3 * a, 3 * a + N_HEADS, 3 * a + N_HEADS + 2 * D_CONV
    w_f = jnp.pad(w_in[:, c_f:c_glu], ((0, 0), (0, LANES - N_HEADS)))
    w_pack = jnp.concatenate([w_in[:, :c_f], w_in[:, c_glu:c_gate], w_f], axis=1).astype(bf16)
    pad_route = LANES - N_GROUPS - N_EXPERTS
    row = lambda v: v.reshape(1, -1)
    return dict(
        w_pack=w_pack,
        b_forget=jnp.pad(b_forget, (0, LANES - N_HEADS)).reshape(1, LANES),
        w_gate=w_in[:, c_gate:].astype(bf16),
        w_dw=jnp.broadcast_to(w_dw[:, None, :], (CONV_WIDTH, SUBLANES, D_CONV)),
        b_dw=jnp.broadcast_to(b_dw[None, :], (SUBLANES, D_CONV)),
        g_cln=row(g_conv_ln), b_cln=row(b_conv_ln),
        w_attn_out=w_attn_out.astype(bf16), w_conv_out=w_conv_out.astype(bf16), w_out=w_out.astype(bf16),
        g_ln1=row(g_ln1), b_ln1=row(b_ln1),
        w_route=jnp.pad(jnp.concatenate([w_group, w_router], axis=1), ((0, 0), (0, pad_route))),
        b_route=jnp.pad(jnp.concatenate([b_group, b_router]), (0, pad_route)).reshape(1, LANES),
        w1=w1.astype(bf16), w3=w3.astype(bf16), w2=w2.astype(bf16),
        g_ln2=row(g_ln2), b_ln2=row(b_ln2),
    )


def _finish(x, o, u, ctx, w, bb, t, cb, ct, tm_moe):
    b, s, _ = x.shape
    h, gates = _tail(x, o, u, ctx, w, bb, t, cb, ct)
    y = _moe(h.reshape(b * s, D_MODEL), gates.reshape(b * s, LANES), w["w1"], w["w3"], w["w2"],
             w["g_ln2"], w["b_ln2"], tm_moe)
    return y.reshape(b, s, D_MODEL)


def _layer(xp, xs, cache_k, cache_v, cache_logf, cache_conv, w):
    bp, sp, _ = xp.shape
    bs, ts, _ = xs.shape
    past = cache_k.shape[1]
    tq, tk = ATTN_Q_TILE, TOKEN_TILE

    q, k, kb, v, vt, vb, lf, u = _in_proj(xp.reshape(bp * sp, D_MODEL), w["w_pack"], w["b_forget"], tk, bp)
    logf = lf.transpose(0, 2, 1)
    parts = _cumsum_lanes(lf.reshape(bp * N_HEADS, sp), reverse=False, split_scale=-LOG2E)
    parts = parts.reshape(N_SPLIT, bp, ATT_WIDTH // LANES, 2, sp)
    shp = (bp, sp, ATT_WIDTH)
    o = _attn_prompt(q.reshape(shp), kb.reshape(shp), parts, vt, tq)
    u = u.reshape(bp, sp, D_CONV)
    yp = _finish(xp, o, u, jnp.zeros((bp, HALO, D_CONV), f32), w, 1, TOKEN_TILE, 1, CONV_ROWS, MOE_TILE)
    outs_p = (k.reshape(bp, sp, N_HEADS, HEAD_DIM), v.reshape(bp, sp, N_HEADS, HEAD_DIM), logf,
              u[:, sp - CONV_CTX:])

    q, k, kb, v, _, vb, lf, u = _in_proj(xs.reshape(bs * ts, D_MODEL), w["w_pack"], w["b_forget"], bs * ts, 1)
    lf = lf.reshape(N_HEADS, bs, ts)
    logf = lf.transpose(1, 2, 0)
    lf_t = jnp.pad(lf.transpose(1, 0, 2), ((0, 0), (0, 0), (0, LANES - ts)))
    cq = _cumsum_lanes(lf_t.reshape(bs * N_HEADS, LANES), reverse=False).reshape(bs, N_HEADS, LANES)
    r = _cumsum_lanes(cache_logf.transpose(0, 2, 1).reshape(bs * N_HEADS, past), reverse=True)
    decay_c = LOG2E * r.reshape(bs, N_HEADS, past)
    decay_n = -LOG2E * cq
    new_t = lambda a: jnp.pad(a.reshape(bs, ts, ATT_WIDTH).transpose(0, 2, 1), ((0, 0), (0, 0), (0, LANES - ts)))
    cache_t = lambda a: a.transpose(0, 2, 3, 1).reshape(bs, ATT_WIDTH, past)
    o = _attn_sample(q.reshape(bs, ts, ATT_WIDTH), cache_t(cache_k), cache_t(cache_v), new_t(kb), new_t(vb),
                     decay_c, decay_n, ts)
    u = u.reshape(bs, ts, D_CONV)
    ctx = jnp.pad(cache_conv, ((0, 0), (HALO_PAD, 0), (0, 0)))
    ys = _finish(xs, o, u, ctx, w, bs, ts, 2, ts, bs * ts)
    u_ext = jnp.concatenate([cache_conv, u], axis=1)
    outs_s = (k.reshape(bs, ts, N_HEADS, HEAD_DIM), v.reshape(bs, ts, N_HEADS, HEAD_DIM), logf,
              u_ext[:, u_ext.shape[1] - CONV_CTX:])
    return yp, ys, outs_p, outs_s


def kernel(x_prompt, x_sample, cache_k, cache_v, cache_logf, cache_conv, w_in, b_forget, w_dw, b_dw,
           g_conv_ln, b_conv_ln, w_attn_out, w_conv_out, w_out, g_ln1, b_ln1, w_group, b_group,
           w_router, b_router, w1, w3, w2, g_ln2, b_ln2):
    xp, xs = x_prompt, x_sample
    per_layer_p, per_layer_s = [], []
    for l in range(DEPTH):
        w = _pack_weights(w_in[l], b_forget[l], w_dw[l], b_dw[l], g_conv_ln[l], b_conv_ln[l], w_attn_out[l],
                          w_conv_out[l], w_out[l], g_ln1[l], b_ln1[l], w_group[l], b_group[l], w_router[l],
                          b_router[l], w1[l], w3[l], w2[l], g_ln2[l], b_ln2[l])
        xp, xs, outs_p, outs_s = _layer(xp, xs, cache_k[l], cache_v[l], cache_logf[l], cache_conv[l], w)
        per_layer_p.append(outs_p)
        per_layer_s.append(outs_s)
    stack = lambda outs, j: jnp.stack([o[j] for o in outs])
    return (xp, xs,
            stack(per_layer_p, 0), stack(per_layer_p, 1), stack(per_layer_p, 2), stack(per_layer_p, 3),
            stack(per_layer_s, 0), stack(per_layer_s, 1), stack(per_layer_s, 2), stack(per_layer_s, 3))
```

```python
import functools

import jax
import jax.numpy as jnp
from jax import lax
from jax.experimental import pallas as pl
from jax.experimental.pallas import tpu as pltpu

f32 = jnp.float32
bf16 = jnp.bfloat16

D_MODEL = 1024
N_HEADS = 8
HEAD_DIM = 64
ATT_WIDTH = N_HEADS * HEAD_DIM
D_CONV = 512
CONV_WIDTH = 31
CONV_CTX = CONV_WIDTH - 1
N_GROUPS = 4
EXPERTS_PER_GROUP = 8
N_EXPERTS = N_GROUPS * EXPERTS_PER_GROUP
D_EXPERT = 256
DEPTH = 1
ALPHA = (2 * DEPTH) ** 0.25
LN_EPS = 1e-5
NEG = -1e30
SCALE = HEAD_DIM ** -0.5
LOG2E = 1.4426950408889634

N_SPLIT = 3
LANES = 128
SUBLANES = 8
HALO = 32
HALO_PAD = HALO - CONV_CTX
EXPERT_LANE0 = N_GROUPS
VMEM_LIMIT = 56 * 1024 * 1024

TOKEN_TILE = 512
ATTN_Q_TILE = 1024
CONV_ROWS = 32
MOE_TILE = 1024


def _sigmoid(x):
    return 1.0 / (1.0 + jnp.exp(-x))


def _layernorm(x, g, b):
    mu = jnp.mean(x, axis=-1, keepdims=True)
    xc = x - mu
    var = jnp.mean(xc * xc, axis=-1, keepdims=True)
    return xc * lax.rsqrt(var + LN_EPS) * g + b


def _params(*sem):
    return pltpu.CompilerParams(dimension_semantics=sem, vmem_limit_bytes=VMEM_LIMIT)


def _store_head_major(ref, val):
    tokens = val.shape[0]
    for h in range(N_HEADS):
        ref[pl.ds(h, tokens, stride=N_HEADS), :] = val[:, h * HEAD_DIM:(h + 1) * HEAD_DIM]


def _in_proj_kernel(x_ref, wqkv_ref, wglu_ref, wf_ref, bf_ref,
                    q_ref, k_ref, kb_ref, v_ref, vt_ref, vb_ref, lf_ref, u_ref):
    tm = x_ref.shape[0]
    xb = x_ref[...].astype(bf16)

    def proj(w_ref, lo, hi):
        return jnp.dot(xb, w_ref[:, lo:hi], preferred_element_type=f32)

    a = ATT_WIDTH
    q_ref[...] = (proj(wqkv_ref, 0, a) * (SCALE * LOG2E)).astype(bf16)
    k = proj(wqkv_ref, a, 2 * a)
    _store_head_major(k_ref, k)
    kb_ref[...] = k.astype(bf16)
    v = proj(wqkv_ref, 2 * a, 3 * a)
    _store_head_major(v_ref, v)
    vb_ref[...] = v.astype(bf16)
    vt_ref[0, :, 0] = v.T.astype(bf16).reshape(ATT_WIDTH // LANES, LANES, tm)
    u_ref[...] = proj(wglu_ref, 0, D_CONV) * _sigmoid(proj(wglu_ref, D_CONV, 2 * D_CONV))
    z = proj(wf_ref, 0, LANES) + bf_ref[...]
    log_f = jnp.minimum(z, 0.0) - jnp.log(1.0 + jnp.exp(-jnp.abs(z)))
    lf_ref[0] = log_f.T[0:N_HEADS, :]


def _in_proj(x2d, w_parts, bf_pad, tm, groups):
    n = x2d.shape[0]
    whole = lambda a: pl.BlockSpec(a.shape, lambda i: (0,) * a.ndim)
    cols = n // groups
    tpg = cols // tm
    row = lambda width: pl.BlockSpec((tm, width), lambda i: (i, 0))
    heads = pl.BlockSpec((tm * N_HEADS, HEAD_DIM), lambda i: (i, 0))
    hp = ATT_WIDTH // LANES
    return pl.pallas_call(
        _in_proj_kernel,
        grid=(n // tm,),
        in_specs=[row(D_MODEL)] + [whole(a) for a in w_parts] + [whole(bf_pad)],
        out_specs=[row(ATT_WIDTH), heads, row(ATT_WIDTH), heads,
                   pl.BlockSpec((1, hp, 1, LANES, tm), lambda i: (i // tpg, 0, i % tpg, 0, 0)),
                   row(ATT_WIDTH),
                   pl.BlockSpec((1, N_HEADS, tm), lambda i: (i // tpg, 0, i % tpg)),
                   row(D_CONV)],
        out_shape=[jax.ShapeDtypeStruct((n, ATT_WIDTH), bf16),
                   jax.ShapeDtypeStruct((n * N_HEADS, HEAD_DIM), f32),
                   jax.ShapeDtypeStruct((n, ATT_WIDTH), bf16),
                   jax.ShapeDtypeStruct((n * N_HEADS, HEAD_DIM), f32),
                   jax.ShapeDtypeStruct((groups, hp, tpg, LANES, tm), bf16),
                   jax.ShapeDtypeStruct((n, ATT_WIDTH), bf16),
                   jax.ShapeDtypeStruct((groups, N_HEADS, cols), f32),
                   jax.ShapeDtypeStruct((n, D_CONV), f32)],
        compiler_params=_params("parallel"),
        name="in_proj",
    )(x2d, *w_parts, bf_pad)


def _split2(x):
    hi = x.astype(bf16)
    return hi, (x - hi.astype(f32)).astype(bf16)


def _split3(x):
    hi = x.astype(bf16)
    r1 = x - hi.astype(f32)
    mid = r1.astype(bf16)
    lo = (r1 - mid.astype(f32)).astype(bf16)
    return hi, mid, lo


def _cumsum_kernel(x_ref, o_ref, *, reverse, split_scale):
    nc, rows, _ = x_ref.shape
    r = lax.broadcasted_iota(jnp.int32, (LANES, LANES), 0)
    c = lax.broadcasted_iota(jnp.int32, (LANES, LANES), 1)
    tri = jnp.where((r > c) if reverse else (r <= c), 1.0, 0.0).astype(bf16)
    carry = jnp.zeros((rows, 1), f32)
    for j in (range(nc - 1, -1, -1) if reverse else range(nc)):
        xj = x_ref[j]
        y = carry
        for part in _split3(xj):
            y = y + jnp.dot(part, tri, preferred_element_type=f32)
        if split_scale is None:
            o_ref[j] = y
        else:
            for n, part in enumerate(_split3(split_scale * y)):
                o_ref[n, j] = part.astype(f32)
        carry = carry + jnp.sum(xj, axis=-1, keepdims=True)


def _cumsum_lanes(x, reverse, split_scale=None):
    rows, s = x.shape
    nc = s // LANES
    xc = x.reshape(rows, nc, LANES).transpose(1, 0, 2)
    shape = (nc, rows, LANES) if split_scale is None else (N_SPLIT, nc, rows, LANES)
    out = pl.pallas_call(
        functools.partial(_cumsum_kernel, reverse=reverse, split_scale=split_scale),
        out_shape=jax.ShapeDtypeStruct(shape, f32),
        name="cumsum_rev" if reverse else "cumsum_fwd",
    )(xc)
    if split_scale is None:
        return out.transpose(1, 0, 2).reshape(rows, s)
    return out.transpose(0, 2, 1, 3).reshape(N_SPLIT, rows, s)


def _attn_kernel(q_ref, k_ref, p_ref, vt_ref, o_ref, qx_sc, acc_sc, s0_sc, s1_sc, e_sc, *, tq, tk):
    i = pl.program_id(2)
    per_q = tq // tk

    @pl.when(i == 0)
    def _():
        row_k = lax.broadcasted_iota(jnp.int32, (LANES, tk), 0)
        for c in range(e_sc.shape[0] // tk):
            chans = jnp.zeros((LANES, tk), f32)
            for hh in range(2):
                for n in range(N_SPLIT):
                    part = p_ref[n, 0, 0, hh:hh + 1, c * tk:(c + 1) * tk]
                    chans = jnp.where(row_k == N_SPLIT * hh + n, part, chans)
            e_sc[c * tk:(c + 1) * tk, :] = chans.T.astype(bf16)

    qt = q_ref[0].astype(f32).T
    row = lax.broadcasted_iota(jnp.int32, (LANES, tq), 0)
    for h in range(2):
        qx_sc[h, 0:LANES, :] = jnp.where(row // HEAD_DIM == h, qt, 0.0).astype(bf16)
        qx_sc[h, LANES:2 * LANES, :] = jnp.where(row // N_SPLIT == h, 1.0, 0.0).astype(bf16)
    acc_sc[...] = jnp.zeros(acc_sc.shape, f32)
    ones_rows = jnp.ones((16, tk), bf16)

    s_sc = (s0_sc, s1_sc)

    def scores(j, h, lo=0):
        start = pl.multiple_of(j * tk, tk)
        kx = jnp.concatenate([k_ref[0, pl.ds(start, tk), :], e_sc[pl.ds(start, tk), :]], axis=1)
        s_sc[h][:, lo:] = jnp.dot(kx, qx_sc[h, :, lo:], preferred_element_type=f32)

    def absorb(j, h, stats, lo=0, masked=False):
        m_all, l_all = stats
        m_prev, l_prev = m_all[:, lo:], l_all[:, lo:]
        s = s_sc[h][:, lo:]
        if masked:
            key = lax.broadcasted_iota(jnp.int32, s.shape, 0)
            qry = lax.broadcasted_iota(jnp.int32, s.shape, 1)
            s = jnp.where(key <= qry, s, NEG)
        m_new = jnp.maximum(m_prev, jnp.max(s, axis=0, keepdims=True))
        alpha = jnp.exp2(m_prev - m_new)
        p = jnp.exp2(s - m_new).astype(bf16)
        vx = jnp.concatenate([vt_ref[0, 0, j, h * HEAD_DIM:(h + 1) * HEAD_DIM, :], ones_rows], axis=0)
        pv = jnp.dot(vx, p, preferred_element_type=f32)
        acc_sc[h, :, lo:] = alpha * acc_sc[h, :, lo:] + pv[0:HEAD_DIM, :]
        l_new = alpha * l_prev + pv[HEAD_DIM:HEAD_DIM + 1, :]
        if lo:
            m_new = jnp.concatenate([m_all[:, :lo], m_new], axis=1)
            l_new = jnp.concatenate([l_all[:, :lo], l_new], axis=1)
        return m_new, l_new

    def body(j, stats):
        scores(j, 1)
        st0 = absorb(j, 0, stats[0])
        scores(j + 1, 0)
        st1 = absorb(j, 1, stats[1])
        return st0, st1

    init = tuple((jnp.full((1, tq), NEG, f32), jnp.zeros((1, tq), f32)) for _ in range(2))
    scores(0, 0)
    stats = lax.fori_loop(0, (per_q // 2) * i, lambda p, st: body(2 * p + 1, body(2 * p, st)), init)
    st0, st1 = stats
    j0 = per_q * i
    for d in range(per_q):
        scores(j0 + d, 1, d * tk)
        st0 = absorb(j0 + d, 0, st0, d * tk, masked=True)
        if d + 1 < per_q:
            scores(j0 + d + 1, 0, (d + 1) * tk)
        st1 = absorb(j0 + d, 1, st1, d * tk, masked=True)
    l0, l1 = st0[1], st1[1]
    ot = jnp.concatenate([acc_sc[0] / l0, acc_sc[1] / l1], axis=0)
    o_ref[0] = ot.T.astype(bf16)


def _attn_prompt(q, kb, parts, vt, tq):
    b, s, _ = q.shape
    hp = ATT_WIDTH // LANES
    nq = s // tq
    nk, tk = vt.shape[2], vt.shape[4]
    assert tq % (2 * tk) == 0
    return pl.pallas_call(
        functools.partial(_attn_kernel, tq=tq, tk=tk),
        grid=(b, hp, nq),
        in_specs=[pl.BlockSpec((1, tq, LANES), lambda b_, h_, i: (b_, i, h_)),
                  pl.BlockSpec((1, s, LANES), lambda b_, h_, i: (b_, 0, h_)),
                  pl.BlockSpec((N_SPLIT, 1, 1, 2, s), lambda b_, h_, i: (0, b_, h_, 0, 0)),
                  pl.BlockSpec((1, 1, nk, LANES, tk), lambda b_, h_, i: (b_, h_, 0, 0, 0))],
        out_specs=pl.BlockSpec((1, tq, LANES), lambda b_, h_, i: (b_, i, h_)),
        out_shape=jax.ShapeDtypeStruct((b, s, ATT_WIDTH), bf16),
        scratch_shapes=[pltpu.VMEM((2, 2 * LANES, tq), bf16), pltpu.VMEM((2, HEAD_DIM, tq), f32),
                        pltpu.VMEM((tk, tq), f32), pltpu.VMEM((tk, tq), f32), pltpu.VMEM((s, LANES), bf16)],
        compiler_params=_params("parallel", "parallel", "arbitrary"),
        name="attn_prompt",
    )(q, kb, parts, vt)


def _attn_sample_kernel(q_ref, ck_ref, cv_ref, kn_ref, vn_ref, dc_ref, dn_ref, o_ref, *, t_new):
    rows_head = lambda shape: lax.shift_right_logical(lax.broadcasted_iota(jnp.int32, shape, 0), t_new.bit_length() - 1)
    per_query = lambda a: jnp.broadcast_to(a[:, None, :], (N_HEADS, t_new, a.shape[-1])).reshape(LANES, a.shape[-1])
    shape = (LANES, ATT_WIDTH)
    col_head = lax.shift_right_logical(lax.broadcasted_iota(jnp.int32, shape, 1), HEAD_DIM.bit_length() - 1)
    q_rows = jnp.tile(q_ref[0].astype(f32), (N_HEADS, 1))
    qbd = jnp.where(rows_head(shape) == col_head, q_rows, 0.0).astype(bf16)
    query = jnp.bitwise_and(lax.broadcasted_iota(jnp.int32, (LANES, LANES), 0), t_new - 1)
    new_key = lax.broadcasted_iota(jnp.int32, (LANES, LANES), 1)
    bias_n = jnp.where(new_key <= query, per_query(dn_ref[0]), NEG)
    s_c = jnp.dot(qbd, ck_ref[0].astype(bf16), preferred_element_type=f32) + per_query(dc_ref[0])
    s_n = jnp.dot(qbd, kn_ref[0], preferred_element_type=f32) + bias_n
    m = jnp.maximum(jnp.max(s_c, axis=-1, keepdims=True), jnp.max(s_n, axis=-1, keepdims=True))
    p_c = jnp.exp2(s_c - m)
    p_n = jnp.exp2(s_n - m)
    inv = 1.0 / (jnp.sum(p_c, axis=-1, keepdims=True) + jnp.sum(p_n, axis=-1, keepdims=True))
    p_c = (p_c * inv).astype(bf16)
    p_n = (p_n * inv).astype(bf16)
    over_positions = (((1,), (1,)), ((), ()))
    o_all = (lax.dot_general(cv_ref[0].astype(bf16), p_c, over_positions, preferred_element_type=f32)
             + lax.dot_general(vn_ref[0], p_n, over_positions, preferred_element_type=f32))
    shape = (ATT_WIDTH, LANES)
    row_head = lax.shift_right_logical(lax.broadcasted_iota(jnp.int32, shape, 0), HEAD_DIM.bit_length() - 1)
    col_head = lax.shift_right_logical(lax.broadcasted_iota(jnp.int32, shape, 1), t_new.bit_length() - 1)
    own = jnp.where(row_head == col_head, o_all, 0.0).astype(bf16)
    q_of_row = jnp.bitwise_and(lax.broadcasted_iota(jnp.int32, (LANES, LANES), 0), t_new - 1)
    pick = jnp.where(q_of_row == lax.broadcasted_iota(jnp.int32, (LANES, LANES), 1), 1.0, 0.0).astype(bf16)
    o_t = jnp.dot(own, pick, preferred_element_type=f32)
    o_ref[0] = o_t.T[0:t_new, :].astype(bf16)


def _attn_sample(q, ck, cv, kn, vn, decay_c, decay_n, t_new):
    b, _, p = ck.shape
    assert N_HEADS * t_new == LANES
    blk = lambda r, c: pl.BlockSpec((1, r, c), lambda i: (i, 0, 0))
    return pl.pallas_call(
        functools.partial(_attn_sample_kernel, t_new=t_new),
        grid=(b,),
        in_specs=[blk(t_new, ATT_WIDTH), blk(ATT_WIDTH, p), blk(ATT_WIDTH, p), blk(ATT_WIDTH, LANES),
                  blk(ATT_WIDTH, LANES), blk(N_HEADS, p), blk(N_HEADS, LANES)],
        out_specs=blk(t_new, ATT_WIDTH),
        out_shape=jax.ShapeDtypeStruct((b, t_new, ATT_WIDTH), bf16),
        compiler_params=_params("parallel"),
        name="attn_sample",
    )(q, ck, cv, kn, vn, decay_c, decay_n)


def _route(logits):
    lane = lax.broadcasted_iota(jnp.int32, logits.shape, 1).astype(f32)
    ninf = -jnp.inf
    gl = jnp.where(lane < N_GROUPS, logits, ninf)
    gmax = jnp.max(gl, axis=-1, keepdims=True)
    grp = jnp.min(jnp.where(gl == gmax, lane, float(LANES)), axis=-1, keepdims=True)
    p_sel = 1.0 / jnp.sum(jnp.exp(gl - gmax), axis=-1, keepdims=True)
    e_lo = EXPERT_LANE0 + grp * EXPERTS_PER_GROUP
    el = jnp.where(lane >= e_lo, jnp.where(lane < e_lo + EXPERTS_PER_GROUP, logits, ninf), ninf)
    t1 = jnp.max(el, axis=-1, keepdims=True)
    i1 = jnp.min(jnp.where(el == t1, lane, float(LANES)), axis=-1, keepdims=True)
    el2 = jnp.where(lane == i1, ninf, el)
    t2 = jnp.max(el2, axis=-1, keepdims=True)
    i2 = jnp.min(jnp.where(el2 == t2, lane, float(LANES)), axis=-1, keepdims=True)
    e2 = jnp.exp(t2 - t1)
    w1 = p_sel / (1.0 + e2)
    w2 = p_sel * e2 / (1.0 + e2)
    picks = (i1 - EXPERT_LANE0, i2 - EXPERT_LANE0, w1, w2)
    out = jnp.zeros(logits.shape, f32)
    for n, val in enumerate(picks):
        out = jnp.where(lane == float(n), val, out)
    return out


def _tail_kernel(x_ref, o_ref, u_ref, uprev_ref, ctx_ref, wdw_ref, bdw_ref, gcl_ref, bcl_ref,
                 wa_ref, wc_ref, wg_ref, wo_ref, g1_ref, b1_ref, wr_ref, br_ref,
                 h_ref, gates_ref, uext_sc, shift_sc, c_sc, *, cb, ct):
    i = pl.program_id(1)
    bb, t, _ = x_ref.shape
    n = bb * t

    x = x_ref[...].reshape(n, D_MODEL)
    g = _sigmoid(jnp.dot(x.astype(bf16), wg_ref[...], preferred_element_type=f32))
    branch_a = jnp.dot(o_ref[...].reshape(n, ATT_WIDTH), wa_ref[...], preferred_element_type=f32)

    uext_sc[:, 0:HALO, :] = jnp.where(i == 0, ctx_ref[...], uprev_ref[...])
    uext_sc[:, HALO:HALO + t, :] = u_ref[...]
    span = t + HALO - SUBLANES
    for r in range(1, SUBLANES):
        shift_sc[r - 1] = uext_sc[:, r:r + span, :]
    for b0 in range(0, bb, cb):
        for t0 in range(0, t, ct):
            groups = cb * ct // SUBLANES
            acc = jnp.zeros((groups, SUBLANES, D_CONV), f32) + bdw_ref[...]
            for k in range(CONV_WIDTH):
                a, r = divmod(HALO_PAD + k, SUBLANES)
                r0 = t0 + a * SUBLANES
                src = uext_sc[b0:b0 + cb, r0:r0 + ct, :] if r == 0 else shift_sc[r - 1, b0:b0 + cb, r0:r0 + ct, :]
                acc = acc + wdw_ref[k] * src.reshape(groups, SUBLANES, D_CONV)
            c_sc[b0:b0 + cb, t0:t0 + ct, :] = acc.reshape(cb, ct, D_CONV)

    c = _layernorm(c_sc[...].reshape(n, D_CONV), gcl_ref[...], bcl_ref[...])
    c = c * _sigmoid(c)
    branch_b = jnp.dot(c.astype(bf16), wc_ref[...], preferred_element_type=f32)
    merged = g[:, :D_MODEL] * branch_a + g[:, D_MODEL:] * branch_b
    mix = jnp.dot(merged.astype(bf16), wo_ref[...], preferred_element_type=f32)
    h = _layernorm(ALPHA * x + mix, g1_ref[...], b1_ref[...])
    h_ref[...] = h.reshape(bb, t, D_MODEL)
    w_hi, w_lo = _split2(wr_ref[...])
    w_parts = jnp.concatenate([w_hi, w_lo], axis=1)
    cross = sum(jnp.dot(part, w_parts, preferred_element_type=f32) for part in _split2(h))
    logits = cross[:, :LANES] + cross[:, LANES:] + br_ref[...]
    gates_ref[...] = _route(logits).reshape(bb, t, LANES)


def _tail(x, o, u, ctx, w, bb, t, cb, ct):
    b, s, _ = x.shape
    nt = s // t
    uprev = u if nt > 1 else ctx
    per_tile = t // HALO
    tile = lambda width: pl.BlockSpec((bb, t, width), lambda b_, i: (b_, i, 0))
    full = lambda a: pl.BlockSpec(a.shape, lambda b_, i: (0,) * a.ndim)
    weights = [w["w_dw"], w["b_dw"], w["g_cln"], w["b_cln"], w["w_attn_out"], w["w_conv_out"], w["w_gate"],
               w["w_out"], w["g_ln1"], w["b_ln1"], w["w_route"], w["b_route"]]
    return pl.pallas_call(
        functools.partial(_tail_kernel, cb=cb, ct=ct),
        grid=(b // bb, nt),
        in_specs=[tile(D_MODEL), tile(ATT_WIDTH), tile(D_CONV),
                  pl.BlockSpec((bb, HALO, D_CONV), lambda b_, i: (b_, jnp.maximum(i * per_tile - 1, 0), 0)),
                  pl.BlockSpec((bb, HALO, D_CONV), lambda b_, i: (b_, 0, 0))]
                 + [full(a) for a in weights],
        out_specs=[tile(D_MODEL), tile(LANES)],
        out_shape=[jax.ShapeDtypeStruct((b, s, D_MODEL), f32), jax.ShapeDtypeStruct((b, s, LANES), f32)],
        scratch_shapes=[pltpu.VMEM((bb, HALO + t, D_CONV), f32),
                        pltpu.VMEM((SUBLANES - 1, bb, t + HALO - SUBLANES, D_CONV), f32),
                        pltpu.VMEM((bb, t, D_CONV), f32)],
        compiler_params=_params("parallel", "arbitrary"),
        name="tail",
    )(x, o, u, uprev, ctx, *weights)


MOE_SUB = 256
MOE_CAP = 32
MOE_EPC = 8
MOE_KEY = 4096
_CAP_SHIFT = MOE_CAP.bit_length() - 1


def _moe_kernel(h_ref, r_ref, w1_ref, w3_ref, w2_ref, g2_ref, b2_ref, y_ref,
                acc_sc, hb_sc, key_sc, keyt_sc, nr_sc, *, n_sub):
    c = pl.program_id(1)
    slots = MOE_EPC * MOE_CAP

    @pl.when(c == 0)
    def _():
        acc_sc[...] = jnp.zeros(acc_sc.shape, f32)
        hb_sc[...] = h_ref[...].astype(bf16)
        lane = lax.broadcasted_iota(jnp.int32, (MOE_SUB, LANES), 1).astype(f32)
        ti = lax.broadcasted_iota(jnp.int32, (MOE_SUB, MOE_SUB), 0)
        tj = lax.broadcasted_iota(jnp.int32, (MOE_SUB, MOE_SUB), 1)
        earlier = jnp.where(tj < ti, 1.0, 0.0).astype(bf16)
        li = lax.broadcasted_iota(jnp.int32, (LANES, LANES), 0)
        from_lane = [jnp.where(li == n, 1.0, 0.0).astype(bf16) for n in range(2)]
        lane_sum = jnp.ones((LANES, LANES), bf16)
        spread = lambda x, m: jnp.dot(x.astype(bf16), m, preferred_element_type=f32)
        most = jnp.zeros((1, LANES), f32)
        for a in range(n_sub):
            r = r_ref[a * MOE_SUB:(a + 1) * MOE_SUB, :]
            picks = jnp.where(lane < 2.0, r, 0.0)
            e1, e2 = spread(picks, from_lane[0]), spread(picks, from_lane[1])
            oh1 = jnp.where(lane == e1, 1.0, 0.0)
            oh2 = jnp.where(lane == e2, 1.0, 0.0)
            both = oh1 + oh2
            before = jnp.dot(earlier, both.astype(bf16), preferred_element_type=f32)
            key1 = e1 * MOE_KEY + spread(before * oh1, lane_sum)
            key2 = e2 * MOE_KEY + spread(before * oh2, lane_sum)
            info = jnp.where(lane == 0.0, key1, jnp.where(lane == 1.0, key2, r))
            key_sc[a] = info
            keyt_sc[a] = info.T[0:8, :]
            most = jnp.maximum(most, jnp.sum(both, axis=0, keepdims=True))
        nr_sc[0] = (jnp.max(most).astype(jnp.int32) + (MOE_CAP - 1)) // MOE_CAP

    e0 = c * MOE_EPC
    s_row = lax.broadcasted_iota(jnp.int32, (slots, MOE_SUB), 0)
    s_col = lax.broadcasted_iota(jnp.int32, (MOE_SUB, slots), 1)

    def slot_key(s, rd):
        expert = lax.shift_right_logical(s, _CAP_SHIFT) + e0
        return (expert * MOE_KEY + jnp.bitwise_and(s, MOE_CAP - 1) + rd * MOE_CAP).astype(f32)

    def one_round(rd, carry):
        key_r = slot_key(s_row, rd)
        key_c = slot_key(s_col, rd)
        xs = []
        for a in range(n_sub):
            k1, k2 = keyt_sc[a, 0:1, :], keyt_sc[a, 1:2, :]
            p = jnp.where(key_r == k1, 1.0, jnp.where(key_r == k2, 1.0, 0.0)).astype(bf16)
            hb = hb_sc[a * MOE_SUB:(a + 1) * MOE_SUB, :]
            xs.append(jnp.dot(p, hb, preferred_element_type=f32).astype(bf16))
        ys = [[] for _ in range(n_sub)]
        for e in range(MOE_EPC):
            xe = jnp.concatenate([x[e * MOE_CAP:(e + 1) * MOE_CAP] for x in xs], axis=0)
            a1 = jnp.dot(xe, w1_ref[e], preferred_element_type=f32)
            a3 = jnp.dot(xe, w3_ref[e], preferred_element_type=f32)
            act = (a1 * _sigmoid(a1)) * a3
            ye = jnp.dot(act.astype(bf16), w2_ref[e], preferred_element_type=f32).astype(bf16)
            for a in range(n_sub):
                ys[a].append(ye[a * MOE_CAP:(a + 1) * MOE_CAP])
        for a in range(n_sub):
            info = key_sc[a]
            pw = jnp.where(key_c == info[:, 0:1], info[:, 2:3],
                           jnp.where(key_c == info[:, 1:2], info[:, 3:4], 0.0)).astype(bf16)
            rows = slice(a * MOE_SUB, (a + 1) * MOE_SUB)
            acc_sc[rows, :] += jnp.dot(pw, jnp.concatenate(ys[a], axis=0), preferred_element_type=f32)
        return carry

    lax.fori_loop(0, nr_sc[0], one_round, 0)

    @pl.when(c == pl.num_programs(1) - 1)
    def _():
        y_ref[...] = _layernorm(ALPHA * h_ref[...] + acc_sc[...], g2_ref[...], b2_ref[...])


def _moe(h2d, route, w1, w3, w2, g2, b2, tm):
    n = h2d.shape[0]
    n_sub = tm // MOE_SUB
    expert_block = lambda a: pl.BlockSpec((MOE_EPC,) + a.shape[1:], lambda i, c: (c, 0, 0))
    return pl.pallas_call(
        functools.partial(_moe_kernel, n_sub=n_sub),
        grid=(n // tm, N_EXPERTS // MOE_EPC),
        in_specs=[pl.BlockSpec((tm, D_MODEL), lambda i, c: (i, 0)),
                  pl.BlockSpec((tm, LANES), lambda i, c: (i, 0)),
                  expert_block(w1), expert_block(w3), expert_block(w2),
                  pl.BlockSpec((1, D_MODEL), lambda i, c: (0, 0)),
                  pl.BlockSpec((1, D_MODEL), lambda i, c: (0, 0))],
        out_specs=pl.BlockSpec((tm, D_MODEL), lambda i, c: (i, 0)),
        out_shape=jax.ShapeDtypeStruct((n, D_MODEL), f32),
        scratch_shapes=[pltpu.VMEM((tm, D_MODEL), f32), pltpu.VMEM((tm, D_MODEL), bf16),
                        pltpu.VMEM((n_sub, MOE_SUB, LANES), f32), pltpu.VMEM((n_sub, 8, MOE_SUB), f32),
                        pltpu.SMEM((1,), jnp.int32)],
        compiler_params=_params("parallel", "arbitrary"),
        name="moe",
    )(h2d, route, w1, w3, w2, g2, b2)


def _pack_weights(w_in, b_forget, w_dw, b_dw, g_conv_ln, b_conv_ln, w_attn_out, w_conv_out, w_out,
                  g_ln1, b_ln1, w_group, b_group, w_router, b_router, w1, w3, w2, g_ln2, b_ln2):
    a = ATT_WIDTH
    c_f, c_glu, c_gate = 3 * a, 3 * a + N_HEADS, 3 * a + N_HEADS + 2 * D_CONV
    w_f = jnp.pad(w_in[:, c_f:c_glu].astype(bf16), ((0, 0), (0, LANES - N_HEADS)))
    pad_route = LANES - N_GROUPS - N_EXPERTS
    row = lambda v: v.reshape(1, -1)
    return dict(
        w_proj=(w_in[:, :c_f].astype(bf16), w_in[:, c_glu:c_gate].astype(bf16), w_f),
        b_forget=jnp.pad(b_forget, (0, LANES - N_HEADS)).reshape(1, LANES),
        w_gate=w_in[:, c_gate:].astype(bf16),
        w_dw=jnp.broadcast_to(w_dw[:, None, :], (CONV_WIDTH, SUBLANES, D_CONV)),
        b_dw=jnp.broadcast_to(b_dw[None, :], (SUBLANES, D_CONV)),
        g_cln=row(g_conv_ln), b_cln=row(b_conv_ln),
        w_attn_out=w_attn_out.astype(bf16), w_conv_out=w_conv_out.astype(bf16), w_out=w_out.astype(bf16),
        g_ln1=row(g_ln1), b_ln1=row(b_ln1),
        w_route=jnp.pad(jnp.concatenate([w_group, w_router], axis=1), ((0, 0), (0, pad_route))),
        b_route=jnp.pad(jnp.concatenate([b_group, b_router]), (0, pad_route)).reshape(1, LANES),
        w1=w1.astype(bf16), w3=w3.astype(bf16), w2=w2.astype(bf16),
        g_ln2=row(g_ln2), b_ln2=row(b_ln2),
    )


def _finish(x, o, u, ctx, w, bb, t, cb, ct, tm_moe):
    b, s, _ = x.shape
    h, gates = _tail(x, o, u, ctx, w, bb, t, cb, ct)
    y = _moe(h.reshape(b * s, D_MODEL), gates.reshape(b * s, LANES), w["w1"], w["w3"], w["w2"],
             w["g_ln2"], w["b_ln2"], tm_moe)
    return y.reshape(b, s, D_MODEL)


def _layer(xp, xs, cache_k, cache_v, cache_logf, cache_conv, w):
    bp, sp, _ = xp.shape
    bs, ts, _ = xs.shape
    past = cache_k.shape[1]
    tq, tk = ATTN_Q_TILE, TOKEN_TILE

    q, k, kb, v, vt, vb, lf, u = _in_proj(xp.reshape(bp * sp, D_MODEL), w["w_proj"], w["b_forget"], tk, bp)
    logf = lf.transpose(0, 2, 1)
    parts = _cumsum_lanes(lf.reshape(bp * N_HEADS, sp), reverse=False, split_scale=-LOG2E)
    parts = parts.reshape(N_SPLIT, bp, ATT_WIDTH // LANES, 2, sp)
    shp = (bp, sp, ATT_WIDTH)
    o = _attn_prompt(q.reshape(shp), kb.reshape(shp), parts, vt, tq)
    u = u.reshape(bp, sp, D_CONV)
    yp = _finish(xp, o, u, jnp.zeros((bp, HALO, D_CONV), f32), w, 1, TOKEN_TILE, 1, CONV_ROWS, MOE_TILE)
    outs_p = (k.reshape(bp, sp, N_HEADS, HEAD_DIM), v.reshape(bp, sp, N_HEADS, HEAD_DIM), logf,
              u[:, sp - CONV_CTX:])

    q, k, kb, v, _, vb, lf, u = _in_proj(xs.reshape(bs * ts, D_MODEL), w["w_proj"], w["b_forget"], bs * ts, 1)
    lf = lf.reshape(N_HEADS, bs, ts)
    logf = lf.transpose(1, 2, 0)
    lf_t = jnp.pad(lf.transpose(1, 0, 2), ((0, 0), (0, 0), (0, LANES - ts)))
    cq = _cumsum_lanes(lf_t.reshape(bs * N_HEADS, LANES), reverse=False).reshape(bs, N_HEADS, LANES)
    r = _cumsum_lanes(cache_logf.transpose(0, 2, 1).reshape(bs * N_HEADS, past), reverse=True)
    decay_c = LOG2E * r.reshape(bs, N_HEADS, past)
    decay_n = -LOG2E * cq
    new_t = lambda a: jnp.pad(a.reshape(bs, ts, ATT_WIDTH).transpose(0, 2, 1), ((0, 0), (0, 0), (0, LANES - ts)))
    cache_t = lambda a: a.transpose(0, 2, 3, 1).reshape(bs, ATT_WIDTH, past)
    o = _attn_sample(q.reshape(bs, ts, ATT_WIDTH), cache_t(cache_k), cache_t(cache_v), new_t(kb), new_t(vb),
                     decay_c, decay_n, ts)
    u = u.reshape(bs, ts, D_CONV)
    ctx = jnp.pad(cache_conv, ((0, 0), (HALO_PAD, 0), (0, 0)))
    ys = _finish(xs, o, u, ctx, w, bs, ts, 2, ts, bs * ts)
    u_ext = jnp.concatenate([cache_conv, u], axis=1)
    outs_s = (k.reshape(bs, ts, N_HEADS, HEAD_DIM), v.reshape(bs, ts, N_HEADS, HEAD_DIM), logf,
              u_ext[:, u_ext.shape[1] - CONV_CTX:])
    return yp, ys, outs_p, outs_s


def kernel(x_prompt, x_sample, cache_k, cache_v, cache_logf, cache_conv, w_in, b_forget, w_dw, b_dw,
           g_conv_ln, b_conv_ln, w_attn_out, w_conv_out, w_out, g_ln1, b_ln1, w_group, b_group,
           w_router, b_router, w1, w3, w2, g_ln2, b_ln2):
    xp, xs = x_prompt, x_sample
    per_layer_p, per_layer_s = [], []
    for l in range(DEPTH):
        w = _pack_weights(w_in[l], b_forget[l], w_dw[l], b_dw[l], g_conv_ln[l], b_conv_ln[l], w_attn_out[l],
                          w_conv_out[l], w_out[l], g_ln1[l], b_ln1[l], w_group[l], b_group[l], w_router[l],
                          b_router[l], w1[l], w3[l], w2[l], g_ln2[l], b_ln2[l])
        xp, xs, outs_p, outs_s = _layer(xp, xs, cache_k[l], cache_v[l], cache_logf[l], cache_conv[l], w)
        per_layer_p.append(outs_p)
        per_layer_s.append(outs_s)
    stack = lambda outs, j: jnp.stack([o[j] for o in outs])
    return (xp, xs,
            stack(per_layer_p, 0), stack(per_layer_p, 1), stack(per_layer_p, 2), stack(per_layer_p, 3),
            stack(per_layer_s, 0), stack(per_layer_s, 1), stack(per_layer_s, 2), stack(per_layer_s, 3))
```

```python
import functools

import jax
import jax.numpy as jnp
from jax import lax
from jax.experimental import pallas as pl
from jax.experimental.pallas import tpu as pltpu

f32 = jnp.float32
bf16 = jnp.bfloat16

D_MODEL = 1024
N_HEADS = 8
HEAD_DIM = 64
ATT_WIDTH = N_HEADS * HEAD_DIM
D_CONV = 512
CONV_WIDTH = 31
CONV_CTX = CONV_WIDTH - 1
N_GROUPS = 4
EXPERTS_PER_GROUP = 8
N_EXPERTS = N_GROUPS * EXPERTS_PER_GROUP
D_EXPERT = 256
DEPTH = 1
ALPHA = (2 * DEPTH) ** 0.25
LN_EPS = 1e-5
NEG = -1e30
SCALE = HEAD_DIM ** -0.5
LOG2E = 1.4426950408889634

N_SPLIT = 3
LANES = 128
SUBLANES = 8
HALO = 32
HALO_PAD = HALO - CONV_CTX
EXPERT_LANE0 = N_GROUPS
VMEM_LIMIT = 56 * 1024 * 1024

TOKEN_TILE = 512
ATTN_Q_TILE = 1024
CONV_ROWS = 32
MOE_TILE = 1024

_C_Q, _C_K, _C_V, _C_GA, _C_GB, _C_F, _C_END = 0, 512, 1024, 1536, 2048, 2560, 2688


def _sigmoid(x):
    return 1.0 / (1.0 + jnp.exp(-x))


def _layernorm(x, g, b):
    mu = jnp.mean(x, axis=-1, keepdims=True)
    xc = x - mu
    var = jnp.mean(xc * xc, axis=-1, keepdims=True)
    return xc * lax.rsqrt(var + LN_EPS) * g + b


def _params(*sem):
    return pltpu.CompilerParams(dimension_semantics=sem, vmem_limit_bytes=VMEM_LIMIT)


def _store_head_major(ref, val):
    tokens = val.shape[0]
    for h in range(N_HEADS):
        ref[pl.ds(h, tokens, stride=N_HEADS), :] = val[:, h * HEAD_DIM:(h + 1) * HEAD_DIM]


def _in_proj_kernel(x_ref, w_ref, bf_ref, q_ref, k_ref, kb_ref, v_ref, vt_ref, vb_ref, lf_ref, u_ref, qt_ref):
    tm = x_ref.shape[0]
    xb = x_ref[...].astype(bf16)

    def proj(lo, hi):
        return jnp.dot(xb, w_ref[:, lo:hi], preferred_element_type=f32)

    q = proj(_C_Q, _C_K) * (SCALE * LOG2E)
    q_ref[...] = q.astype(bf16)
    qt_ref[0, :, 0] = q.T.astype(bf16).reshape(ATT_WIDTH // LANES, LANES, tm)
    k = proj(_C_K, _C_V)
    _store_head_major(k_ref, k)
    kb_ref[...] = k.astype(bf16)
    v = proj(_C_V, _C_GA)
    _store_head_major(v_ref, v)
    vb_ref[...] = v.astype(bf16)
    vt_ref[0, :, 0] = v.T.astype(bf16).reshape(ATT_WIDTH // LANES, LANES, tm)
    u_ref[...] = proj(_C_GA, _C_GB) * _sigmoid(proj(_C_GB, _C_F))
    z = proj(_C_F, _C_END) + bf_ref[...]
    log_f = jnp.minimum(z, 0.0) - jnp.log(1.0 + jnp.exp(-jnp.abs(z)))
    lf_ref[0] = log_f.T[0:N_HEADS, :]


def _in_proj(x2d, w_pack, bf_pad, tm, groups):
    n = x2d.shape[0]
    cols = n // groups
    tpg = cols // tm
    row = lambda width: pl.BlockSpec((tm, width), lambda i: (i, 0))
    heads = pl.BlockSpec((tm * N_HEADS, HEAD_DIM), lambda i: (i, 0))
    hp = ATT_WIDTH // LANES
    return pl.pallas_call(
        _in_proj_kernel,
        grid=(n // tm,),
        in_specs=[row(D_MODEL),
                  pl.BlockSpec((D_MODEL, _C_END), lambda i: (0, 0)),
                  pl.BlockSpec((1, LANES), lambda i: (0, 0))],
        out_specs=[row(ATT_WIDTH), heads, row(ATT_WIDTH), heads,
                   pl.BlockSpec((1, hp, 1, LANES, tm), lambda i: (i // tpg, 0, i % tpg, 0, 0)),
                   row(ATT_WIDTH),
                   pl.BlockSpec((1, N_HEADS, tm), lambda i: (i // tpg, 0, i % tpg)),
                   row(D_CONV),
                   pl.BlockSpec((1, hp, 1, LANES, tm), lambda i: (i // tpg, 0, i % tpg, 0, 0))],
        out_shape=[jax.ShapeDtypeStruct((n, ATT_WIDTH), bf16),
                   jax.ShapeDtypeStruct((n * N_HEADS, HEAD_DIM), f32),
                   jax.ShapeDtypeStruct((n, ATT_WIDTH), bf16),
                   jax.ShapeDtypeStruct((n * N_HEADS, HEAD_DIM), f32),
                   jax.ShapeDtypeStruct((groups, hp, tpg, LANES, tm), bf16),
                   jax.ShapeDtypeStruct((n, ATT_WIDTH), bf16),
                   jax.ShapeDtypeStruct((groups, N_HEADS, cols), f32),
                   jax.ShapeDtypeStruct((n, D_CONV), f32),
                   jax.ShapeDtypeStruct((groups, hp, tpg, LANES, tm), bf16)],
        compiler_params=_params("parallel"),
        name="in_proj",
    )(x2d, w_pack, bf_pad)


def _split2(x):
    hi = x.astype(bf16)
    return hi, (x - hi.astype(f32)).astype(bf16)


def _split3(x):
    hi = x.astype(bf16)
    r1 = x - hi.astype(f32)
    mid = r1.astype(bf16)
    lo = (r1 - mid.astype(f32)).astype(bf16)
    return hi, mid, lo


def _cumsum_kernel(x_ref, o_ref, *, reverse, split_scale):
    nc, rows, _ = x_ref.shape
    r = lax.broadcasted_iota(jnp.int32, (LANES, LANES), 0)
    c = lax.broadcasted_iota(jnp.int32, (LANES, LANES), 1)
    tri = jnp.where((r > c) if reverse else (r <= c), 1.0, 0.0).astype(bf16)
    carry = jnp.zeros((rows, 1), f32)
    for j in (range(nc - 1, -1, -1) if reverse else range(nc)):
        xj = x_ref[j]
        y = carry
        for part in _split3(xj):
            y = y + jnp.dot(part, tri, preferred_element_type=f32)
        if split_scale is None:
            o_ref[j] = y
        else:
            for n, part in enumerate(_split3(split_scale * y)):
                o_ref[n, j] = part.astype(f32)
        carry = carry + jnp.sum(xj, axis=-1, keepdims=True)


def _cumsum_lanes(x, reverse, split_scale=None):
    rows, s = x.shape
    nc = s // LANES
    xc = x.reshape(rows, nc, LANES).transpose(1, 0, 2)
    shape = (nc, rows, LANES) if split_scale is None else (N_SPLIT, nc, rows, LANES)
    out = pl.pallas_call(
        functools.partial(_cumsum_kernel, reverse=reverse, split_scale=split_scale),
        out_shape=jax.ShapeDtypeStruct(shape, f32),
        name="cumsum_rev" if reverse else "cumsum_fwd",
    )(xc)
    if split_scale is None:
        return out.transpose(1, 0, 2).reshape(rows, s)
    return out.transpose(0, 2, 1, 3).reshape(N_SPLIT, rows, s)


def _attn_kernel(q_ref, k_ref, p_ref, vt_ref, o_ref, qx_sc, acc_sc, s0_sc, s1_sc, e_sc, *, tq, tk):
    i = pl.program_id(2)
    per_q = tq // tk

    @pl.when(i == 0)
    def _():
        row_k = lax.broadcasted_iota(jnp.int32, (LANES, tk), 0)
        for c in range(e_sc.shape[0] // tk):
            chans = jnp.zeros((LANES, tk), f32)
            for hh in range(2):
                for n in range(N_SPLIT):
                    part = p_ref[n, 0, 0, hh:hh + 1, c * tk:(c + 1) * tk]
                    chans = jnp.where(row_k == N_SPLIT * hh + n, part, chans)
            e_sc[c * tk:(c + 1) * tk, :] = chans.T.astype(bf16)

    qt = jnp.concatenate([q_ref[0, 0, d] for d in range(per_q)], axis=1).astype(f32)
    row = lax.broadcasted_iota(jnp.int32, (LANES, tq), 0)
    for h in range(2):
        qx_sc[h, 0:LANES, :] = jnp.where(row // HEAD_DIM == h, qt, 0.0).astype(bf16)
        qx_sc[h, LANES:2 * LANES, :] = jnp.where(row // N_SPLIT == h, 1.0, 0.0).astype(bf16)
    acc_sc[...] = jnp.zeros(acc_sc.shape, f32)
    ones_rows = jnp.ones((16, tk), bf16)

    s_sc = (s0_sc, s1_sc)

    def scores(j, h, lo=0):
        start = pl.multiple_of(j * tk, tk)
        kx = jnp.concatenate([k_ref[0, pl.ds(start, tk), :], e_sc[pl.ds(start, tk), :]], axis=1)
        s_sc[h][:, lo:] = jnp.dot(kx, qx_sc[h, :, lo:], preferred_element_type=f32)

    def absorb(j, h, stats, lo=0, masked=False):
        m_all, l_all = stats
        m_prev, l_prev = m_all[:, lo:], l_all[:, lo:]
        s = s_sc[h][:, lo:]
        if masked:
            key = lax.broadcasted_iota(jnp.int32, s.shape, 0)
            qry = lax.broadcasted_iota(jnp.int32, s.shape, 1)
            s = jnp.where(key <= qry, s, NEG)
        m_new = jnp.maximum(m_prev, jnp.max(s, axis=0, keepdims=True))
        alpha = jnp.exp2(m_prev - m_new)
        p = jnp.exp2(s - m_new).astype(bf16)
        vx = jnp.concatenate([vt_ref[0, 0, j, h * HEAD_DIM:(h + 1) * HEAD_DIM, :], ones_rows], axis=0)
        pv = jnp.dot(vx, p, preferred_element_type=f32)
        acc_sc[h, :, lo:] = alpha * acc_sc[h, :, lo:] + pv[0:HEAD_DIM, :]
        l_new = alpha * l_prev + pv[HEAD_DIM:HEAD_DIM + 1, :]
        if lo:
            m_new = jnp.concatenate([m_all[:, :lo], m_new], axis=1)
            l_new = jnp.concatenate([l_all[:, :lo], l_new], axis=1)
        return m_new, l_new

    def body(j, stats):
        scores(j, 1)
        st0 = absorb(j, 0, stats[0])
        scores(j + 1, 0)
        st1 = absorb(j, 1, stats[1])
        return st0, st1

    init = tuple((jnp.full((1, tq), NEG, f32), jnp.zeros((1, tq), f32)) for _ in range(2))
    scores(0, 0)
    stats = lax.fori_loop(0, (per_q // 2) * i, lambda p, st: body(2 * p + 1, body(2 * p, st)), init)
    st0, st1 = stats
    j0 = per_q * i
    for d in range(per_q):
        scores(j0 + d, 1, d * tk)
        st0 = absorb(j0 + d, 0, st0, d * tk, masked=True)
        if d + 1 < per_q:
            scores(j0 + d + 1, 0, (d + 1) * tk)
        st1 = absorb(j0 + d, 1, st1, d * tk, masked=True)
    l0, l1 = st0[1], st1[1]
    ot = jnp.concatenate([acc_sc[0] / l0, acc_sc[1] / l1], axis=0)
    o_ref[0] = ot.T.astype(bf16)


def _attn_prompt(qt, kb, parts, vt, tq):
    b, s, _ = kb.shape
    hp = ATT_WIDTH // LANES
    nq = s // tq
    nk, tk = vt.shape[2], vt.shape[4]
    assert tq % (2 * tk) == 0
    return pl.pallas_call(
        functools.partial(_attn_kernel, tq=tq, tk=tk),
        grid=(b, hp, nq),
        in_specs=[pl.BlockSpec((1, 1, tq // tk, LANES, tk), lambda b_, h_, i: (b_, h_, i, 0, 0)),
                  pl.BlockSpec((1, s, LANES), lambda b_, h_, i: (b_, 0, h_)),
                  pl.BlockSpec((N_SPLIT, 1, 1, 2, s), lambda b_, h_, i: (0, b_, h_, 0, 0)),
                  pl.BlockSpec((1, 1, nk, LANES, tk), lambda b_, h_, i: (b_, h_, 0, 0, 0))],
        out_specs=pl.BlockSpec((1, tq, LANES), lambda b_, h_, i: (b_, i, h_)),
        out_shape=jax.ShapeDtypeStruct((b, s, ATT_WIDTH), bf16),
        scratch_shapes=[pltpu.VMEM((2, 2 * LANES, tq), bf16), pltpu.VMEM((2, HEAD_DIM, tq), f32),
                        pltpu.VMEM((tk, tq), f32), pltpu.VMEM((tk, tq), f32), pltpu.VMEM((s, LANES), bf16)],
        compiler_params=_params("parallel", "parallel", "arbitrary"),
        name="attn_prompt",
    )(qt, kb, parts, vt)


def _attn_sample_kernel(q_ref, ck_ref, cv_ref, kn_ref, vn_ref, dc_ref, dn_ref, o_ref, *, t_new):
    rows_head = lambda shape: lax.shift_right_logical(lax.broadcasted_iota(jnp.int32, shape, 0), t_new.bit_length() - 1)
    per_query = lambda a: jnp.broadcast_to(a[:, None, :], (N_HEADS, t_new, a.shape[-1])).reshape(LANES, a.shape[-1])
    shape = (LANES, ATT_WIDTH)
    col_head = lax.shift_right_logical(lax.broadcasted_iota(jnp.int32, shape, 1), HEAD_DIM.bit_length() - 1)
    q_rows = jnp.tile(q_ref[0].astype(f32), (N_HEADS, 1))
    qbd = jnp.where(rows_head(shape) == col_head, q_rows, 0.0).astype(bf16)
    query = jnp.bitwise_and(lax.broadcasted_iota(jnp.int32, (LANES, LANES), 0), t_new - 1)
    new_key = lax.broadcasted_iota(jnp.int32, (LANES, LANES), 1)
    bias_n = jnp.where(new_key <= query, per_query(dn_ref[0]), NEG)
    s_c = jnp.dot(qbd, ck_ref[0].astype(bf16), preferred_element_type=f32) + per_query(dc_ref[0])
    s_n = jnp.dot(qbd, kn_ref[0], preferred_element_type=f32) + bias_n
    m = jnp.maximum(jnp.max(s_c, axis=-1, keepdims=True), jnp.max(s_n, axis=-1, keepdims=True))
    p_c = jnp.exp2(s_c - m)
    p_n = jnp.exp2(s_n - m)
    inv = 1.0 / (jnp.sum(p_c, axis=-1, keepdims=True) + jnp.sum(p_n, axis=-1, keepdims=True))
    p_c = (p_c * inv).astype(bf16)
    p_n = (p_n * inv).astype(bf16)
    over_positions = (((1,), (1,)), ((), ()))
    o_all = (lax.dot_general(cv_ref[0].astype(bf16), p_c, over_positions, preferred_element_type=f32)
             + lax.dot_general(vn_ref[0], p_n, over_positions, preferred_element_type=f32))
    shape = (ATT_WIDTH, LANES)
    row_head = lax.shift_right_logical(lax.broadcasted_iota(jnp.int32, shape, 0), HEAD_DIM.bit_length() - 1)
    col_head = lax.shift_right_logical(lax.broadcasted_iota(jnp.int32, shape, 1), t_new.bit_length() - 1)
    own = jnp.where(row_head == col_head, o_all, 0.0).astype(bf16)
    q_of_row = jnp.bitwise_and(lax.broadcasted_iota(jnp.int32, (LANES, LANES), 0), t_new - 1)
    pick = jnp.where(q_of_row == lax.broadcasted_iota(jnp.int32, (LANES, LANES), 1), 1.0, 0.0).astype(bf16)
    o_t = jnp.dot(own, pick, preferred_element_type=f32)
    o_ref[0] = o_t.T[0:t_new, :].astype(bf16)


def _attn_sample(q, ck, cv, kn, vn, decay_c, decay_n, t_new):
    b, _, p = ck.shape
    assert N_HEADS * t_new == LANES
    blk = lambda r, c: pl.BlockSpec((1, r, c), lambda i: (i, 0, 0))
    return pl.pallas_call(
        functools.partial(_attn_sample_kernel, t_new=t_new),
        grid=(b,),
        in_specs=[blk(t_new, ATT_WIDTH), blk(ATT_WIDTH, p), blk(ATT_WIDTH, p), blk(ATT_WIDTH, LANES),
                  blk(ATT_WIDTH, LANES), blk(N_HEADS, p), blk(N_HEADS, LANES)],
        out_specs=blk(t_new, ATT_WIDTH),
        out_shape=jax.ShapeDtypeStruct((b, t_new, ATT_WIDTH), bf16),
        compiler_params=_params("parallel"),
        name="attn_sample",
    )(q, ck, cv, kn, vn, decay_c, decay_n)


def _route(logits):
    lane = lax.broadcasted_iota(jnp.int32, logits.shape, 1).astype(f32)
    ninf = -jnp.inf
    gl = jnp.where(lane < N_GROUPS, logits, ninf)
    gmax = jnp.max(gl, axis=-1, keepdims=True)
    grp = jnp.min(jnp.where(gl == gmax, lane, float(LANES)), axis=-1, keepdims=True)
    p_sel = 1.0 / jnp.sum(jnp.exp(gl - gmax), axis=-1, keepdims=True)
    e_lo = EXPERT_LANE0 + grp * EXPERTS_PER_GROUP
    el = jnp.where(lane >= e_lo, jnp.where(lane < e_lo + EXPERTS_PER_GROUP, logits, ninf), ninf)
    t1 = jnp.max(el, axis=-1, keepdims=True)
    i1 = jnp.min(jnp.where(el == t1, lane, float(LANES)), axis=-1, keepdims=True)
    el2 = jnp.where(lane == i1, ninf, el)
    t2 = jnp.max(el2, axis=-1, keepdims=True)
    i2 = jnp.min(jnp.where(el2 == t2, lane, float(LANES)), axis=-1, keepdims=True)
    e2 = jnp.exp(t2 - t1)
    w1 = p_sel / (1.0 + e2)
    w2 = p_sel * e2 / (1.0 + e2)
    picks = (i1 - EXPERT_LANE0, i2 - EXPERT_LANE0, w1, w2)
    out = jnp.zeros(logits.shape, f32)
    for n, val in enumerate(picks):
        out = jnp.where(lane == float(n), val, out)
    return out


def _tail_kernel(x_ref, o_ref, u_ref, uprev_ref, ctx_ref, wdw_ref, bdw_ref, gcl_ref, bcl_ref,
                 wa_ref, wc_ref, wg_ref, wo_ref, g1_ref, b1_ref, wr_ref, br_ref,
                 h_ref, gates_ref, uext_sc, shift_sc, c_sc, *, cb, ct):
    i = pl.program_id(1)
    bb, t, _ = x_ref.shape
    n = bb * t

    x = x_ref[...].reshape(n, D_MODEL)
    g = _sigmoid(jnp.dot(x.astype(bf16), wg_ref[...], preferred_element_type=f32))
    branch_a = jnp.dot(o_ref[...].reshape(n, ATT_WIDTH), wa_ref[...], preferred_element_type=f32)

    uext_sc[:, 0:HALO, :] = jnp.where(i == 0, ctx_ref[...], uprev_ref[...])
    uext_sc[:, HALO:HALO + t, :] = u_ref[...]
    span = t + HALO - SUBLANES
    for r in range(1, SUBLANES):
        shift_sc[r - 1] = uext_sc[:, r:r + span, :]
    for b0 in range(0, bb, cb):
        for t0 in range(0, t, ct):
            groups = cb * ct // SUBLANES
            acc = jnp.zeros((groups, SUBLANES, D_CONV), f32) + bdw_ref[...]
            for k in range(CONV_WIDTH):
                a, r = divmod(HALO_PAD + k, SUBLANES)
                r0 = t0 + a * SUBLANES
                src = uext_sc[b0:b0 + cb, r0:r0 + ct, :] if r == 0 else shift_sc[r - 1, b0:b0 + cb, r0:r0 + ct, :]
                acc = acc + wdw_ref[k] * src.reshape(groups, SUBLANES, D_CONV)
            c_sc[b0:b0 + cb, t0:t0 + ct, :] = acc.reshape(cb, ct, D_CONV)

    c = _layernorm(c_sc[...].reshape(n, D_CONV), gcl_ref[...], bcl_ref[...])
    c = c * _sigmoid(c)
    branch_b = jnp.dot(c.astype(bf16), wc_ref[...], preferred_element_type=f32)
    merged = g[:, :D_MODEL] * branch_a + g[:, D_MODEL:] * branch_b
    mix = jnp.dot(merged.astype(bf16), wo_ref[...], preferred_element_type=f32)
    h = _layernorm(ALPHA * x + mix, g1_ref[...], b1_ref[...])
    h_ref[...] = h.reshape(bb, t, D_MODEL)
    w_hi, w_lo = _split2(wr_ref[...])
    w_parts = jnp.concatenate([w_hi, w_lo], axis=1)
    cross = sum(jnp.dot(part, w_parts, preferred_element_type=f32) for part in _split2(h))
    logits = cross[:, :LANES] + cross[:, LANES:] + br_ref[...]
    gates_ref[...] = _route(logits).reshape(bb, t, LANES)


def _tail(x, o, u, ctx, w, bb, t, cb, ct):
    b, s, _ = x.shape
    nt = s // t
    uprev = u if nt > 1 else ctx
    per_tile = t // HALO
    tile = lambda width: pl.BlockSpec((bb, t, width), lambda b_, i: (b_, i, 0))
    full = lambda a: pl.BlockSpec(a.shape, lambda b_, i: (0,) * a.ndim)
    weights = [w["w_dw"], w["b_dw"], w["g_cln"], w["b_cln"], w["w_attn_out"], w["w_conv_out"], w["w_gate"],
               w["w_out"], w["g_ln1"], w["b_ln1"], w["w_route"], w["b_route"]]
    return pl.pallas_call(
        functools.partial(_tail_kernel, cb=cb, ct=ct),
        grid=(b // bb, nt),
        in_specs=[tile(D_MODEL), tile(ATT_WIDTH), tile(D_CONV),
                  pl.BlockSpec((bb, HALO, D_CONV), lambda b_, i: (b_, jnp.maximum(i * per_tile - 1, 0), 0)),
                  pl.BlockSpec((bb, HALO, D_CONV), lambda b_, i: (b_, 0, 0))]
                 + [full(a) for a in weights],
        out_specs=[tile(D_MODEL), tile(LANES)],
        out_shape=[jax.ShapeDtypeStruct((b, s, D_MODEL), f32), jax.ShapeDtypeStruct((b, s, LANES), f32)],
        scratch_shapes=[pltpu.VMEM((bb, HALO + t, D_CONV), f32),
                        pltpu.VMEM((SUBLANES - 1, bb, t + HALO - SUBLANES, D_CONV), f32),
                        pltpu.VMEM((bb, t, D_CONV), f32)],
        compiler_params=_params("parallel", "arbitrary"),
        name="tail",
    )(x, o, u, uprev, ctx, *weights)


MOE_SUB = 256
MOE_CAP = 32
MOE_EPC = 4
MOE_KEY = 4096
_CAP_SHIFT = MOE_CAP.bit_length() - 1


def _moe_kernel(h_ref, r_ref, w1_ref, w3_ref, w2_ref, g2_ref, b2_ref, y_ref,
                acc_sc, hb_sc, key_sc, keyt_sc, nr_sc, *, n_sub):
    c = pl.program_id(1)
    slots = MOE_EPC * MOE_CAP

    @pl.when(c == 0)
    def _():
        acc_sc[...] = jnp.zeros(acc_sc.shape, f32)
        hb_sc[...] = h_ref[...].astype(bf16)
        lane = lax.broadcasted_iota(jnp.int32, (MOE_SUB, LANES), 1).astype(f32)
        ti = lax.broadcasted_iota(jnp.int32, (MOE_SUB, MOE_SUB), 0)
        tj = lax.broadcasted_iota(jnp.int32, (MOE_SUB, MOE_SUB), 1)
        earlier = jnp.where(tj < ti, 1.0, 0.0).astype(bf16)
        li = lax.broadcasted_iota(jnp.int32, (LANES, LANES), 0)
        from_lane = [jnp.where(li == n, 1.0, 0.0).astype(bf16) for n in range(2)]
        lane_sum = jnp.ones((LANES, LANES), bf16)
        spread = lambda x, m: jnp.dot(x.astype(bf16), m, preferred_element_type=f32)
        most = jnp.zeros((1, LANES), f32)
        for a in range(n_sub):
            r = r_ref[a * MOE_SUB:(a + 1) * MOE_SUB, :]
            picks = jnp.where(lane < 2.0, r, 0.0)
            e1, e2 = spread(picks, from_lane[0]), spread(picks, from_lane[1])
            oh1 = jnp.where(lane == e1, 1.0, 0.0)
            oh2 = jnp.where(lane == e2, 1.0, 0.0)
            both = oh1 + oh2
            before = jnp.dot(earlier, both.astype(bf16), preferred_element_type=f32)
            key1 = e1 * MOE_KEY + spread(before * oh1, lane_sum)
            key2 = e2 * MOE_KEY + spread(before * oh2, lane_sum)
            info = jnp.where(lane == 0.0, key1, jnp.where(lane == 1.0, key2, r))
            key_sc[a] = info
            keyt_sc[a] = info.T[0:8, :]
            most = jnp.maximum(most, jnp.sum(both, axis=0, keepdims=True))
        nr_sc[0] = (jnp.max(most).astype(jnp.int32) + (MOE_CAP - 1)) // MOE_CAP

    e0 = c * MOE_EPC
    s_row = lax.broadcasted_iota(jnp.int32, (slots, MOE_SUB), 0)
    s_col = lax.broadcasted_iota(jnp.int32, (MOE_SUB, slots), 1)

    def slot_key(s, rd):
        expert = lax.shift_right_logical(s, _CAP_SHIFT) + e0
        return (expert * MOE_KEY + jnp.bitwise_and(s, MOE_CAP - 1) + rd * MOE_CAP).astype(f32)

    def one_round(rd, carry):
        key_r = slot_key(s_row, rd)
        key_c = slot_key(s_col, rd)
        xs = []
        for a in range(n_sub):
            k1, k2 = keyt_sc[a, 0:1, :], keyt_sc[a, 1:2, :]
            p = jnp.where(key_r == k1, 1.0, jnp.where(key_r == k2, 1.0, 0.0)).astype(bf16)
            hb = hb_sc[a * MOE_SUB:(a + 1) * MOE_SUB, :]
            xs.append(jnp.dot(p, hb, preferred_element_type=f32).astype(bf16))
        ys = [[] for _ in range(n_sub)]
        for e in range(MOE_EPC):
            xe = jnp.concatenate([x[e * MOE_CAP:(e + 1) * MOE_CAP] for x in xs], axis=0)
            a1 = jnp.dot(xe, w1_ref[e], preferred_element_type=f32)
            a3 = jnp.dot(xe, w3_ref[e], preferred_element_type=f32)
            act = (a1 * _sigmoid(a1)) * a3
            ye = jnp.dot(act.astype(bf16), w2_ref[e], preferred_element_type=f32).astype(bf16)
            for a in range(n_sub):
                ys[a].append(ye[a * MOE_CAP:(a + 1) * MOE_CAP])
        for a in range(n_sub):
            info = key_sc[a]
            pw = jnp.where(key_c == info[:, 0:1], info[:, 2:3],
                           jnp.where(key_c == info[:, 1:2], info[:, 3:4], 0.0)).astype(bf16)
            rows = slice(a * MOE_SUB, (a + 1) * MOE_SUB)
            acc_sc[rows, :] += jnp.dot(pw, jnp.concatenate(ys[a], axis=0), preferred_element_type=f32)
        return carry

    lax.fori_loop(0, nr_sc[0], one_round, 0)

    @pl.when(c == pl.num_programs(1) - 1)
    def _():
        y_ref[...] = _layernorm(ALPHA * h_ref[...] + acc_sc[...], g2_ref[...], b2_ref[...])


def _moe(h2d, route, w1, w3, w2, g2, b2, tm):
    n = h2d.shape[0]
    n_sub = tm // MOE_SUB
    expert_block = lambda a: pl.BlockSpec((MOE_EPC,) + a.shape[1:], lambda i, c: (c, 0, 0))
    return pl.pallas_call(
        functools.partial(_moe_kernel, n_sub=n_sub),
        grid=(n // tm, N_EXPERTS // MOE_EPC),
        in_specs=[pl.BlockSpec((tm, D_MODEL), lambda i, c: (i, 0)),
                  pl.BlockSpec((tm, LANES), lambda i, c: (i, 0)),
                  expert_block(w1), expert_block(w3), expert_block(w2),
                  pl.BlockSpec((1, D_MODEL), lambda i, c: (0, 0)),
                  pl.BlockSpec((1, D_MODEL), lambda i, c: (0, 0))],
        out_specs=pl.BlockSpec((tm, D_MODEL), lambda i, c: (i, 0)),
        out_shape=jax.ShapeDtypeStruct((n, D_MODEL), f32),
        scratch_shapes=[pltpu.VMEM((tm, D_MODEL), f32), pltpu.VMEM((tm, D_MODEL), bf16),
                        pltpu.VMEM((n_sub, MOE_SUB, LANES), f32), pltpu.VMEM((n_sub, 8, MOE_SUB), f32),
                        pltpu.SMEM((1,), jnp.int32)],
        compiler_params=_params("parallel", "arbitrary"),
        name="moe",
    )(h2d, route, w1, w3, w2, g2, b2)


def _pack_weights(w_in, b_forget, w_dw, b_dw, g_conv_ln, b_conv_ln, w_attn_out, w_conv_out, w_out,
                  g_ln1, b_ln1, w_group, b_group, w_router, b_router, w1, w3, w2, g_ln2, b_ln2):
    a = ATT_WIDTH
    c_f, c_glu, c_gate = 3 * a, 3 * a + N_HEADS, 3 * a + N_HEADS + 2 * D_CONV
    w_f = jnp.pad(w_in[:, c_f:c_glu], ((0, 0), (0, LANES - N_HEADS)))
    w_pack = jnp.concatenate([w_in[:, :c_f], w_in[:, c_glu:c_gate], w_f], axis=1).astype(bf16)
    pad_route = LANES - N_GROUPS - N_EXPERTS
    row = lambda v: v.reshape(1, -1)
    return dict(
        w_pack=w_pack,
        b_forget=jnp.pad(b_forget, (0, LANES - N_HEADS)).reshape(1, LANES),
        w_gate=w_in[:, c_gate:].astype(bf16),
        w_dw=jnp.broadcast_to(w_dw[:, None, :], (CONV_WIDTH, SUBLANES, D_CONV)),
        b_dw=jnp.broadcast_to(b_dw[None, :], (SUBLANES, D_CONV)),
        g_cln=row(g_conv_ln), b_cln=row(b_conv_ln),
        w_attn_out=w_attn_out.astype(bf16), w_conv_out=w_conv_out.astype(bf16), w_out=w_out.astype(bf16),
        g_ln1=row(g_ln1), b_ln1=row(b_ln1),
        w_route=jnp.pad(jnp.concatenate([w_group, w_router], axis=1), ((0, 0), (0, pad_route))),
        b_route=jnp.pad(jnp.concatenate([b_group, b_router]), (0, pad_route)).reshape(1, LANES),
        w1=w1.astype(bf16), w3=w3.astype(bf16), w2=w2.astype(bf16),
        g_ln2=row(g_ln2), b_ln2=row(b_ln2),
    )


def _finish(x, o, u, ctx, w, bb, t, cb, ct, tm_moe):
    b, s, _ = x.shape
    h, gates = _tail(x, o, u, ctx, w, bb, t, cb, ct)
    y = _moe(h.reshape(b * s, D_MODEL), gates.reshape(b * s, LANES), w["w1"], w["w3"], w["w2"],
             w["g_ln2"], w["b_ln2"], tm_moe)
    return y.reshape(b, s, D_MODEL)


def _layer(xp, xs, cache_k, cache_v, cache_logf, cache_conv, w):
    bp, sp, _ = xp.shape
    bs, ts, _ = xs.shape
    past = cache_k.shape[1]
    tq, tk = ATTN_Q_TILE, TOKEN_TILE

    _, k, kb, v, vt, vb, lf, u, qt = _in_proj(xp.reshape(bp * sp, D_MODEL), w["w_pack"], w["b_forget"], tk, bp)
    logf = lf.transpose(0, 2, 1)
    parts = _cumsum_lanes(lf.reshape(bp * N_HEADS, sp), reverse=False, split_scale=-LOG2E)
    parts = parts.reshape(N_SPLIT, bp, ATT_WIDTH // LANES, 2, sp)
    shp = (bp, sp, ATT_WIDTH)
    o = _attn_prompt(qt, kb.reshape(shp), parts, vt, tq)
    u = u.reshape(bp, sp, D_CONV)
    yp = _finish(xp, o, u, jnp.zeros((bp, HALO, D_CONV), f32), w, 1, TOKEN_TILE, 1, CONV_ROWS, MOE_TILE)
    outs_p = (k.reshape(bp, sp, N_HEADS, HEAD_DIM), v.reshape(bp, sp, N_HEADS, HEAD_DIM), logf,
              u[:, sp - CONV_CTX:])

    q, k, kb, v, _, vb, lf, u, _ = _in_proj(xs.reshape(bs * ts, D_MODEL), w["w_pack"], w["b_forget"], bs * ts, 1)
    lf = lf.reshape(N_HEADS, bs, ts)
    logf = lf.transpose(1, 2, 0)
    lf_t = jnp.pad(lf.transpose(1, 0, 2), ((0, 0), (0, 0), (0, LANES - ts)))
    cq = _cumsum_lanes(lf_t.reshape(bs * N_HEADS, LANES), reverse=False).reshape(bs, N_HEADS, LANES)
    r = _cumsum_lanes(cache_logf.transpose(0, 2, 1).reshape(bs * N_HEADS, past), reverse=True)
    decay_c = LOG2E * r.reshape(bs, N_HEADS, past)
    decay_n = -LOG2E * cq
    new_t = lambda a: jnp.pad(a.reshape(bs, ts, ATT_WIDTH).transpose(0, 2, 1), ((0, 0), (0, 0), (0, LANES - ts)))
    cache_t = lambda a: a.transpose(0, 2, 3, 1).reshape(bs, ATT_WIDTH, past)
    o = _attn_sample(q.reshape(bs, ts, ATT_WIDTH), cache_t(cache_k), cache_t(cache_v), new_t(kb), new_t(vb),
                     decay_c, decay_n, ts)
    u = u.reshape(bs, ts, D_CONV)
    ctx = jnp.pad(cache_conv, ((0, 0), (HALO_PAD, 0), (0, 0)))
    ys = _finish(xs, o, u, ctx, w, bs, ts, 2, ts, bs * ts)
    u_ext = jnp.concatenate([cache_conv, u], axis=1)
    outs_s = (k.reshape(bs, ts, N_HEADS, HEAD_DIM), v.reshape(bs, ts, N_HEADS, HEAD_DIM), logf,
              u_ext[:, u_ext.shape[1] - CONV_CTX:])
    return yp, ys, outs_p, outs_s


def kernel(x_prompt, x_sample, cache_k, cache_v, cache_logf, cache_conv, w_in, b_forget, w_dw, b_dw,
           g_conv_ln, b_conv_ln, w_attn_out, w_conv_out, w_out, g_ln1, b_ln1, w_group, b_group,
           w_router, b_router, w1, w3, w2, g_ln2, b_ln2):
    xp, xs = x_prompt, x_sample
    per_layer_p, per_layer_s = [], []
    for l in range(DEPTH):
        w = _pack_weights(w_in[l], b_forget[l], w_dw[l], b_dw[l], g_conv_ln[l], b_conv_ln[l], w_attn_out[l],
                          w_conv_out[l], w_out[l], g_ln1[l], b_ln1[l], w_group[l], b_group[l], w_router[l],
                          b_router[l], w1[l], w3[l], w2[l], g_ln2[l], b_ln2[l])
        xp, xs, outs_p, outs_s = _layer(xp, xs, cache_k[l], cache_v[l], cache_logf[l], cache_conv[l], w)
        per_layer_p.append(outs_p)
        per_layer_s.append(outs_s)
    stack = lambda outs, j: jnp.stack([o[j] for o in outs])
    return (xp, xs,
            stack(per_layer_p, 0), stack(per_layer_p, 1), stack(per_layer_p, 2), stack(per_layer_p, 3),
            stack(per_layer_s, 0), stack(per_layer_s, 1), stack(per_layer_s, 2), stack(per_layer_s, 3))
```
